```python
import math
import jax
import jax.numpy as jnp
from jax import lax
import numpy as np

D_MODEL = 1024
BATCH = 16
SEQ = 2048
DEPTH = 4
DEC_BATCH = 2
DEC_SEQ = 8192
PAST_LEN = 128

ATTN_HEADS = 8
ATTN_KV_HEADS = 2
HEAD_DIM = 64
ATTN_Q_WIDTH = ATTN_HEADS * HEAD_DIM
ATTN_KV_WIDTH = ATTN_KV_HEADS * HEAD_DIM
WINDOW = 128
ATTN_BLOCK = 128
NUM_BUCKETS = 32
MAX_DISTANCE = 128

GMLP_WIDTH = 512
GMLP_GROUPS = 4
GMLP_CHUNK = 128

SSD_INNER = 1024
SSD_HEAD_DIM = 64
SSD_HEADS = SSD_INNER // SSD_HEAD_DIM
SSD_GROUPS = 2
SSD_STATE = 128
SSD_CONV = 3
SSD_CHUNK = 128
SSD_XBC = SSD_INNER + 2 * SSD_GROUPS * SSD_STATE
N_DIR = 2

N_BRANCH = 3
FF_DIM = 4 * D_MODEL
EPS = 1e-6
NEG_INF = -1e30

IN_WIDTHS = (ATTN_Q_WIDTH, ATTN_KV_WIDTH, ATTN_KV_WIDTH, GMLP_WIDTH, GMLP_WIDTH,
             SSD_INNER, SSD_XBC, N_DIR * SSD_HEADS, N_BRANCH * D_MODEL)
IN_PROJ = sum(IN_WIDTHS)

kernel_name = 'hybrid_bidir_encoder'


def _split_points(widths):
    pts, acc = [], 0
    for w in widths[:-1]:
        acc += w
        pts.append(acc)
    return pts


def rms_f32(x, g):
    xf = x.astype(jnp.float32)
    return xf * lax.rsqrt(jnp.mean(xf * xf, axis=-1, keepdims=True) + EPS) * g.astype(jnp.float32)


def rms_norm(x, g):
    return rms_f32(x, g).astype(x.dtype)


def t5_bucket(rel):
    nb = NUM_BUCKETS // 2
    max_exact = nb // 2
    ret = jnp.where(rel > 0, nb, 0)
    n = jnp.abs(rel)
    n_safe = jnp.maximum(n, 1).astype(jnp.float32)
    large = max_exact + (jnp.log(n_safe / max_exact) / math.log(MAX_DISTANCE / max_exact)
                         * (nb - max_exact)).astype(jnp.int32)
    large = jnp.minimum(large, nb - 1)
    return ret + jnp.where(n < max_exact, n, large)


def window_attention(q, k, v, q_g, k_g, sink, rel_bias):
    b, s, _ = q.shape
    nb = s // ATTN_BLOCK
    grp = ATTN_HEADS // ATTN_KV_HEADS
    blk = ATTN_BLOCK
    qn = rms_f32(q.reshape(b, s, ATTN_HEADS, HEAD_DIM), q_g) * (HEAD_DIM ** -0.5)
    qn = qn.reshape(b, nb, blk, ATTN_KV_HEADS, grp, HEAD_DIM)
    kn = rms_f32(k.reshape(b, s, ATTN_KV_HEADS, HEAD_DIM), k_g)
    vv = v.reshape(b, s, ATTN_KV_HEADS, HEAD_DIM).astype(jnp.float32)

    def windows(t):
        tp = jnp.pad(t, ((0, 0), (blk, blk), (0, 0), (0, 0)))
        tb = tp.reshape(b, nb + 2, blk, ATTN_KV_HEADS, HEAD_DIM)
        return jnp.concatenate([tb[:, :-2], tb[:, 1:-1], tb[:, 2:]], axis=2)

    kw = windows(kn)
    vw = windows(vv)
    scores = jnp.einsum('bnqhgd,bnkhd->bnhgqk', qn, kw)
    qi = jnp.arange(blk)[:, None]
    ki = jnp.arange(3 * blk)[None, :]
    rel = ki - blk - qi
    bias = rel_bias[t5_bucket(rel)].astype(jnp.float32)
    bias = bias.transpose(2, 0, 1).reshape(ATTN_KV_HEADS, grp, blk, 3 * blk)
    key_pos = jnp.arange(nb)[:, None] * blk - blk + ki
    mask = (jnp.abs(rel) <= WINDOW)[None] & ((key_pos >= 0) & (key_pos < s))[:, None, :]
    scores = jnp.where(mask[None, :, None, None], scores + bias, NEG_INF)
    snk = sink.astype(jnp.float32).reshape(ATTN_KV_HEADS, grp)[None, None, :, :, None, None]
    m = jnp.maximum(jnp.max(scores, axis=-1, keepdims=True), snk)
    pr = jnp.exp(scores - m)
    denom = jnp.sum(pr, axis=-1, keepdims=True) + jnp.exp(snk - m)
    out = jnp.einsum('bnhgqk,bnkhd->bnqhgd', pr / denom, vw)
    return out.reshape(b, s, ATTN_Q_WIDTH)


def spatial_gating(u, v, ln_g, ln_b, w_s, b_s):
    b, s, _ = u.shape
    u = jax.nn.gelu(u)
    vf = jax.nn.gelu(v).astype(jnp.float32)
    mu = jnp.mean(vf, axis=-1, keepdims=True)
    var = jnp.mean(jnp.square(vf - mu), axis=-1, keepdims=True)
    vn = ((vf - mu) * lax.rsqrt(var + EPS) * ln_g.astype(jnp.float32) + ln_b.astype(jnp.float32)).astype(u.dtype)
    vn = vn.reshape(b, s // GMLP_CHUNK, GMLP_CHUNK, GMLP_GROUPS, GMLP_WIDTH // GMLP_GROUPS)
    sv = jnp.einsum('gts,bcsge->bctge', w_s, vn) + b_s.T[:, :, None]
    return u * sv.reshape(b, s, GMLP_WIDTH)


def ssd_chunked(x, dt, a, bm, cm):
    b, s, h, p = x.shape
    g = bm.shape[2]
    j = h // g
    nc = s // SSD_CHUNK
    L = SSD_CHUNK
    xdt = (x.astype(jnp.float32) * dt[..., None]).reshape(b, nc, L, g, j, p)
    acum = jnp.cumsum((dt * a).reshape(b, nc, L, g, j), axis=2)
    bc = bm.reshape(b, nc, L, g, SSD_STATE)
    cc = cm.reshape(b, nc, L, g, SSD_STATE)
    seg = acum[:, :, :, None] - acum[:, :, None, :]
    tri = jnp.tril(jnp.ones((L, L), dtype=bool))
    decay = jnp.exp(jnp.where(tri[:, :, None, None], seg, -jnp.inf))
    cb = jnp.einsum('bclgn,bcsgn->bclsg', cc, bc)
    y_diag = jnp.einsum('bclsg,bclsgj,bcsgjp->bclgjp', cb, decay, xdt)
    decay_to_end = jnp.exp(acum[:, :, -1:] - acum)
    states = jnp.einsum('bclgn,bclgj,bclgjp->bcgjpn', bc, decay_to_end, xdt)
    chunk_decay = jnp.exp(acum[:, :, -1])

    def step(carry, inp):
        st, dec = inp
        return carry * dec[..., None, None] + st, carry

    init = jnp.zeros((b, g, j, p, SSD_STATE), dtype=states.dtype)
    _, prev = lax.scan(step, init, (jnp.swapaxes(states, 0, 1), jnp.swapaxes(chunk_decay, 0, 1)))
    prev = jnp.swapaxes(prev, 0, 1)
    y_off = jnp.einsum('bclgn,bcgjpn,bclgj->bclgjp', cc, prev, jnp.exp(acum))
    return (y_diag + y_off).reshape(b, s, h, p)


def ssd_branch(z, xbc, dt_raw, conv_w, conv_b, dt_bias, a_log, d_skip, norm_g):
    b, s, _ = xbc.shape
    pad = SSD_CONV // 2
    xp = jnp.pad(xbc, ((0, 0), (pad, pad), (0, 0)))
    conv = conv_b + xp[:, 0:s] * conv_w[0]
    for i in range(1, SSD_CONV):
        conv = conv + xp[:, i:i + s] * conv_w[i]
    xbc = jax.nn.silu(conv)
    xs, bm, cm = jnp.split(xbc, [SSD_INNER, SSD_INNER + SSD_GROUPS * SSD_STATE], axis=-1)
    xs = xs.reshape(b, s, SSD_HEADS, SSD_HEAD_DIM)
    bm = bm.reshape(b, s, SSD_GROUPS, SSD_STATE)
    cm = cm.reshape(b, s, SSD_GROUPS, SSD_STATE)
    dt = jax.nn.softplus(dt_raw.astype(jnp.float32).reshape(b, s, N_DIR, SSD_HEADS) + dt_bias.astype(jnp.float32))
    a = -jnp.exp(a_log.astype(jnp.float32))
    y_fwd = ssd_chunked(xs, dt[:, :, 0], a[0], bm, cm)
    flip = lambda t: jnp.flip(t, axis=1)
    y_bwd = flip(ssd_chunked(flip(xs), flip(dt[:, :, 1]), a[1], flip(bm), flip(cm)))
    y = y_fwd + y_bwd + d_skip.astype(jnp.float32)[:, None] * xs.astype(jnp.float32)
    y = y.reshape(b, s, SSD_INNER) * jax.nn.silu(z.astype(jnp.float32))
    yn = rms_f32(y.reshape(b, s, SSD_GROUPS, SSD_INNER // SSD_GROUPS), norm_g.reshape(SSD_GROUPS, -1))
    return yn.reshape(b, s, SSD_INNER).astype(z.dtype)


def encoder_layer(x, rel_bias, p):
    b, s, _ = x.shape
    xn = rms_norm(x, p['norm_mix_g'])
    proj = xn @ p['w_in']
    q, k, v, gu, gv, z, xbc, dt_raw, gates = jnp.split(proj, _split_points(IN_WIDTHS), axis=-1)
    y_a = window_attention(q, k, v, p['q_norm_g'], p['k_norm_g'], p['attn_sink'], rel_bias).astype(x.dtype)
    y_b = spatial_gating(gu, gv, p['gmlp_ln_g'], p['gmlp_ln_b'], p['w_spatial'], p['b_spatial'])
    y_c = ssd_branch(z, xbc, dt_raw, p['conv_w'], p['conv_b'], p['dt_bias'], p['a_log'], p['d_skip'], p['ssd_norm_g'])
    gt = jax.nn.sigmoid(gates.astype(jnp.float32)).astype(x.dtype).reshape(b, s, N_BRANCH, D_MODEL)
    merged = (gt[:, :, 0] * (y_a @ p['w_up_attn'])
              + gt[:, :, 1] * (y_b @ p['w_up_gmlp'])
              + gt[:, :, 2] * (y_c @ p['w_up_ssd']))
    h = x + merged @ p['w_out']
    hn = rms_norm(h, p['norm_ff_g'])
    ff = jnp.square(jax.nn.relu(hn @ p['w_ff1'])) @ p['w_ff2']
    return h + ff


def setup_inputs(seed: int = 0) -> dict:
    key = jax.random.key(seed)
    ks = jax.random.split(key, 32)
    f32 = jnp.float32

    def nrm(k, shape, scale):
        return jax.random.normal(k, shape, f32) * scale

    def gain(k, shape):
        return 1.0 + 0.05 * jax.random.normal(k, shape, f32)

    dt0 = jnp.exp(jax.random.uniform(ks[15], (DEPTH, N_DIR, SSD_HEADS), f32, math.log(1e-3), math.log(1e-1)))
    return {
        'x_prompt': nrm(ks[0], (BATCH, SEQ, D_MODEL), 1.0),
        'x_sample': nrm(ks[1], (DEC_BATCH, DEC_SEQ, D_MODEL), 1.0),
        'rel_bias': nrm(ks[2], (NUM_BUCKETS, ATTN_HEADS), 0.5),
        'norm_mix_g': gain(ks[3], (DEPTH, D_MODEL)),
        'w_in': nrm(ks[4], (DEPTH, D_MODEL, IN_PROJ), D_MODEL ** -0.5),
        'q_norm_g': gain(ks[5], (DEPTH, HEAD_DIM)),
        'k_norm_g': gain(ks[6], (DEPTH, HEAD_DIM)),
        'attn_sink': nrm(ks[7], (DEPTH, ATTN_HEADS), 1.0),
        'gmlp_ln_g': gain(ks[8], (DEPTH, GMLP_WIDTH)),
        'gmlp_ln_b': nrm(ks[9], (DEPTH, GMLP_WIDTH), 0.02),
        'w_spatial': nrm(ks[10], (DEPTH, GMLP_GROUPS, GMLP_CHUNK, GMLP_CHUNK), GMLP_CHUNK ** -0.5),
        'b_spatial': gain(ks[11], (DEPTH, GMLP_GROUPS, GMLP_CHUNK)),
        'conv_w': nrm(ks[12], (DEPTH, SSD_CONV, SSD_XBC), SSD_CONV ** -0.5),
        'conv_b': nrm(ks[13], (DEPTH, SSD_XBC), 0.02),
        'dt_bias': dt0 + jnp.log(-jnp.expm1(-dt0)),
        'a_log': jnp.log(jax.random.uniform(ks[14], (DEPTH, N_DIR, SSD_HEADS), f32, 1.0, 16.0)),
        'd_skip': gain(ks[16], (DEPTH, SSD_HEADS)),
        'ssd_norm_g': gain(ks[17], (DEPTH, SSD_INNER)),
        'w_up_attn': nrm(ks[18], (DEPTH, ATTN_Q_WIDTH, D_MODEL), ATTN_Q_WIDTH ** -0.5),
        'w_up_gmlp': nrm(ks[19], (DEPTH, GMLP_WIDTH, D_MODEL), GMLP_WIDTH ** -0.5),
        'w_up_ssd': nrm(ks[20], (DEPTH, SSD_INNER, D_MODEL), SSD_INNER ** -0.5),
        'w_out': nrm(ks[21], (DEPTH, D_MODEL, D_MODEL), D_MODEL ** -0.5),
        'norm_ff_g': gain(ks[22], (DEPTH, D_MODEL)),
        'w_ff1': nrm(ks[23], (DEPTH, D_MODEL, FF_DIM), D_MODEL ** -0.5),
        'w_ff2': nrm(ks[24], (DEPTH, FF_DIM, D_MODEL), FF_DIM ** -0.5),
    }


def reference(x_prompt, x_sample, rel_bias, norm_mix_g, w_in, q_norm_g, k_norm_g, attn_sink,
              gmlp_ln_g, gmlp_ln_b, w_spatial, b_spatial, conv_w, conv_b, dt_bias, a_log, d_skip,
              ssd_norm_g, w_up_attn, w_up_gmlp, w_up_ssd, w_out, norm_ff_g, w_ff1, w_ff2):
    y_prompt = x_prompt
    y_sample = x_sample
    for l in range(DEPTH):
        p = {
            'norm_mix_g': norm_mix_g[l], 'w_in': w_in[l],
            'q_norm_g': q_norm_g[l], 'k_norm_g': k_norm_g[l], 'attn_sink': attn_sink[l],
            'gmlp_ln_g': gmlp_ln_g[l], 'gmlp_ln_b': gmlp_ln_b[l],
            'w_spatial': w_spatial[l], 'b_spatial': b_spatial[l],
            'conv_w': conv_w[l], 'conv_b': conv_b[l], 'dt_bias': dt_bias[l], 'a_log': a_log[l],
            'd_skip': d_skip[l], 'ssd_norm_g': ssd_norm_g[l],
            'w_up_attn': w_up_attn[l], 'w_up_gmlp': w_up_gmlp[l], 'w_up_ssd': w_up_ssd[l],
            'w_out': w_out[l], 'norm_ff_g': norm_ff_g[l], 'w_ff1': w_ff1[l], 'w_ff2': w_ff2[l],
        }
        y_prompt = encoder_layer(y_prompt, rel_bias, p)
        y_sample = encoder_layer(y_sample, rel_bias, p)
    return (y_prompt, y_sample)
```

```python
import functools
import math
from typing import NamedTuple

import numpy as np
import jax
import jax.numpy as jnp
from jax import lax
from jax.experimental import pallas as pl
from jax.experimental.pallas import tpu as pltpu

F32 = jnp.float32
BF16 = jnp.bfloat16

D_MODEL = 1024
ATTN_HEADS = 8
ATTN_KV_HEADS = 2
HEAD_DIM = 64
ATTN_Q_WIDTH = ATTN_HEADS * HEAD_DIM
ATTN_KV_WIDTH = ATTN_KV_HEADS * HEAD_DIM
WINDOW = 128
NUM_BUCKETS = 32
MAX_DISTANCE = 128
GMLP_WIDTH = 512
GMLP_GROUPS = 4
SSD_INNER = 1024
SSD_HEAD_DIM = 64
SSD_HEADS = SSD_INNER // SSD_HEAD_DIM
SSD_GROUPS = 2
SSD_STATE = 128
N_DIR = 2
FF_DIM = 4 * D_MODEL
EPS = 1e-6
NEG_INF = -1e30

BLK = 128
LANES = 128
BF16_SUBLANES = 16
VMEM_LIMIT = 56 * 1024 * 1024

COL_Z = 0
COL_XS = 1024
COL_GATE = 2048
COL_Q = 5120
COL_GU = 5632
COL_GV = 6144
COL_BC = 6656
COL_KV = 7168
P_WIDTH = 7424
DT_REP = 32

_R_Q, _R_K, _R_V, _R_GU, _R_GV, _R_Z, _R_XS, _R_B, _R_C, _R_DT, _R_GATE, _R_END = (
    0, 512, 640, 768, 1280, 1792, 2816, 3840, 4096, 4352, 4384, 7456)


class Layout(NamedTuple):
    n_blocks: int
    n_first: int
    per_seq_first: int
    per_seq_second: int


def _seq_pos(blk, lay):
    in_first = blk < lay.n_first
    pos = jnp.where(in_first, lax.rem(blk, lay.per_seq_first),
                    lax.rem(blk - lay.n_first, lay.per_seq_second))
    last = jnp.where(in_first, lay.per_seq_first - 1, lay.per_seq_second - 1)
    return pos == 0, pos == last


def _sigmoid(x):
    return 1.0 / (1.0 + jnp.exp(-x))


def _silu(x):
    return x * _sigmoid(x)


def _gelu(x):
    return x * (0.5 * (1.0 + jnp.tanh(math.sqrt(2.0 / math.pi) * (x + 0.044715 * (x * x * x)))))


def _softplus(x):
    return jnp.maximum(x, 0.0) + jnp.log1p(jnp.exp(-jnp.abs(x)))


def _rms(x, g):
    return x * lax.rsqrt(jnp.mean(x * x, axis=-1, keepdims=True) + EPS) * g


def _dot(a, b):
    return jnp.dot(a, b, preferred_element_type=F32)


def _dot_nt(a, b):
    return lax.dot_general(a, b, (((1,), (1,)), ((), ())), preferred_element_type=F32)


def _const_spec(shape):
    nd = len(shape)
    return pl.BlockSpec(shape, lambda *_: (0,) * nd)


def _params(semantics):
    return pltpu.CompilerParams(dimension_semantics=semantics, vmem_limit_bytes=VMEM_LIMIT)


_PROJ_CHUNK = 512


def _inproj_body(x_ref, g_ref, w_ref, wdt_ref, wdtT_ref, o_ref, dt_ref, dtT_ref):
    xn = _rms(x_ref[...], g_ref[...]).astype(BF16)
    for c0 in range(0, P_WIDTH, _PROJ_CHUNK):
        c1 = min(c0 + _PROJ_CHUNK, P_WIDTH)
        o_ref[:, c0:c1] = _dot(xn, w_ref[:, c0:c1]).astype(BF16)
    dt_ref[...] = _dot(xn, wdt_ref[...])
    dtT_ref[...] = _dot_nt(wdtT_ref[...], xn)


def _inproj(x, g, w, wdt, wdtT, tm=512):
    t = x.shape[0]
    return pl.pallas_call(
        _inproj_body,
        grid=(t // tm,),
        in_specs=[
            pl.BlockSpec((tm, D_MODEL), lambda i: (i, 0)),
            _const_spec((1, D_MODEL)),
            _const_spec((D_MODEL, P_WIDTH)),
            _const_spec((D_MODEL, LANES)),
            _const_spec((DT_REP, D_MODEL)),
        ],
        out_specs=[
            pl.BlockSpec((tm, P_WIDTH), lambda i: (i, 0)),
            pl.BlockSpec((tm, LANES), lambda i: (i, 0)),
            pl.BlockSpec((DT_REP, tm), lambda i: (0, i)),
        ],
        out_shape=[
            jax.ShapeDtypeStruct((t, P_WIDTH), BF16),
            jax.ShapeDtypeStruct((t, LANES), F32),
            jax.ShapeDtypeStruct((DT_REP, t), F32),
        ],
        compiler_params=_params(("parallel",)),
        name="inproj",
    )(x, g, w, wdt, wdtT)


_ATTN_TILE_BLOCKS = 4


def _attn_body(q_ref, kvp_ref, kvm_ref, kvn_ref, bias_ref, qg_ref, kg_ref, sink_ref, o_ref, *, lay):
    i = pl.program_id(0)
    nt = _ATTN_TILE_BLOCKS
    kv = jnp.concatenate([kvp_ref[...], kvm_ref[...], kvn_ref[...]], axis=0)
    grp = ATTN_HEADS // ATTN_KV_HEADS
    kn, vv = [], []
    for g in range(ATTN_KV_HEADS):
        kh = kv[:, g * HEAD_DIM:(g + 1) * HEAD_DIM].astype(F32)
        kn.append(_rms(kh, kg_ref[...]).astype(BF16))
        vv.append(kv[:, ATTN_KV_WIDTH + g * HEAD_DIM:ATTN_KV_WIDTH + (g + 1) * HEAD_DIM])
    col = lax.broadcasted_iota(jnp.int32, (BLK, 3 * BLK), 1)
    qscale = qg_ref[...] * (HEAD_DIM ** -0.5)
    for jb in range(nt):
        first, last = _seq_pos(i * nt + jb, lay)
        lo = jnp.where(first, BLK, 0)
        hi = jnp.where(last, 2 * BLK, 3 * BLK)
        valid = (col >= lo) & (col < hi)
        q = q_ref[jb * BLK:(jb + 1) * BLK, :].astype(F32)
        outs = []
        for h in range(ATTN_HEADS):
            g = h // grp
            qh = q[:, h * HEAD_DIM:(h + 1) * HEAD_DIM]
            qn = qh * lax.rsqrt(jnp.mean(qh * qh, axis=-1, keepdims=True) + EPS) * qscale
            s = _dot_nt(qn.astype(BF16), kn[g][jb * BLK:(jb + 3) * BLK])
            s = jnp.where(valid, s + bias_ref[h], NEG_INF)
            snk = sink_ref[0:1, h:h + 1]
            m = jnp.maximum(jnp.max(s, axis=-1, keepdims=True), snk)
            p = jnp.exp(s - m)
            denom = jnp.sum(p, axis=-1, keepdims=True) + jnp.exp(snk - m)
            o = _dot(p.astype(BF16), vv[g][jb * BLK:(jb + 3) * BLK])
            outs.append(o / denom)
        o_ref[jb * BLK:(jb + 1) * BLK, :] = jnp.concatenate(outs, axis=1).astype(BF16)


def _attention(proj, bias, qg, kg, sink, lay):
    t = proj.shape[0]
    nt = _ATTN_TILE_BLOCKS
    rows = nt * BLK
    q_blk = COL_Q // ATTN_Q_WIDTH
    kv_w = 2 * ATTN_KV_WIDTH
    kv_blk = COL_KV // kv_w
    last_blk = lay.n_blocks - 1
    return pl.pallas_call(
        functools.partial(_attn_body, lay=lay),
        grid=(t // rows,),
        in_specs=[
            pl.BlockSpec((rows, ATTN_Q_WIDTH), lambda i: (i, q_blk)),
            pl.BlockSpec((BLK, kv_w), lambda i: (jnp.maximum(i * nt - 1, 0), kv_blk)),
            pl.BlockSpec((rows, kv_w), lambda i: (i, kv_blk)),
            pl.BlockSpec((BLK, kv_w), lambda i: (jnp.minimum(i * nt + nt, last_blk), kv_blk)),
            _const_spec((ATTN_HEADS, BLK, 3 * BLK)),
            _const_spec((1, HEAD_DIM)),
            _const_spec((1, HEAD_DIM)),
            _const_spec((1, ATTN_HEADS)),
        ],
        out_specs=pl.BlockSpec((rows, ATTN_Q_WIDTH), lambda i: (i, 0)),
        out_shape=jax.ShapeDtypeStruct((t, ATTN_Q_WIDTH), BF16),
        compiler_params=_params(("parallel",)),
        name="attention",
    )(proj, proj, proj, proj, bias, qg, kg, sink)


_GMLP_TILE_BLOCKS = 4


def _gmlp_body(gu_ref, gv_ref, lng_ref, lnb_ref, ws_ref, bs_ref, o_ref):
    u = _gelu(gu_ref[...].astype(F32))
    vf = _gelu(gv_ref[...].astype(F32))
    mu = jnp.mean(vf, axis=-1, keepdims=True)
    vc = vf - mu
    var = jnp.mean(vc * vc, axis=-1, keepdims=True)
    vn = (vc * lax.rsqrt(var + EPS) * lng_ref[...] + lnb_ref[...]).astype(BF16)
    gw = GMLP_WIDTH // GMLP_GROUPS
    for c in range(_GMLP_TILE_BLOCKS):
        r0, r1 = c * BLK, (c + 1) * BLK
        for g in range(GMLP_GROUPS):
            l0, l1 = g * gw, (g + 1) * gw
            sv = _dot(ws_ref[g], vn[r0:r1, l0:l1]) + bs_ref[:, l0:l1]
            o_ref[r0:r1, l0:l1] = (u[r0:r1, l0:l1] * sv).astype(BF16)


def _gmlp(proj, lng, lnb, ws, bs):
    t = proj.shape[0]
    rows = _GMLP_TILE_BLOCKS * BLK
    return pl.pallas_call(
        _gmlp_body,
        grid=(t // rows,),
        in_specs=[
            pl.BlockSpec((rows, GMLP_WIDTH), lambda i: (i, COL_GU // GMLP_WIDTH)),
            pl.BlockSpec((rows, GMLP_WIDTH), lambda i: (i, COL_GV // GMLP_WIDTH)),
            _const_spec((1, GMLP_WIDTH)),
            _const_spec((1, GMLP_WIDTH)),
            _const_spec((GMLP_GROUPS, BLK, BLK)),
            _const_spec((BLK, GMLP_WIDTH)),
        ],
        out_specs=pl.BlockSpec((rows, GMLP_WIDTH), lambda i: (i, 0)),
        out_shape=jax.ShapeDtypeStruct((t, GMLP_WIDTH), BF16),
        compiler_params=_params(("parallel",)),
        name="gmlp",
    )(proj, proj, lng, lnb, ws, bs)


_SPLIT_TERMS = 4


def _lane_split(x):
    rep = lax.broadcasted_iota(jnp.int32, x.shape, 1) // DT_REP
    out = jnp.zeros(x.shape, BF16)
    rem = x
    for r in range(_SPLIT_TERMS):
        term = rem.astype(BF16)
        out = jnp.where(rep == r, term, out)
        rem = rem - term.astype(F32)
    return out


def _sublane_split(x):
    terms, rem = [], x
    for _ in range(_SPLIT_TERMS):
        term = rem.astype(BF16)
        terms.append(term)
        rem = rem - term.astype(F32)
    return jnp.concatenate(terms, axis=0)


def _conv_silu(main_ref, prev_ref, next_ref, w_ref, b_ref, first, last):
    x = main_ref[...].astype(F32)
    n = x.shape[0]
    halo_p = prev_ref[BF16_SUBLANES - 1:BF16_SUBLANES, :].astype(F32) * jnp.where(first, 0.0, 1.0)
    halo_n = next_ref[0:1, :].astype(F32) * jnp.where(last, 0.0, 1.0)
    row = lax.broadcasted_iota(jnp.int32, x.shape, 0)
    x_m1 = jnp.where(row == 0, halo_p, pltpu.roll(x, 1, axis=0))
    x_p1 = jnp.where(row == n - 1, halo_n, pltpu.roll(x, n - 1, axis=0))
    conv = b_ref[...] + x_m1 * w_ref[0:1, :] + x * w_ref[1:2, :] + x_p1 * w_ref[2:3, :]
    return _silu(conv)


def _ssd_body(xs_ref, xsp_ref, xsn_ref, bc_ref, bcp_ref, bcn_ref, dt_ref, dtT_ref,
              wxs_ref, bxs_ref, wbc_ref, bbc_ref, dtb_ref, alog_ref, dtbT_ref, alogT_ref, e_ref,
              *rest, lay, rev, final):
    if final:
        yprev_ref, z_ref, dskip_ref, ng_ref, o_ref, st_ref, y_ref = rest
    else:
        o_ref, st_ref = rest
        y_ref = o_ref
    c = pl.program_id(0)
    cc = lay.n_blocks - 1 - c if rev else c
    first, last = _seq_pos(cc, lay)
    entering = last if rev else first

    @pl.when(entering)
    def _():
        st_ref[...] = jnp.zeros(st_ref.shape, F32)

    L = BLK
    d = 1 if rev else 0
    xs = _conv_silu(xs_ref, xsp_ref, xsn_ref, wxs_ref, bxs_ref, first, last)
    bc = _conv_silu(bc_ref, bcp_ref, bcn_ref, wbc_ref, bbc_ref, first, last)
    xs_bf = xs.astype(BF16)

    li = lax.broadcasted_iota(jnp.int32, (L, L), 0)
    si = lax.broadcasted_iota(jnp.int32, (L, L), 1)
    tri = (si >= li) if rev else (si <= li)
    tm = jnp.where(tri, 1.0, 0.0).astype(BF16)
    tmT = jnp.where((li >= si) if rev else (li <= si), 1.0, 0.0).astype(BF16)
    end = 0 if rev else L - 1

    dt = _softplus(dt_ref[...] + dtb_ref[...])
    dta = dt * (-jnp.exp(alog_ref[...]))
    part = _dot(tm, _lane_split(dta))
    acum = part
    for r in range(1, _SPLIT_TERMS):
        acum = acum + pltpu.roll(part, r * DT_REP, axis=1)
    a_end = acum[end:end + 1, :]
    wfac = dt * jnp.exp(a_end - acum)
    w_exp = _dot(_lane_split(wfac), e_ref[...])
    cd = jnp.broadcast_to(jnp.exp(a_end), (8, LANES))
    cd_exp = _dot(_lane_split(cd), e_ref[...])[0:1, :]

    h0 = d * SSD_HEADS
    dtT = _softplus(dtT_ref[h0:h0 + SSD_HEADS, :] + dtbT_ref[...])
    dtaT = dtT * (-jnp.exp(alogT_ref[...]))
    partT = _dot(_sublane_split(dtaT), tmT)
    acumT = partT[0:SSD_HEADS]
    for r in range(1, _SPLIT_TERMS):
        acumT = acumT + partT[r * SSD_HEADS:(r + 1) * SSD_HEADS]

    lane = lax.broadcasted_iota(jnp.int32, (L, LANES), 1)
    left = lane < SSD_HEAD_DIM
    hpg = SSD_HEADS // SSD_GROUPS
    cb, cm_f = [], []
    for g in range(SSD_GROUPS):
        bm_g = bc[:, g * SSD_STATE:(g + 1) * SSD_STATE].astype(BF16)
        cm_g = bc[:, (SSD_GROUPS + g) * SSD_STATE:(SSD_GROUPS + g + 1) * SSD_STATE]
        cb.append(_dot_nt(cm_g.astype(BF16), bm_g))
        cm_f.append(cm_g)

    zero_bf = jnp.zeros((L, LANES), BF16)
    for k in range(SSD_HEADS // 2):
        g = (2 * k) // hpg
        lhs = []
        for h in (2 * k, 2 * k + 1):
            colb = jnp.broadcast_to(acum[:, h0 + h:h0 + h + 1], (L, L))
            seg = colb - acumT[h:h + 1, :]
            dec = jnp.exp(jnp.where(tri, seg, NEG_INF))
            lhs.append((cb[g] * dec * dtT[h:h + 1, :]).astype(BF16))
            lhs.append((cm_f[g] * jnp.exp(colb)).astype(BF16))
        xs_pair = xs_bf[:, k * LANES:(k + 1) * LANES]
        st_pair = st_ref[g, :, (k % (hpg // 2)) * LANES:(k % (hpg // 2) + 1) * LANES].astype(BF16)
        rhs = jnp.concatenate([
            jnp.where(left, xs_pair, zero_bf), jnp.where(left, st_pair, zero_bf),
            jnp.where(left, zero_bf, xs_pair), jnp.where(left, zero_bf, st_pair)], axis=0)
        y_ref[:, k * LANES:(k + 1) * LANES] = _dot(jnp.concatenate(lhs, axis=1), rhs).astype(y_ref.dtype)

    wx = (xs * w_exp).astype(BF16)
    gw = SSD_INNER // SSD_GROUPS
    for g in range(SSD_GROUPS):
        bmT = jnp.transpose(bc[:, g * SSD_STATE:(g + 1) * SSD_STATE]).astype(BF16)
        st_ref[g] = st_ref[g] * cd_exp[:, g * gw:(g + 1) * gw] + _dot(bmT, wx[:, g * gw:(g + 1) * gw])

    if final:
        y = y_ref[...] + yprev_ref[...] + dskip_ref[...] * xs
        y = y * _silu(z_ref[...].astype(F32))
        for g in range(SSD_GROUPS):
            yg = y[:, g * gw:(g + 1) * gw]
            o_ref[:, g * gw:(g + 1) * gw] = _rms(yg, ng_ref[:, g * gw:(g + 1) * gw]).astype(BF16)


def _ssd_sweep(proj, dt, dtT, consts, lay, rev, final_inputs=None):
    t = proj.shape[0]
    nb = lay.n_blocks
    final = final_inputs is not None
    hal = BLK // BF16_SUBLANES
    n_hal = nb * hal

    def chunk(c):
        return nb - 1 - c if rev else c

    def main(col_blk):
        return lambda c: (chunk(c), col_blk)

    def prev(col_blk):
        return lambda c: (jnp.maximum(chunk(c) * hal - 1, 0), col_blk)

    def nxt(col_blk):
        return lambda c: (jnp.minimum((chunk(c) + 1) * hal, n_hal - 1), col_blk)

    bcw = 2 * SSD_GROUPS * SSD_STATE
    xs_blk, bc_blk = COL_XS // SSD_INNER, COL_BC // bcw
    wxs, bxs, wbc, bbc, dtb, alog, dtbT, alogT, e_mat = consts
    in_specs = [
        pl.BlockSpec((BLK, SSD_INNER), main(xs_blk)),
        pl.BlockSpec((BF16_SUBLANES, SSD_INNER), prev(xs_blk)),
        pl.BlockSpec((BF16_SUBLANES, SSD_INNER), nxt(xs_blk)),
        pl.BlockSpec((BLK, bcw), main(bc_blk)),
        pl.BlockSpec((BF16_SUBLANES, bcw), prev(bc_blk)),
        pl.BlockSpec((BF16_SUBLANES, bcw), nxt(bc_blk)),
        pl.BlockSpec((BLK, LANES), main(0)),
        pl.BlockSpec((DT_REP, BLK), lambda c: (0, chunk(c))),
        _const_spec((3, SSD_INNER)), _const_spec((1, SSD_INNER)),
        _const_spec((3, bcw)), _const_spec((1, bcw)),
        _const_spec((1, LANES)), _const_spec((1, LANES)),
        _const_spec((SSD_HEADS, 1)), _const_spec((SSD_HEADS, 1)),
        _const_spec((LANES, SSD_INNER)),
    ]
    args = [proj, proj, proj, proj, proj, proj, dt, dtT, wxs, bxs, wbc, bbc, dtb, alog, dtbT, alogT, e_mat]
    scratch = [pltpu.VMEM((SSD_GROUPS, SSD_STATE, SSD_INNER // SSD_GROUPS), F32)]
    if final:
        yprev, dskip, ng = final_inputs
        in_specs += [
            pl.BlockSpec((BLK, SSD_INNER), main(0)),
            pl.BlockSpec((BLK, SSD_INNER), main(COL_Z // SSD_INNER)),
            _const_spec((1, SSD_INNER)), _const_spec((1, SSD_INNER)),
        ]
        args += [yprev, proj, dskip, ng]
        scratch.append(pltpu.VMEM((BLK, SSD_INNER), F32))
        out_dtype = BF16
    else:
        out_dtype = F32
    return pl.pallas_call(
        functools.partial(_ssd_body, lay=lay, rev=rev, final=final),
        grid=(nb,),
        in_specs=in_specs,
        out_specs=pl.BlockSpec((BLK, SSD_INNER), main(0)),
        out_shape=jax.ShapeDtypeStruct((t, SSD_INNER), out_dtype),
        scratch_shapes=scratch,
        compiler_params=_params(("arbitrary",)),
        name="ssd_bwd" if rev else "ssd_fwd",
    )(*args)


def _merge_body(x_ref, ya_ref, yb_ref, yc_ref, g0_ref, g1_ref, g2_ref, wa_ref, wb_ref, wc_ref, wo_ref, h_ref):
    m = _sigmoid(g0_ref[...].astype(F32)) * _dot(ya_ref[...], wa_ref[...])
    m = m + _sigmoid(g1_ref[...].astype(F32)) * _dot(yb_ref[...], wb_ref[...])
    m = m + _sigmoid(g2_ref[...].astype(F32)) * _dot(yc_ref[...], wc_ref[...])
    h_ref[...] = x_ref[...] + _dot(m.astype(BF16), wo_ref[...])


def _merge(x, ya, yb, yc, proj, wa, wb, wc, wo, tm=512):
    t = x.shape[0]
    g_blk = COL_GATE // D_MODEL

    def row(width):
        return pl.BlockSpec((tm, width), lambda i: (i, 0))

    return pl.pallas_call(
        _merge_body,
        grid=(t // tm,),
        in_specs=[
            row(D_MODEL), row(ATTN_Q_WIDTH), row(GMLP_WIDTH), row(SSD_INNER),
            pl.BlockSpec((tm, D_MODEL), lambda i: (i, g_blk)),
            pl.BlockSpec((tm, D_MODEL), lambda i: (i, g_blk + 1)),
            pl.BlockSpec((tm, D_MODEL), lambda i: (i, g_blk + 2)),
            _const_spec((ATTN_Q_WIDTH, D_MODEL)), _const_spec((GMLP_WIDTH, D_MODEL)),
            _const_spec((SSD_INNER, D_MODEL)), _const_spec((D_MODEL, D_MODEL)),
        ],
        out_specs=row(D_MODEL),
        out_shape=jax.ShapeDtypeStruct((t, D_MODEL), F32),
        compiler_params=_params(("parallel",)),
        name="merge",
    )(x, ya, yb, yc, proj, proj, proj, wa, wb, wc, wo)


_FF_CHUNK = 1024


def _ffn_body(h_ref, g_ref, w1_ref, w2_ref, o_ref):
    h = h_ref[...]
    hn = _rms(h, g_ref[...]).astype(BF16)
    acc = h
    for c0 in range(0, FF_DIM, _FF_CHUNK):
        a = jnp.maximum(_dot(hn, w1_ref[:, c0:c0 + _FF_CHUNK]), 0.0)
        acc = acc + _dot((a * a).astype(BF16), w2_ref[c0:c0 + _FF_CHUNK, :])
    o_ref[...] = acc


def _ffn(h, g, w1, w2, tm=512):
    t = h.shape[0]
    return pl.pallas_call(
        _ffn_body,
        grid=(t // tm,),
        in_specs=[
            pl.BlockSpec((tm, D_MODEL), lambda i: (i, 0)),
            _const_spec((1, D_MODEL)),
            _const_spec((D_MODEL, FF_DIM)),
            _const_spec((FF_DIM, D_MODEL)),
        ],
        out_specs=pl.BlockSpec((tm, D_MODEL), lambda i: (i, 0)),
        out_shape=jax.ShapeDtypeStruct((t, D_MODEL), F32),
        compiler_params=_params(("parallel",)),
        name="ffn",
    )(h, g, w1, w2)


def _t5_bucket(rel):
    nb = NUM_BUCKETS // 2
    max_exact = nb // 2
    ret = jnp.where(rel > 0, nb, 0)
    n = jnp.abs(rel)
    n_safe = jnp.maximum(n, 1).astype(F32)
    large = max_exact + (jnp.log(n_safe / max_exact) / math.log(MAX_DISTANCE / max_exact)
                         * (nb - max_exact)).astype(jnp.int32)
    large = jnp.minimum(large, nb - 1)
    return ret + jnp.where(n < max_exact, n, large)


def _bias_table(rel_bias):
    qi = jnp.arange(BLK)[:, None]
    ki = jnp.arange(3 * BLK)[None, :]
    rel = ki - BLK - qi
    bias = rel_bias[_t5_bucket(rel)].astype(F32).transpose(2, 0, 1)
    return jnp.where((jnp.abs(rel) <= WINDOW)[None], bias, NEG_INF)


def _expand_matrix(d):
    k = np.arange(LANES)[:, None]
    col = np.arange(SSD_INNER)[None, :]
    hit = ((k % DT_REP) == d * SSD_HEADS + col // SSD_HEAD_DIM)
    return jnp.asarray(hit, dtype=BF16)


def _prep_layer_params(w_in, norm_mix_g, q_norm_g, k_norm_g, attn_sink, gmlp_ln_g, gmlp_ln_b, w_spatial,
                       b_spatial, conv_w, conv_b, dt_bias, a_log, d_skip, ssd_norm_g, w_up_attn, w_up_gmlp,
                       w_up_ssd, w_out, norm_ff_g, w_ff1, w_ff2):
    depth = w_in.shape[0]

    def cols(a, b):
        return w_in[:, :, a:b]

    w_main = jnp.concatenate([
        cols(_R_Z, _R_XS), cols(_R_XS, _R_B), cols(_R_GATE, _R_END), cols(_R_Q, _R_K),
        cols(_R_GU, _R_GV), cols(_R_GV, _R_Z), cols(_R_B, _R_DT), cols(_R_K, _R_GU)], axis=-1).astype(BF16)
    w_dt = cols(_R_DT, _R_GATE)
    rep = LANES // DT_REP
    nxs = SSD_INNER
    dtb = dt_bias.reshape(depth, 1, DT_REP)
    alog = a_log.reshape(depth, 1, DT_REP)
    gw = GMLP_WIDTH // GMLP_GROUPS
    return dict(
        norm_mix_g=norm_mix_g[:, None, :],
        w_main=w_main,
        w_dt=jnp.tile(w_dt, (1, 1, rep)).astype(BF16),
        w_dtT=jnp.swapaxes(w_dt, 1, 2).astype(BF16),
        q_g=q_norm_g[:, None, :], k_g=k_norm_g[:, None, :], sink=attn_sink[:, None, :],
        ln_g=gmlp_ln_g[:, None, :], ln_b=gmlp_ln_b[:, None, :],
        w_s=w_spatial.astype(BF16),
        b_s=jnp.repeat(jnp.swapaxes(b_spatial, 1, 2), gw, axis=-1),
        conv_w_xs=conv_w[:, :, :nxs], conv_b_xs=conv_b[:, None, :nxs],
        conv_w_bc=conv_w[:, :, nxs:], conv_b_bc=conv_b[:, None, nxs:],
        dtb=jnp.tile(dtb, (1, 1, rep)), alog=jnp.tile(alog, (1, 1, rep)),
        dtbT=dt_bias[..., None], alogT=a_log[..., None],
        d_skip=jnp.repeat(d_skip, SSD_HEAD_DIM, axis=-1)[:, None, :],
        ssd_norm_g=ssd_norm_g[:, None, :],
        w_a=w_up_attn.astype(BF16), w_b=w_up_gmlp.astype(BF16), w_c=w_up_ssd.astype(BF16),
        w_o=w_out.astype(BF16),
        norm_ff_g=norm_ff_g[:, None, :],
        w1=w_ff1.astype(BF16), w2=w_ff2.astype(BF16),
    )


def _layer(x, p, bias, e_mats, lay):
    proj, dt, dtT = _inproj(x, p["norm_mix_g"], p["w_main"], p["w_dt"], p["w_dtT"])
    ya = _attention(proj, bias, p["q_g"], p["k_g"], p["sink"], lay)
    yb = _gmlp(proj, p["ln_g"], p["ln_b"], p["w_s"], p["b_s"])

    def ssd_consts(d):
        return (p["conv_w_xs"], p["conv_b_xs"], p["conv_w_bc"], p["conv_b_bc"], p["dtb"], p["alog"],
                p["dtbT"][d], p["alogT"][d], e_mats[d])

    y_bwd = _ssd_sweep(proj, dt, dtT, ssd_consts(1), lay, rev=True)
    yc = _ssd_sweep(proj, dt, dtT, ssd_consts(0), lay, rev=False,
                    final_inputs=(y_bwd, p["d_skip"], p["ssd_norm_g"]))
    h = _merge(x, ya, yb, yc, proj, p["w_a"], p["w_b"], p["w_c"], p["w_o"])
    return _ffn(h, p["norm_ff_g"], p["w1"], p["w2"])


def _encoder(x_first, x_second, rel_bias, layer_params):
    b1, s1, _ = x_first.shape
    b2, s2, _ = x_second.shape
    t1, t2 = b1 * s1, b2 * s2
    lay = Layout(n_blocks=(t1 + t2) // BLK, n_first=t1 // BLK, per_seq_first=s1 // BLK,
                 per_seq_second=s2 // BLK)
    x = jnp.concatenate([x_first.reshape(t1, D_MODEL), x_second.reshape(t2, D_MODEL)], axis=0)
    bias = _bias_table(rel_bias)
    e_mats = (_expand_matrix(0), _expand_matrix(1))
    params = _prep_layer_params(**layer_params)

    def step(carry, p):
        return _layer(carry, p, bias, e_mats, lay), None

    x, _ = lax.scan(step, x, params)
    return x[:t1].reshape(b1, s1, D_MODEL), x[t1:].reshape(b2, s2, D_MODEL)


def kernel(x_prompt, x_sample, rel_bias, norm_mix_g, w_in, q_norm_g, k_norm_g, attn_sink, gmlp_ln_g, gmlp_ln_b,
           w_spatial, b_spatial, conv_w, conv_b, dt_bias, a_log, d_skip, ssd_norm_g, w_up_attn, w_up_gmlp,
           w_up_ssd, w_out, norm_ff_g, w_ff1, w_ff2):
    layer_params = dict(
        w_in=w_in, norm_mix_g=norm_mix_g, q_norm_g=q_norm_g, k_norm_g=k_norm_g, attn_sink=attn_sink,
        gmlp_ln_g=gmlp_ln_g, gmlp_ln_b=gmlp_ln_b, w_spatial=w_spatial, b_spatial=b_spatial, conv_w=conv_w,
        conv_b=conv_b, dt_bias=dt_bias, a_log=a_log, d_skip=d_skip, ssd_norm_g=ssd_norm_g,
        w_up_attn=w_up_attn, w_up_gmlp=w_up_gmlp, w_up_ssd=w_up_ssd, w_out=w_out, norm_ff_g=norm_ff_g,
        w_ff1=w_ff1, w_ff2=w_ff2)
    return _encoder(x_prompt, x_sample, rel_bias, layer_params)
```

```python
import functools
import math
from typing import NamedTuple

import numpy as np
import jax
import jax.numpy as jnp
from jax import lax
from jax.experimental import pallas as pl
from jax.experimental.pallas import tpu as pltpu

F32 = jnp.float32
BF16 = jnp.bfloat16

D_MODEL = 1024
ATTN_HEADS = 8
ATTN_KV_HEADS = 2
HEAD_DIM = 64
ATTN_Q_WIDTH = ATTN_HEADS * HEAD_DIM
ATTN_KV_WIDTH = ATTN_KV_HEADS * HEAD_DIM
WINDOW = 128
NUM_BUCKETS = 32
MAX_DISTANCE = 128
GMLP_WIDTH = 512
GMLP_GROUPS = 4
SSD_INNER = 1024
SSD_HEAD_DIM = 64
SSD_HEADS = SSD_INNER // SSD_HEAD_DIM
SSD_GROUPS = 2
SSD_STATE = 128
N_DIR = 2
FF_DIM = 4 * D_MODEL
EPS = 1e-6
NEG_INF = -1e30

BLK = 128
LANES = 128
BF16_SUBLANES = 16
VMEM_LIMIT = 56 * 1024 * 1024
_LOG2E = math.log2(math.e)

COL_Z = 0
COL_XS = 1024
COL_GATE = 2048
COL_Q = 5120
COL_GU = 5632
COL_GV = 6144
COL_BC = 6656
COL_KV = 7168
P_WIDTH = 7424
DT_REP = 32
SSD_BC_WIDTH = 2 * SSD_GROUPS * SSD_STATE

_R_Q, _R_K, _R_V, _R_GU, _R_GV, _R_Z, _R_XS, _R_B, _R_C, _R_DT, _R_GATE, _R_END = (
    0, 512, 640, 768, 1280, 1792, 2816, 3840, 4096, 4352, 4384, 7456)


class Layout(NamedTuple):
    n_blocks: int
    n_first: int
    per_seq_first: int
    per_seq_second: int


def _seq_pos(blk, lay):
    in_first = blk < lay.n_first
    pos = jnp.where(in_first, lax.rem(blk, lay.per_seq_first),
                    lax.rem(blk - lay.n_first, lay.per_seq_second))
    last = jnp.where(in_first, lay.per_seq_first - 1, lay.per_seq_second - 1)
    return pos == 0, pos == last


def _sigmoid(x):
    return 1.0 / (1.0 + jnp.exp(-x))


def _silu(x):
    return x * _sigmoid(x)


def _gelu(x):
    return x * (0.5 * (1.0 + jnp.tanh(math.sqrt(2.0 / math.pi) * (x + 0.044715 * (x * x * x)))))


def _softplus(x):
    return jnp.maximum(x, 0.0) + jnp.log1p(jnp.exp(-jnp.abs(x)))


def _rms(x, g):
    return x * lax.rsqrt(jnp.mean(x * x, axis=-1, keepdims=True) + EPS) * g


def _dot(a, b):
    return jnp.dot(a, b, preferred_element_type=F32)


def _dot_nt(a, b):
    return lax.dot_general(a, b, (((1,), (1,)), ((), ())), preferred_element_type=F32)


def _const_spec(shape):
    nd = len(shape)
    return pl.BlockSpec(shape, lambda *_: (0,) * nd)


def _params(semantics):
    return pltpu.CompilerParams(dimension_semantics=semantics, vmem_limit_bytes=VMEM_LIMIT)


_PROJ_CHUNK = 512


def _inproj_body(x_ref, g_ref, w_ref, wdt_ref, wdtT_ref, o_ref, dt_ref, dtT_ref):
    xn = _rms(x_ref[...], g_ref[...]).astype(BF16)
    for c0 in range(0, P_WIDTH, _PROJ_CHUNK):
        c1 = min(c0 + _PROJ_CHUNK, P_WIDTH)
        o_ref[:, c0:c1] = _dot(xn, w_ref[:, c0:c1]).astype(BF16)
    dt_ref[...] = _dot(xn, wdt_ref[...])
    dtT_ref[...] = _dot_nt(wdtT_ref[...], xn)


def _inproj(x, g, w, wdt, wdtT, tm=512):
    t = x.shape[0]
    return pl.pallas_call(
        _inproj_body,
        grid=(t // tm,),
        in_specs=[
            pl.BlockSpec((tm, D_MODEL), lambda i: (i, 0)),
            _const_spec((1, D_MODEL)),
            _const_spec((D_MODEL, P_WIDTH)),
            _const_spec((D_MODEL, LANES)),
            _const_spec((DT_REP, D_MODEL)),
        ],
        out_specs=[
            pl.BlockSpec((tm, P_WIDTH), lambda i: (i, 0)),
            pl.BlockSpec((tm, LANES), lambda i: (i, 0)),
            pl.BlockSpec((DT_REP, tm), lambda i: (0, i)),
        ],
        out_shape=[
            jax.ShapeDtypeStruct((t, P_WIDTH), BF16),
            jax.ShapeDtypeStruct((t, LANES), F32),
            jax.ShapeDtypeStruct((DT_REP, t), F32),
        ],
        compiler_params=_params(("parallel",)),
        name="inproj",
    )(x, g, w, wdt, wdtT)


_ATTN_TILE_BLOCKS = 4
_ATTN_LOOKAHEAD = 4


def _split2(x):
    hi = x.astype(BF16)
    return hi, (x - hi.astype(F32)).astype(BF16)


def _attn_body(q_ref, kvp_ref, kvm_ref, kvn_ref, bias_ref, kscale_ref, sink_ref, hsum_ref, sel_ref, o_ref, *, lay):
    i = pl.program_id(0)
    nt = _ATTN_TILE_BLOCKS
    grp = ATTN_HEADS // ATTN_KV_HEADS
    kv = jnp.concatenate([kvp_ref[...], kvm_ref[...], kvn_ref[...]], axis=0)
    kf = kv[:, :ATTN_KV_WIDTH].astype(F32)
    khi, klo = _split2(kf * kf)
    kss = _dot(jnp.concatenate([khi, klo], axis=1), hsum_ref[...])
    kn = (kf * lax.rsqrt(kss * (1.0 / HEAD_DIM) + EPS) * kscale_ref[...]).astype(BF16)
    kn = [kn[:, g * HEAD_DIM:(g + 1) * HEAD_DIM] for g in range(ATTN_KV_HEADS)]
    vT = jnp.transpose(kv[:, ATTN_KV_WIDTH:].astype(F32)).astype(BF16)
    neg_prev = jnp.where(_seq_pos(i * nt, lay)[0], NEG_INF, 0.0)
    neg_next = jnp.where(_seq_pos(i * nt + nt - 1, lay)[1], NEG_INF, 0.0)
    qbs, rqs = [], []
    for jb in range(nt):
        qb = q_ref[jb * BLK:(jb + 1) * BLK, :]
        qf = qb.astype(F32)
        qhi, qlo = _split2(qf * qf)
        ssq = _dot_nt(sel_ref[...], jnp.concatenate([qhi, qlo], axis=1))
        qbs.append(qb)
        rqs.append(lax.rsqrt(ssq * (1.0 / HEAD_DIM) + EPS))

    def scores(jb, h):
        return _dot_nt(kn[h // grp][jb * BLK:(jb + 3) * BLK], qbs[jb][:, h * HEAD_DIM:(h + 1) * HEAD_DIM])

    items = [(jb, h) for jb in range(nt) for h in range(ATTN_HEADS)]
    pending = [scores(*it) for it in items[:_ATTN_LOOKAHEAD]]
    outs = []
    for idx, (jb, h) in enumerate(items):
        s = pending.pop(0)
        if idx + _ATTN_LOOKAHEAD < len(items):
            pending.append(scores(*items[idx + _ATTN_LOOKAHEAD]))
        g = h // grp
        s = s * rqs[jb][h:h + 1, :] + bias_ref[h]
        if jb == 0:
            s = jnp.concatenate([s[:BLK] + neg_prev, s[BLK:]], axis=0)
        if jb == nt - 1:
            s = jnp.concatenate([s[:2 * BLK], s[2 * BLK:] + neg_next], axis=0)
        snk = sink_ref[0:1, h:h + 1]
        m = jnp.maximum(jnp.max(s, axis=0, keepdims=True), snk)
        p = jnp.exp2(s - m)
        denom = jnp.sum(p, axis=0, keepdims=True) + jnp.exp2(snk - m)
        o = _dot(vT[g * HEAD_DIM:(g + 1) * HEAD_DIM, jb * BLK:(jb + 3) * BLK], p.astype(BF16))
        outs.append(o * (1.0 / denom))
        if h == ATTN_HEADS - 1:
            o_ref[jb * BLK:(jb + 1) * BLK, :] = jnp.transpose(jnp.concatenate(outs, axis=0)).astype(BF16)
            outs = []


def _attention(proj, bias, kscale, sink, lay):
    t = proj.shape[0]
    nt = _ATTN_TILE_BLOCKS
    rows = nt * BLK
    q_blk = COL_Q // ATTN_Q_WIDTH
    kv_w = 2 * ATTN_KV_WIDTH
    kv_blk = COL_KV // kv_w
    last_blk = lay.n_blocks - 1
    assert lay.per_seq_first % nt == 0 and lay.per_seq_second % nt == 0 and lay.n_first % nt == 0
    k_idx = np.arange(2 * ATTN_KV_WIDTH)[:, None]
    n_idx = np.arange(ATTN_KV_WIDTH)[None, :]
    hsum = jnp.asarray((k_idx % ATTN_KV_WIDTH) // HEAD_DIM == n_idx // HEAD_DIM, dtype=BF16)
    h_idx = np.arange(BF16_SUBLANES)[:, None]
    c_idx = np.arange(2 * ATTN_Q_WIDTH)[None, :]
    sel = jnp.asarray((c_idx % ATTN_Q_WIDTH) // HEAD_DIM == h_idx, dtype=BF16)
    return pl.pallas_call(
        functools.partial(_attn_body, lay=lay),
        grid=(t // rows,),
        in_specs=[
            pl.BlockSpec((rows, ATTN_Q_WIDTH), lambda i: (i, q_blk)),
            pl.BlockSpec((BLK, kv_w), lambda i: (jnp.maximum(i * nt - 1, 0), kv_blk)),
            pl.BlockSpec((rows, kv_w), lambda i: (i, kv_blk)),
            pl.BlockSpec((BLK, kv_w), lambda i: (jnp.minimum(i * nt + nt, last_blk), kv_blk)),
            _const_spec((ATTN_HEADS, 3 * BLK, BLK)),
            _const_spec((1, ATTN_KV_WIDTH)),
            _const_spec((1, ATTN_HEADS)),
            _const_spec((2 * ATTN_KV_WIDTH, ATTN_KV_WIDTH)),
            _const_spec((BF16_SUBLANES, 2 * ATTN_Q_WIDTH)),
        ],
        out_specs=pl.BlockSpec((rows, ATTN_Q_WIDTH), lambda i: (i, 0)),
        out_shape=jax.ShapeDtypeStruct((t, ATTN_Q_WIDTH), BF16),
        compiler_params=_params(("parallel",)),
        name="attention",
    )(proj, proj, proj, proj, bias, kscale, sink, hsum, sel)


_GMLP_TILE_BLOCKS = 4


def _gmlp_body(gu_ref, gv_ref, lng_ref, lnb_ref, ws_ref, bs_ref, o_ref):
    u = _gelu(gu_ref[...].astype(F32))
    vf = _gelu(gv_ref[...].astype(F32))
    mu = jnp.mean(vf, axis=-1, keepdims=True)
    vc = vf - mu
    var = jnp.mean(vc * vc, axis=-1, keepdims=True)
    vn = (vc * lax.rsqrt(var + EPS) * lng_ref[...] + lnb_ref[...]).astype(BF16)
    gw = GMLP_WIDTH // GMLP_GROUPS
    for c in range(_GMLP_TILE_BLOCKS):
        r0, r1 = c * BLK, (c + 1) * BLK
        for g in range(GMLP_GROUPS):
            l0, l1 = g * gw, (g + 1) * gw
            sv = _dot(ws_ref[g], vn[r0:r1, l0:l1]) + bs_ref[:, l0:l1]
            o_ref[r0:r1, l0:l1] = (u[r0:r1, l0:l1] * sv).astype(BF16)


def _gmlp(proj, lng, lnb, ws, bs):
    t = proj.shape[0]
    rows = _GMLP_TILE_BLOCKS * BLK
    return pl.pallas_call(
        _gmlp_body,
        grid=(t // rows,),
        in_specs=[
            pl.BlockSpec((rows, GMLP_WIDTH), lambda i: (i, COL_GU // GMLP_WIDTH)),
            pl.BlockSpec((rows, GMLP_WIDTH), lambda i: (i, COL_GV // GMLP_WIDTH)),
            _const_spec((1, GMLP_WIDTH)),
            _const_spec((1, GMLP_WIDTH)),
            _const_spec((GMLP_GROUPS, BLK, BLK)),
            _const_spec((BLK, GMLP_WIDTH)),
        ],
        out_specs=pl.BlockSpec((rows, GMLP_WIDTH), lambda i: (i, 0)),
        out_shape=jax.ShapeDtypeStruct((t, GMLP_WIDTH), BF16),
        compiler_params=_params(("parallel",)),
        name="gmlp",
    )(proj, proj, lng, lnb, ws, bs)


_SPLIT_TERMS = 4


def _lane_split(x):
    rep = lax.broadcasted_iota(jnp.int32, x.shape, 1) // DT_REP
    out = jnp.zeros(x.shape, BF16)
    rem = x
    for r in range(_SPLIT_TERMS):
        term = rem.astype(BF16)
        out = jnp.where(rep == r, term, out)
        rem = rem - term.astype(F32)
    return out


def _sublane_split(x):
    terms, rem = [], x
    for _ in range(_SPLIT_TERMS):
        term = rem.astype(BF16)
        terms.append(term)
        rem = rem - term.astype(F32)
    return jnp.concatenate(terms, axis=0)


def _expand(x, e_ref):
    return _dot(_lane_split(x), e_ref[...])


def _conv_silu(main_ref, prev_ref, next_ref, w_ref, b_ref, first, last):
    x = main_ref[...].astype(F32)
    n = x.shape[0]
    halo_p = prev_ref[BF16_SUBLANES - 1:BF16_SUBLANES, :].astype(F32) * jnp.where(first, 0.0, 1.0)
    halo_n = next_ref[0:1, :].astype(F32) * jnp.where(last, 0.0, 1.0)
    row = lax.broadcasted_iota(jnp.int32, x.shape, 0)
    x_m1 = jnp.where(row == 0, halo_p, pltpu.roll(x, 1, axis=0))
    x_p1 = jnp.where(row == n - 1, halo_n, pltpu.roll(x, n - 1, axis=0))
    conv = b_ref[...] + x_m1 * w_ref[0:1, :] + x * w_ref[1:2, :] + x_p1 * w_ref[2:3, :]
    return _silu(conv)


def _tri(upper):
    r = lax.broadcasted_iota(jnp.int32, (BLK, BLK), 0)
    c = lax.broadcasted_iota(jnp.int32, (BLK, BLK), 1)
    return (c >= r) if upper else (c <= r)


def _ones_where(mask):
    return jnp.where(mask, 1.0, 0.0).astype(BF16)


_SSD_TILE_CHUNKS = 4


def _rows(j):
    return slice(j * BLK, (j + 1) * BLK)


def _dt_row_major(dt_raw, dtb, alog):
    L = BLK
    dt = _softplus(dt_raw + dtb)
    sp = _lane_split(dt * (-jnp.exp(alog)))
    lane = lax.broadcasted_iota(jnp.int32, (L, LANES), 1)
    is_fwd = (lane & (DT_REP - 1)) < SSD_HEADS
    part = jnp.where(is_fwd, _dot(_ones_where(_tri(False)), sp), _dot(_ones_where(_tri(True)), sp))
    acum = part
    for r in range(1, _SPLIT_TERMS):
        acum = acum + pltpu.roll(part, r * DT_REP, axis=1)
    a_end = jnp.where(is_fwd[0:1], acum[L - 1:L], acum[0:1])
    return dt, acum, a_end


def _dt_head_major(dtT_raw, dtbT, alogT):
    nh = SSD_HEADS
    dtT = _softplus(dtT_raw + dtbT)
    spT = _sublane_split(dtT * (-jnp.exp(alogT)))
    pf = _dot(spT, _ones_where(_tri(True)))
    pb = _dot(spT, _ones_where(_tri(False)))
    acumT_f = pf[0:nh]
    acumT_b = pb[nh:2 * nh]
    for r in range(1, _SPLIT_TERMS):
        acumT_f = acumT_f + pf[r * DT_REP:r * DT_REP + nh]
        acumT_b = acumT_b + pb[r * DT_REP + nh:(r + 1) * DT_REP]
    log2_dtT = jnp.log(dtT) * _LOG2E
    return acumT_f * _LOG2E - log2_dtT[0:nh], acumT_b * _LOG2E - log2_dtT[nh:2 * nh]


def _state_factors(dt, acum, a_end, e_ref):
    ea_exp = _expand(jnp.exp(acum), e_ref)
    w_exp = _expand(dt * jnp.exp(a_end - acum), e_ref)
    cd_exp = _expand(jnp.broadcast_to(jnp.exp(a_end), (8, LANES)), e_ref)[0:1]
    return ea_exp, w_exp, cd_exp


def _state_sweep(st_ref, order, xs, bm, cm_bf, fac):
    gw = SSD_INNER // SSD_GROUPS
    upd = {}
    for j in order:
        wx = (xs[j] * fac[j][1]).astype(BF16)
        upd[j] = [_dot(jnp.transpose(bm[j][g]).astype(BF16), wx[:, g * gw:(g + 1) * gw])
                  for g in range(SSD_GROUPS)]
    sg = [st_ref[g] for g in range(SSD_GROUPS)]
    out = {}
    for j in order:
        y_off = [_dot(cm_bf[j][g], sg[g].astype(BF16)) for g in range(SSD_GROUPS)]
        sg = [sg[g] * fac[j][2][:, g * gw:(g + 1) * gw] + upd[j][g] for g in range(SSD_GROUPS)]
        out[j] = jnp.concatenate(y_off, axis=1) * fac[j][0]
    for g in range(SSD_GROUPS):
        st_ref[g] = sg[g]
    return out


def _split_bc(bc):
    n = SSD_STATE
    return ([bc[:, g * n:(g + 1) * n] for g in range(SSD_GROUPS)],
            [bc[:, (SSD_GROUPS + g) * n:(SSD_GROUPS + g + 1) * n] for g in range(SSD_GROUPS)])


def _ssd_bwd_body(xs_ref, xsp_ref, xsn_ref, bc_ref, bcp_ref, bcn_ref, dt_ref,
                  wxs_ref, bxs_ref, wbc_ref, bbc_ref, dtb_ref, alog_ref, eb_ref,
                  xsc_ref, bcc_ref, yo_ref, st_ref, *, lay):
    ns = _SSD_TILE_CHUNKS
    blk0 = (lay.n_blocks // ns - 1 - pl.program_id(0)) * ns
    first = _seq_pos(blk0, lay)[0]
    last = _seq_pos(blk0 + ns - 1, lay)[1]

    @pl.when(last)
    def _():
        st_ref[...] = jnp.zeros(st_ref.shape, F32)

    order = list(reversed(range(ns)))
    rm = {j: _dt_row_major(dt_ref[_rows(j), :], dtb_ref[...], alog_ref[...]) for j in order}
    fac = {j: _state_factors(*rm[j], eb_ref) for j in order}
    xs_t = _conv_silu(xs_ref, xsp_ref, xsn_ref, wxs_ref, bxs_ref, first, last)
    bc_t = _conv_silu(bc_ref, bcp_ref, bcn_ref, wbc_ref, bbc_ref, first, last)
    xsc_ref[...] = xs_t.astype(BF16)
    bc_bf = bc_t.astype(BF16)
    bcc_ref[...] = bc_bf
    xs = {j: xs_t[_rows(j)] for j in order}
    bm = {j: _split_bc(bc_t[_rows(j)])[0] for j in order}
    cm_bf = {j: _split_bc(bc_bf[_rows(j)])[1] for j in order}
    y_off = _state_sweep(st_ref, order, xs, bm, cm_bf, fac)
    for j in order:
        yo_ref[_rows(j), :] = y_off[j].astype(BF16)


def _ssd_fwd_body(xsc_ref, bcc_ref, dt_ref, dtT_ref, yob_ref, dtb_ref, alog_ref, dtbT_ref, alogT_ref,
                  dskip_ref, ef_ref, eb_ref, y_ref, st_ref, *, lay):
    ns = _SSD_TILE_CHUNKS
    first = _seq_pos(pl.program_id(0) * ns, lay)[0]

    @pl.when(first)
    def _():
        st_ref[...] = jnp.zeros(st_ref.shape, F32)

    L, nh = BLK, SSD_HEADS
    hpg = nh // SSD_GROUPS
    order = list(range(ns))
    lane = lax.broadcasted_iota(jnp.int32, (L, LANES), 1)
    xs_bf = {j: xsc_ref[_rows(j), :] for j in order}
    bc_bf = {j: bcc_ref[_rows(j), :] for j in order}
    bm_bf = {j: _split_bc(bc_bf[j])[0] for j in order}
    cm_bf = {j: _split_bc(bc_bf[j])[1] for j in order}
    cb = {j: [_dot_nt(cm_bf[j][g], bm_bf[j][g]) for g in range(SSD_GROUPS)] for j in order}
    rm = {j: _dt_row_major(dt_ref[_rows(j), :], dtb_ref[...], alog_ref[...]) for j in order}
    hm = {j: _dt_head_major(dtT_ref[:, _rows(j)], dtbT_ref[...], alogT_ref[...]) for j in order}
    fac = {j: _state_factors(*rm[j], ef_ref) for j in order}
    xs = {j: xs_bf[j].astype(F32) for j in order}
    bm, coef_exp = {}, {}
    for j in order:
        bm[j], cm = _split_bc(bc_bf[j].astype(F32))
        cbd = jnp.where((lane & (nh - 1)) < hpg, jnp.sum(cm[0] * bm[j][0], axis=1, keepdims=True),
                        jnp.sum(cm[1] * bm[j][1], axis=1, keepdims=True))
        coef_exp[j] = _expand(dskip_ref[...] + cbd * rm[j][0], eb_ref)
    y_off = _state_sweep(st_ref, order, xs, bm, cm_bf, fac)

    causal = _tri(False)
    left = lane < SSD_HEAD_DIM
    zero_bf = jnp.zeros((L, LANES), BF16)
    acum2 = {j: rm[j][1] * _LOG2E for j in order}
    y_loc = {j: [] for j in order}
    for k in range(nh // 2):
        g = (2 * k) // hpg
        for j in order:
            lhs = []
            for h in (2 * k, 2 * k + 1):
                col_f = jnp.broadcast_to(acum2[j][:, h:h + 1], (L, L))
                col_b = jnp.broadcast_to(acum2[j][:, nh + h:nh + h + 1], (L, L))
                e = jnp.where(causal, col_f - hm[j][0][h:h + 1, :], col_b - hm[j][1][h:h + 1, :])
                lhs.append((cb[j][g] * jnp.exp2(e)).astype(BF16))
            xs_pair = xs_bf[j][:, k * LANES:(k + 1) * LANES]
            rhs = jnp.concatenate([jnp.where(left, xs_pair, zero_bf), jnp.where(left, zero_bf, xs_pair)], axis=0)
            y_loc[j].append(_dot(jnp.concatenate(lhs, axis=1), rhs))
    for j in order:
        y = jnp.concatenate(y_loc[j], axis=1) + coef_exp[j] * xs[j] + yob_ref[_rows(j), :].astype(F32) + y_off[j]
        y_ref[_rows(j), :] = y.astype(BF16)


def _ssd(proj, dt, dtT, p, e_mats, lay):
    t = proj.shape[0]
    ns = _SSD_TILE_CHUNKS
    assert lay.per_seq_first % ns == 0 and lay.per_seq_second % ns == 0 and lay.n_first % ns == 0
    nb = lay.n_blocks // ns
    rows = ns * BLK
    hal = rows // BF16_SUBLANES
    n_hal = nb * hal
    xs_blk, bc_blk = COL_XS // SSD_INNER, COL_BC // SSD_BC_WIDTH
    state = pltpu.VMEM((SSD_GROUPS, SSD_STATE, SSD_INNER // SSD_GROUPS), F32)

    def rev(c):
        return nb - 1 - c

    xsc, bcc, yob = pl.pallas_call(
        functools.partial(_ssd_bwd_body, lay=lay),
        grid=(nb,),
        in_specs=[
            pl.BlockSpec((rows, SSD_INNER), lambda c: (rev(c), xs_blk)),
            pl.BlockSpec((BF16_SUBLANES, SSD_INNER), lambda c: (jnp.maximum(rev(c) * hal - 1, 0), xs_blk)),
            pl.BlockSpec((BF16_SUBLANES, SSD_INNER), lambda c: (jnp.minimum((rev(c) + 1) * hal, n_hal - 1), xs_blk)),
            pl.BlockSpec((rows, SSD_BC_WIDTH), lambda c: (rev(c), bc_blk)),
            pl.BlockSpec((BF16_SUBLANES, SSD_BC_WIDTH), lambda c: (jnp.maximum(rev(c) * hal - 1, 0), bc_blk)),
            pl.BlockSpec((BF16_SUBLANES, SSD_BC_WIDTH),
                         lambda c: (jnp.minimum((rev(c) + 1) * hal, n_hal - 1), bc_blk)),
            pl.BlockSpec((rows, LANES), lambda c: (rev(c), 0)),
            _const_spec((3, SSD_INNER)), _const_spec((1, SSD_INNER)),
            _const_spec((3, SSD_BC_WIDTH)), _const_spec((1, SSD_BC_WIDTH)),
            _const_spec((1, LANES)), _const_spec((1, LANES)),
            _const_spec((LANES, SSD_INNER)),
        ],
        out_specs=[
            pl.BlockSpec((rows, SSD_INNER), lambda c: (rev(c), 0)),
            pl.BlockSpec((rows, SSD_BC_WIDTH), lambda c: (rev(c), 0)),
            pl.BlockSpec((rows, SSD_INNER), lambda c: (rev(c), 0)),
        ],
        out_shape=[
            jax.ShapeDtypeStruct((t, SSD_INNER), BF16),
            jax.ShapeDtypeStruct((t, SSD_BC_WIDTH), BF16),
            jax.ShapeDtypeStruct((t, SSD_INNER), BF16),
        ],
        scratch_shapes=[state],
        compiler_params=_params(("arbitrary",)),
        name="ssd_bwd",
    )(proj, proj, proj, proj, proj, proj, dt, p["conv_w_xs"], p["conv_b_xs"], p["conv_w_bc"], p["conv_b_bc"],
      p["dtb"], p["alog"], e_mats[1])

    return pl.pallas_call(
        functools.partial(_ssd_fwd_body, lay=lay),
        grid=(nb,),
        in_specs=[
            pl.BlockSpec((rows, SSD_INNER), lambda c: (c, 0)),
            pl.BlockSpec((rows, SSD_BC_WIDTH), lambda c: (c, 0)),
            pl.BlockSpec((rows, LANES), lambda c: (c, 0)),
            pl.BlockSpec((DT_REP, rows), lambda c: (0, c)),
            pl.BlockSpec((rows, SSD_INNER), lambda c: (c, 0)),
            _const_spec((1, LANES)), _const_spec((1, LANES)),
            _const_spec((DT_REP, 1)), _const_spec((DT_REP, 1)),
            _const_spec((1, LANES)),
            _const_spec((LANES, SSD_INNER)), _const_spec((LANES, SSD_INNER)),
        ],
        out_specs=pl.BlockSpec((rows, SSD_INNER), lambda c: (c, 0)),
        out_shape=jax.ShapeDtypeStruct((t, SSD_INNER), BF16),
        scratch_shapes=[state],
        compiler_params=_params(("arbitrary",)),
        name="ssd_fwd",
    )(xsc, bcc, dt, dtT, yob, p["dtb"], p["alog"], p["dtbT"], p["alogT"], p["dskip"], e_mats[0], e_mats[1])


def _merge_body(x_ref, ya_ref, yb_ref, yc_ref, z_ref, g0_ref, g1_ref, g2_ref, ng_ref,
                wa_ref, wb_ref, wc_ref, wo_ref, h_ref):
    y = yc_ref[...].astype(F32) * _silu(z_ref[...].astype(F32))
    gw = SSD_INNER // SSD_GROUPS
    yn = jnp.concatenate([_rms(y[:, g * gw:(g + 1) * gw], ng_ref[:, g * gw:(g + 1) * gw])
                          for g in range(SSD_GROUPS)], axis=1).astype(BF16)
    m = _sigmoid(g0_ref[...].astype(F32)) * _dot(ya_ref[...], wa_ref[...])
    m = m + _sigmoid(g1_ref[...].astype(F32)) * _dot(yb_ref[...], wb_ref[...])
    m = m + _sigmoid(g2_ref[...].astype(F32)) * _dot(yn, wc_ref[...])
    h_ref[...] = x_ref[...] + _dot(m.astype(BF16), wo_ref[...])


def _merge(x, ya, yb, yc, proj, ng, wa, wb, wc, wo, tm=512):
    t = x.shape[0]
    g_blk = COL_GATE // D_MODEL

    def row(width):
        return pl.BlockSpec((tm, width), lambda i: (i, 0))

    return pl.pallas_call(
        _merge_body,
        grid=(t // tm,),
        in_specs=[
            row(D_MODEL), row(ATTN_Q_WIDTH), row(GMLP_WIDTH), row(SSD_INNER),
            pl.BlockSpec((tm, SSD_INNER), lambda i: (i, COL_Z // SSD_INNER)),
            pl.BlockSpec((tm, D_MODEL), lambda i: (i, g_blk)),
            pl.BlockSpec((tm, D_MODEL), lambda i: (i, g_blk + 1)),
            pl.BlockSpec((tm, D_MODEL), lambda i: (i, g_blk + 2)),
            _const_spec((1, SSD_INNER)),
            _const_spec((ATTN_Q_WIDTH, D_MODEL)), _const_spec((GMLP_WIDTH, D_MODEL)),
            _const_spec((SSD_INNER, D_MODEL)), _const_spec((D_MODEL, D_MODEL)),
        ],
        out_specs=row(D_MODEL),
        out_shape=jax.ShapeDtypeStruct((t, D_MODEL), F32),
        compiler_params=_params(("parallel",)),
        name="merge",
    )(x, ya, yb, yc, proj, proj, proj, proj, ng, wa, wb, wc, wo)


_FF_CHUNK = 1024


def _ffn_body(h_ref, g_ref, w1_ref, w2_ref, o_ref):
    h = h_ref[...]
    hn = _rms(h, g_ref[...]).astype(BF16)
    acc = h
    for c0 in range(0, FF_DIM, _FF_CHUNK):
        a = jnp.maximum(_dot(hn, w1_ref[:, c0:c0 + _FF_CHUNK]), 0.0)
        acc = acc + _dot((a * a).astype(BF16), w2_ref[c0:c0 + _FF_CHUNK, :])
    o_ref[...] = acc


def _ffn(h, g, w1, w2, tm=512):
    t = h.shape[0]
    return pl.pallas_call(
        _ffn_body,
        grid=(t // tm,),
        in_specs=[
            pl.BlockSpec((tm, D_MODEL), lambda i: (i, 0)),
            _const_spec((1, D_MODEL)),
            _const_spec((D_MODEL, FF_DIM)),
            _const_spec((FF_DIM, D_MODEL)),
        ],
        out_specs=pl.BlockSpec((tm, D_MODEL), lambda i: (i, 0)),
        out_shape=jax.ShapeDtypeStruct((t, D_MODEL), F32),
        compiler_params=_params(("parallel",)),
        name="ffn",
    )(h, g, w1, w2)


def _t5_bucket(rel):
    nb = NUM_BUCKETS // 2
    max_exact = nb // 2
    ret = jnp.where(rel > 0, nb, 0)
    n = jnp.abs(rel)
    n_safe = jnp.maximum(n, 1).astype(F32)
    large = max_exact + (jnp.log(n_safe / max_exact) / math.log(MAX_DISTANCE / max_exact)
                         * (nb - max_exact)).astype(jnp.int32)
    large = jnp.minimum(large, nb - 1)
    return ret + jnp.where(n < max_exact, n, large)


def _bias_table(rel_bias):
    qi = jnp.arange(BLK)[:, None]
    ki = jnp.arange(3 * BLK)[None, :]
    rel = ki - BLK - qi
    bias = rel_bias[_t5_bucket(rel)].astype(F32).transpose(2, 1, 0) * _LOG2E
    return jnp.where((jnp.abs(rel) <= WINDOW).T[None], bias, NEG_INF)


def _expand_matrix(d):
    k = np.arange(LANES)[:, None]
    col = np.arange(SSD_INNER)[None, :]
    hit = ((k % DT_REP) == d * SSD_HEADS + col // SSD_HEAD_DIM)
    return jnp.asarray(hit, dtype=BF16)


def _prep_layer_params(w_in, norm_mix_g, q_norm_g, k_norm_g, attn_sink, gmlp_ln_g, gmlp_ln_b, w_spatial,
                       b_spatial, conv_w, conv_b, dt_bias, a_log, d_skip, ssd_norm_g, w_up_attn, w_up_gmlp,
                       w_up_ssd, w_out, norm_ff_g, w_ff1, w_ff2):
    depth = w_in.shape[0]

    def cols(a, b):
        return w_in[:, :, a:b]

    w_main = jnp.concatenate([
        cols(_R_Z, _R_XS), cols(_R_XS, _R_B), cols(_R_GATE, _R_END), cols(_R_Q, _R_K),
        cols(_R_GU, _R_GV), cols(_R_GV, _R_Z), cols(_R_B, _R_DT), cols(_R_K, _R_GU)], axis=-1).astype(BF16)
    w_dt = cols(_R_DT, _R_GATE)
    rep = LANES // DT_REP
    nxs = SSD_INNER
    gw = GMLP_WIDTH // GMLP_GROUPS

    def dt_lanes(v):
        return jnp.tile(v.reshape(depth, 1, DT_REP), (1, 1, rep))

    return dict(
        norm_mix_g=norm_mix_g[:, None, :],
        w_main=w_main,
        w_dt=jnp.tile(w_dt, (1, 1, rep)).astype(BF16),
        w_dtT=jnp.swapaxes(w_dt, 1, 2).astype(BF16),
        kscale=jnp.tile(q_norm_g * k_norm_g * (HEAD_DIM ** -0.5 * _LOG2E), (1, ATTN_KV_HEADS))[:, None, :],
        sink=attn_sink[:, None, :] * _LOG2E,
        ln_g=gmlp_ln_g[:, None, :], ln_b=gmlp_ln_b[:, None, :],
        w_s=w_spatial.astype(BF16),
        b_s=jnp.repeat(jnp.swapaxes(b_spatial, 1, 2), gw, axis=-1),
        conv_w_xs=conv_w[:, :, :nxs], conv_b_xs=conv_b[:, None, :nxs],
        conv_w_bc=conv_w[:, :, nxs:], conv_b_bc=conv_b[:, None, nxs:],
        dtb=dt_lanes(dt_bias), alog=dt_lanes(a_log),
        dtbT=dt_bias.reshape(depth, DT_REP, 1), alogT=a_log.reshape(depth, DT_REP, 1),
        dskip=dt_lanes(jnp.concatenate([d_skip, d_skip], axis=-1)),
        ssd_norm_g=ssd_norm_g[:, None, :],
        w_a=w_up_attn.astype(BF16), w_b=w_up_gmlp.astype(BF16), w_c=w_up_ssd.astype(BF16),
        w_o=w_out.astype(BF16),
        norm_ff_g=norm_ff_g[:, None, :],
        w1=w_ff1.astype(BF16), w2=w_ff2.astype(BF16),
    )


def _layer(x, p, bias, e_mats, lay):
    proj, dt, dtT = _inproj(x, p["norm_mix_g"], p["w_main"], p["w_dt"], p["w_dtT"])
    ya = _attention(proj, bias, p["kscale"], p["sink"], lay)
    yb = _gmlp(proj, p["ln_g"], p["ln_b"], p["w_s"], p["b_s"])
    yc = _ssd(proj, dt, dtT, p, e_mats, lay)
    h = _merge(x, ya, yb, yc, proj, p["ssd_norm_g"], p["w_a"], p["w_b"], p["w_c"], p["w_o"])
    return _ffn(h, p["norm_ff_g"], p["w1"], p["w2"])


def _encoder(x_first, x_second, rel_bias, layer_params):
    b1, s1, _ = x_first.shape
    b2, s2, _ = x_second.shape
    t1, t2 = b1 * s1, b2 * s2
    lay = Layout(n_blocks=(t1 + t2) // BLK, n_first=t1 // BLK, per_seq_first=s1 // BLK,
                 per_seq_second=s2 // BLK)
    x = jnp.concatenate([x_first.reshape(t1, D_MODEL), x_second.reshape(t2, D_MODEL)], axis=0)
    bias = _bias_table(rel_bias)
    e_mats = (_expand_matrix(0), _expand_matrix(1))
    params = _prep_layer_params(**layer_params)

    def step(carry, p):
        return _layer(carry, p, bias, e_mats, lay), None

    x, _ = lax.scan(step, x, params)
    return x[:t1].reshape(b1, s1, D_MODEL), x[t1:].reshape(b2, s2, D_MODEL)


def kernel(x_prompt, x_sample, rel_bias, norm_mix_g, w_in, q_norm_g, k_norm_g, attn_sink, gmlp_ln_g, gmlp_ln_b,
           w_spatial, b_spatial, conv_w, conv_b, dt_bias, a_log, d_skip, ssd_norm_g, w_up_attn, w_up_gmlp,
           w_up_ssd, w_out, norm_ff_g, w_ff1, w_ff2):
    layer_params = dict(
        w_in=w_in, norm_mix_g=norm_mix_g, q_norm_g=q_norm_g, k_norm_g=k_norm_g, attn_sink=attn_sink,
        gmlp_ln_g=gmlp_ln_g, gmlp_ln_b=gmlp_ln_b, w_spatial=w_spatial, b_spatial=b_spatial, conv_w=conv_w,
        conv_b=conv_b, dt_bias=dt_bias, a_log=a_log, d_skip=d_skip, ssd_norm_g=ssd_norm_g,
        w_up_attn=w_up_attn, w_up_gmlp=w_up_gmlp, w_up_ssd=w_up_ssd, w_out=w_out, norm_ff_g=norm_ff_g,
        w_ff1=w_ff1, w_ff2=w_ff2)
    return _encoder(x_prompt, x_sample, rel_bias, layer_params)
```

```python
import functools
import math
from typing import NamedTuple

import numpy as np
import jax
import jax.numpy as jnp
from jax import lax
from jax.experimental import pallas as pl
from jax.experimental.pallas import tpu as pltpu

F32 = jnp.float32
BF16 = jnp.bfloat16

D_MODEL = 1024
ATTN_HEADS = 8
ATTN_KV_HEADS = 2
HEAD_DIM = 64
ATTN_Q_WIDTH = ATTN_HEADS * HEAD_DIM
ATTN_KV_WIDTH = ATTN_KV_HEADS * HEAD_DIM
WINDOW = 128
NUM_BUCKETS = 32
MAX_DISTANCE = 128
GMLP_WIDTH = 512
GMLP_GROUPS = 4
SSD_INNER = 1024
SSD_HEAD_DIM = 64
SSD_HEADS = SSD_INNER // SSD_HEAD_DIM
SSD_GROUPS = 2
SSD_STATE = 128
N_DIR = 2
FF_DIM = 4 * D_MODEL
EPS = 1e-6
NEG_INF = -1e30

BLK = 128
LANES = 128
BF16_SUBLANES = 16
VMEM_LIMIT = 56 * 1024 * 1024
_LOG2E = math.log2(math.e)

COL_Z = 0
COL_XS = 1024
COL_GATE = 2048
COL_Q = 5120
COL_GU = 5632
COL_GV = 6144
COL_BC = 6656
COL_KV = 7168
P_WIDTH = 7424
DT_REP = 32
SSD_BC_WIDTH = 2 * SSD_GROUPS * SSD_STATE

_R_Q, _R_K, _R_V, _R_GU, _R_GV, _R_Z, _R_XS, _R_B, _R_C, _R_DT, _R_GATE, _R_END = (
    0, 512, 640, 768, 1280, 1792, 2816, 3840, 4096, 4352, 4384, 7456)


class Layout(NamedTuple):
    n_blocks: int
    n_first: int
    per_seq_first: int
    per_seq_second: int


def _seq_pos(blk, lay):
    in_first = blk < lay.n_first
    pos = jnp.where(in_first, lax.rem(blk, lay.per_seq_first),
                    lax.rem(blk - lay.n_first, lay.per_seq_second))
    last = jnp.where(in_first, lay.per_seq_first - 1, lay.per_seq_second - 1)
    return pos == 0, pos == last


def _sigmoid(x):
    return 1.0 / (1.0 + jnp.exp(-x))


def _silu(x):
    return x * _sigmoid(x)


def _gelu(x):
    return x * (0.5 * (1.0 + jnp.tanh(math.sqrt(2.0 / math.pi) * (x + 0.044715 * (x * x * x)))))


def _softplus(x):
    return jnp.maximum(x, 0.0) + jnp.log1p(jnp.exp(-jnp.abs(x)))


def _rms(x, g):
    return x * lax.rsqrt(jnp.mean(x * x, axis=-1, keepdims=True) + EPS) * g


def _dot(a, b):
    return jnp.dot(a, b, preferred_element_type=F32)


def _dot_nt(a, b):
    return lax.dot_general(a, b, (((1,), (1,)), ((), ())), preferred_element_type=F32)


def _const_spec(shape):
    nd = len(shape)
    return pl.BlockSpec(shape, lambda *_: (0,) * nd)


def _params(semantics):
    return pltpu.CompilerParams(dimension_semantics=semantics, vmem_limit_bytes=VMEM_LIMIT)


_PROJ_TILE = 512
_PROJ_LOOKAHEAD = 1
_X_HALO = 8


def _split2(x):
    hi = x.astype(BF16)
    return hi, (x - hi.astype(F32)).astype(BF16)


def _inproj_body(x_ref, xprev_ref, xnext_ref, g_ref, w_ref, wdt_ref, wdtT_ref, cw_ref, cb_ref, lng_ref, lnb_ref,
                 kscale_ref, qsum_ref, ksum_ref, o_ref, dt_ref, dtT_ref, *, lay):
    tm = _PROJ_TILE
    nblk = tm // BLK
    i = pl.program_id(0)
    keep_prev = jnp.where(_seq_pos(i * nblk, lay)[0], 0.0, 1.0)
    keep_next = jnp.where(_seq_pos(i * nblk + nblk - 1, lay)[1], 0.0, 1.0)
    g = g_ref[...]
    xn = _rms(x_ref[...], g).astype(BF16)
    xh = _rms(jnp.concatenate([xprev_ref[...], xnext_ref[...]], axis=0), g).astype(BF16)

    def put(c0, val):
        o_ref[:, c0:c0 + val.shape[1]] = val.astype(BF16)

    def conv_silu(k0):
        def post(c0, y):
            width = y.shape[1]
            yh = _dot(xh, w_ref[:, c0:c0 + width])
            halo_p = yh[_X_HALO - 1:_X_HALO] * keep_prev
            halo_n = yh[_X_HALO:_X_HALO + 1] * keep_next
            up = pltpu.roll(y, 1, axis=0)
            dn = pltpu.roll(y, tm - 1, axis=0)
            r8 = lax.broadcasted_iota(jnp.int32, (8, width), 0)
            y_m1 = jnp.concatenate([jnp.where(r8 == 0, halo_p, up[:8]), up[8:]], axis=0)
            y_p1 = jnp.concatenate([dn[:tm - 8], jnp.where(r8 == 7, halo_n, dn[tm - 8:])], axis=0)
            w = cw_ref[:, k0:k0 + width]
            put(c0, _silu(cb_ref[:, k0:k0 + width] + y_m1 * w[0:1] + y * w[1:2] + y_p1 * w[2:3]))
        return post

    def q_norm(c0, q):
        qss = _dot(jnp.concatenate(_split2(q * q), axis=1), qsum_ref[...])
        put(c0, q * lax.rsqrt(qss * (1.0 / HEAD_DIM) + EPS))

    def gelu_ln(c0, y):
        vf = _gelu(y)
        vc = vf - jnp.mean(vf, axis=-1, keepdims=True)
        var = jnp.mean(vc * vc, axis=-1, keepdims=True)
        put(c0, vc * lax.rsqrt(var + EPS) * lng_ref[...] + lnb_ref[...])

    def kv_norm(c0, kv):
        kf = kv[:, :ATTN_KV_WIDTH]
        kss = _dot(jnp.concatenate(_split2(kf * kf), axis=1), ksum_ref[...])
        put(c0, kf * lax.rsqrt(kss * (1.0 / HEAD_DIM) + EPS) * kscale_ref[...])
        put(c0 + ATTN_KV_WIDTH, kv[:, ATTN_KV_WIDTH:])

    def gate(k):
        return (COL_GATE + k * D_MODEL, D_MODEL, lambda c0, y: put(c0, _sigmoid(y)))

    items = [
        (COL_XS, SSD_INNER, conv_silu(0)),
        (COL_Z, SSD_INNER, lambda c0, y: put(c0, _silu(y))),
        (COL_BC, SSD_BC_WIDTH, conv_silu(SSD_INNER)),
        gate(0),
        (COL_GV, GMLP_WIDTH, gelu_ln),
        gate(1),
        (COL_GU, GMLP_WIDTH, lambda c0, y: put(c0, _gelu(y))),
        gate(2),
        (COL_Q, ATTN_Q_WIDTH, q_norm),
        (COL_KV, 2 * ATTN_KV_WIDTH, kv_norm),
    ]

    def proj(k):
        c0, width, _ = items[k]
        return _dot(xn, w_ref[:, c0:c0 + width])

    pending = [proj(k) for k in range(_PROJ_LOOKAHEAD)]
    for k, (c0, _, post) in enumerate(items):
        y = pending.pop(0)
        if k + _PROJ_LOOKAHEAD < len(items):
            pending.append(proj(k + _PROJ_LOOKAHEAD))
        post(c0, y)
    dt_ref[...] = _dot(xn, wdt_ref[...])
    dtT_ref[...] = _dot_nt(wdtT_ref[...], xn)


def _head_sum_matrix(width):
    k = np.arange(2 * width)[:, None]
    n = np.arange(width)[None, :]
    return jnp.asarray((k % width) // HEAD_DIM == n // HEAD_DIM, dtype=BF16)


def _inproj(x, p, lay):
    t = x.shape[0]
    tm = _PROJ_TILE
    assert lay.per_seq_first % (tm // BLK) == 0 and lay.per_seq_second % (tm // BLK) == 0
    hal = tm // _X_HALO
    n_hal = t // _X_HALO
    conv_w = SSD_INNER + SSD_BC_WIDTH
    return pl.pallas_call(
        functools.partial(_inproj_body, lay=lay),
        grid=(t // tm,),
        in_specs=[
            pl.BlockSpec((tm, D_MODEL), lambda i: (i, 0)),
            pl.BlockSpec((_X_HALO, D_MODEL), lambda i: (jnp.maximum(i * hal - 1, 0), 0)),
            pl.BlockSpec((_X_HALO, D_MODEL), lambda i: (jnp.minimum((i + 1) * hal, n_hal - 1), 0)),
            _const_spec((1, D_MODEL)),
            _const_spec((D_MODEL, P_WIDTH)),
            _const_spec((D_MODEL, LANES)),
            _const_spec((DT_REP, D_MODEL)),
            _const_spec((3, conv_w)), _const_spec((1, conv_w)),
            _const_spec((1, GMLP_WIDTH)), _const_spec((1, GMLP_WIDTH)),
            _const_spec((1, ATTN_KV_WIDTH)),
            _const_spec((2 * ATTN_Q_WIDTH, ATTN_Q_WIDTH)),
            _const_spec((2 * ATTN_KV_WIDTH, ATTN_KV_WIDTH)),
        ],
        out_specs=[
            pl.BlockSpec((tm, P_WIDTH), lambda i: (i, 0)),
            pl.BlockSpec((tm, LANES), lambda i: (i, 0)),
            pl.BlockSpec((DT_REP, tm), lambda i: (0, i)),
        ],
        out_shape=[
            jax.ShapeDtypeStruct((t, P_WIDTH), BF16),
            jax.ShapeDtypeStruct((t, LANES), F32),
            jax.ShapeDtypeStruct((DT_REP, t), F32),
        ],
        compiler_params=_params(("parallel",)),
        name="inproj",
    )(x, x, x, p["norm_mix_g"], p["w_main"], p["w_dt"], p["w_dtT"], p["conv_w"], p["conv_b"], p["ln_g"], p["ln_b"],
      p["kscale"], _head_sum_matrix(ATTN_Q_WIDTH), _head_sum_matrix(ATTN_KV_WIDTH))


_ATTN_TILE_BLOCKS = 4
_ATTN_LOOKAHEAD = 4


def _attn_body(q_ref, kvp_ref, kvm_ref, kvn_ref, bias_ref, sink_ref, o_ref, *, lay):
    i = pl.program_id(0)
    nt = _ATTN_TILE_BLOCKS
    grp = ATTN_HEADS // ATTN_KV_HEADS
    kv = jnp.concatenate([kvp_ref[...], kvm_ref[...], kvn_ref[...]], axis=0)
    kn = [kv[:, g * HEAD_DIM:(g + 1) * HEAD_DIM] for g in range(ATTN_KV_HEADS)]
    vT = jnp.transpose(kv[:, ATTN_KV_WIDTH:].astype(F32)).astype(BF16)
    neg_prev = jnp.where(_seq_pos(i * nt, lay)[0], NEG_INF, 0.0)
    neg_next = jnp.where(_seq_pos(i * nt + nt - 1, lay)[1], NEG_INF, 0.0)

    qbs = [q_ref[jb * BLK:(jb + 1) * BLK, :] for jb in range(nt)]

    def scores(jb, h):
        s = _dot_nt(kn[h // grp][jb * BLK:(jb + 3) * BLK], qbs[jb][:, h * HEAD_DIM:(h + 1) * HEAD_DIM])
        return s + bias_ref[h]

    items = [(jb, h) for jb in range(nt) for h in range(ATTN_HEADS)]
    pending = [scores(*it) for it in items[:_ATTN_LOOKAHEAD]]
    outs = []
    for idx, (jb, h) in enumerate(items):
        s = pending.pop(0)
        if idx + _ATTN_LOOKAHEAD < len(items):
            pending.append(scores(*items[idx + _ATTN_LOOKAHEAD]))
        g = h // grp
        if jb == 0:
            s = jnp.concatenate([s[:BLK] + neg_prev, s[BLK:]], axis=0)
        if jb == nt - 1:
            s = jnp.concatenate([s[:2 * BLK], s[2 * BLK:] + neg_next], axis=0)
        snk = sink_ref[0:1, h:h + 1]
        m = jnp.maximum(jnp.max(s, axis=0, keepdims=True), snk)
        p = jnp.exp2(s - m)
        denom = jnp.sum(p, axis=0, keepdims=True) + jnp.exp2(snk - m)
        o = _dot(vT[g * HEAD_DIM:(g + 1) * HEAD_DIM, jb * BLK:(jb + 3) * BLK], p.astype(BF16))
        outs.append(o * (1.0 / denom))
        if h == ATTN_HEADS - 1:
            o_ref[jb * BLK:(jb + 1) * BLK, :] = jnp.transpose(jnp.concatenate(outs, axis=0)).astype(BF16)
            outs = []


def _attention(proj, bias, sink, lay):
    t = proj.shape[0]
    nt = _ATTN_TILE_BLOCKS
    rows = nt * BLK
    q_blk = COL_Q // ATTN_Q_WIDTH
    kv_w = 2 * ATTN_KV_WIDTH
    kv_blk = COL_KV // kv_w
    last_blk = lay.n_blocks - 1
    assert lay.per_seq_first % nt == 0 and lay.per_seq_second % nt == 0 and lay.n_first % nt == 0
    return pl.pallas_call(
        functools.partial(_attn_body, lay=lay),
        grid=(t // rows,),
        in_specs=[
            pl.BlockSpec((rows, ATTN_Q_WIDTH), lambda i: (i, q_blk)),
            pl.BlockSpec((BLK, kv_w), lambda i: (jnp.maximum(i * nt - 1, 0), kv_blk)),
            pl.BlockSpec((rows, kv_w), lambda i: (i, kv_blk)),
            pl.BlockSpec((BLK, kv_w), lambda i: (jnp.minimum(i * nt + nt, last_blk), kv_blk)),
            _const_spec((ATTN_HEADS, 3 * BLK, BLK)),
            _const_spec((1, ATTN_HEADS)),
        ],
        out_specs=pl.BlockSpec((rows, ATTN_Q_WIDTH), lambda i: (i, 0)),
        out_shape=jax.ShapeDtypeStruct((t, ATTN_Q_WIDTH), BF16),
        compiler_params=_params(("parallel",)),
        name="attention",
    )(proj, proj, proj, proj, bias, sink)


_GMLP_TILE_BLOCKS = 4


def _gmlp_body(gu_ref, gv_ref, ws_ref, bs_ref, o_ref):
    gw = GMLP_WIDTH // GMLP_GROUPS
    for c in range(_GMLP_TILE_BLOCKS):
        r0, r1 = c * BLK, (c + 1) * BLK
        for g in range(GMLP_GROUPS):
            l0, l1 = g * gw, (g + 1) * gw
            sv = _dot(ws_ref[g], gv_ref[r0:r1, l0:l1]) + bs_ref[:, l0:l1]
            o_ref[r0:r1, l0:l1] = (gu_ref[r0:r1, l0:l1].astype(F32) * sv).astype(BF16)


def _gmlp(proj, ws, bs):
    t = proj.shape[0]
    rows = _GMLP_TILE_BLOCKS * BLK
    return pl.pallas_call(
        _gmlp_body,
        grid=(t // rows,),
        in_specs=[
            pl.BlockSpec((rows, GMLP_WIDTH), lambda i: (i, COL_GU // GMLP_WIDTH)),
            pl.BlockSpec((rows, GMLP_WIDTH), lambda i: (i, COL_GV // GMLP_WIDTH)),
            _const_spec((GMLP_GROUPS, BLK, BLK)),
            _const_spec((BLK, GMLP_WIDTH)),
        ],
        out_specs=pl.BlockSpec((rows, GMLP_WIDTH), lambda i: (i, 0)),
        out_shape=jax.ShapeDtypeStruct((t, GMLP_WIDTH), BF16),
        compiler_params=_params(("parallel",)),
        name="gmlp",
    )(proj, proj, ws, bs)


_SPLIT_TERMS = 4


def _lane_split(x):
    rep = lax.broadcasted_iota(jnp.int32, x.shape, 1) // DT_REP
    out = jnp.zeros(x.shape, BF16)
    rem = x
    for r in range(_SPLIT_TERMS):
        term = rem.astype(BF16)
        out = jnp.where(rep == r, term, out)
        rem = rem - term.astype(F32)
    return out


def _sublane_split(x):
    terms, rem = [], x
    for _ in range(_SPLIT_TERMS):
        term = rem.astype(BF16)
        terms.append(term)
        rem = rem - term.astype(F32)
    return jnp.concatenate(terms, axis=0)


def _expand(x, e_ref):
    return _dot(_lane_split(x), e_ref[...])


def _tri(upper):
    r = lax.broadcasted_iota(jnp.int32, (BLK, BLK), 0)
    c = lax.broadcasted_iota(jnp.int32, (BLK, BLK), 1)
    return (c >= r) if upper else (c <= r)


def _ones_where(mask):
    return jnp.where(mask, 1.0, 0.0).astype(BF16)


_SSD_TILE_CHUNKS = 4


def _rows(j):
    return slice(j * BLK, (j + 1) * BLK)


def _dt_row_major(dt_raw, dtb, alog):
    L = BLK
    dt = _softplus(dt_raw + dtb)
    sp = _lane_split(dt * (-jnp.exp(alog)))
    lane = lax.broadcasted_iota(jnp.int32, (L, LANES), 1)
    is_fwd = (lane & (DT_REP - 1)) < SSD_HEADS
    part = jnp.where(is_fwd, _dot(_ones_where(_tri(False)), sp), _dot(_ones_where(_tri(True)), sp))
    acum = part
    for r in range(1, _SPLIT_TERMS):
        acum = acum + pltpu.roll(part, r * DT_REP, axis=1)
    a_end = jnp.where(is_fwd[0:1], acum[L - 1:L], acum[0:1])
    return dt, acum, a_end


def _dt_head_major(dtT_raw, dtbT, alogT):
    nh = SSD_HEADS
    dtT = _softplus(dtT_raw + dtbT)
    spT = _sublane_split(dtT * (-jnp.exp(alogT)))
    pf = _dot(spT, _ones_where(_tri(True)))
    pb = _dot(spT, _ones_where(_tri(False)))
    acumT_f = pf[0:nh]
    acumT_b = pb[nh:2 * nh]
    for r in range(1, _SPLIT_TERMS):
        acumT_f = acumT_f + pf[r * DT_REP:r * DT_REP + nh]
        acumT_b = acumT_b + pb[r * DT_REP + nh:(r + 1) * DT_REP]
    log2_dtT = jnp.log(dtT) * _LOG2E
    return acumT_f * _LOG2E - log2_dtT[0:nh], acumT_b * _LOG2E - log2_dtT[nh:2 * nh]


def _state_factors(dt, acum, a_end, e_ref):
    ea_exp = _expand(jnp.exp(acum), e_ref)
    w_exp = _expand(dt * jnp.exp(a_end - acum), e_ref)
    cd_exp = _expand(jnp.broadcast_to(jnp.exp(a_end), (8, LANES)), e_ref)[0:1]
    return ea_exp, w_exp, cd_exp


def _state_sweep(st_ref, order, xs, bm, cm_bf, fac):
    gw = SSD_INNER // SSD_GROUPS
    upd = {}
    for j in order:
        wx = (xs[j] * fac[j][1]).astype(BF16)
        upd[j] = [_dot(jnp.transpose(bm[j][g]).astype(BF16), wx[:, g * gw:(g + 1) * gw])
                  for g in range(SSD_GROUPS)]
    sg = [st_ref[g] for g in range(SSD_GROUPS)]
    out = {}
    for j in order:
        y_off = [_dot(cm_bf[j][g], sg[g].astype(BF16)) for g in range(SSD_GROUPS)]
        neg_cd = -fac[j][2]
        sg = [upd[j][g] - sg[g] * neg_cd[:, g * gw:(g + 1) * gw] for g in range(SSD_GROUPS)]
        out[j] = jnp.concatenate(y_off, axis=1) * fac[j][0]
    for g in range(SSD_GROUPS):
        st_ref[g] = sg[g]
    return out


def _split_bc(bc):
    n = SSD_STATE
    return ([bc[:, g * n:(g + 1) * n] for g in range(SSD_GROUPS)],
            [bc[:, (SSD_GROUPS + g) * n:(SSD_GROUPS + g + 1) * n] for g in range(SSD_GROUPS)])


def _ssd_bwd_body(xs_ref, bc_ref, dt_ref, dtb_ref, alog_ref, eb_ref, yo_ref, st_ref, *, lay):
    ns = _SSD_TILE_CHUNKS
    blk0 = (lay.n_blocks // ns - 1 - pl.program_id(0)) * ns
    last = _seq_pos(blk0 + ns - 1, lay)[1]

    @pl.when(last)
    def _():
        st_ref[...] = jnp.zeros(st_ref.shape, F32)

    order = list(reversed(range(ns)))
    rm = {j: _dt_row_major(dt_ref[_rows(j), :], dtb_ref[...], alog_ref[...]) for j in order}
    fac = {j: _state_factors(*rm[j], eb_ref) for j in order}
    xs = {j: xs_ref[_rows(j), :].astype(F32) for j in order}
    bm = {j: _split_bc(bc_ref[_rows(j), :].astype(F32))[0] for j in order}
    cm_bf = {j: _split_bc(bc_ref[_rows(j), :])[1] for j in order}
    y_off = _state_sweep(st_ref, order, xs, bm, cm_bf, fac)
    for j in order:
        yo_ref[_rows(j), :] = y_off[j].astype(BF16)


def _ssd_fwd_body(xsc_ref, bcc_ref, dt_ref, dtT_ref, yob_ref, dtb_ref, alog_ref, dtbT_ref, alogT_ref,
                  dskip_ref, ef_ref, eb_ref, y_ref, st_ref, *, lay):
    ns = _SSD_TILE_CHUNKS
    first = _seq_pos(pl.program_id(0) * ns, lay)[0]

    @pl.when(first)
    def _():
        st_ref[...] = jnp.zeros(st_ref.shape, F32)

    L, nh = BLK, SSD_HEADS
    hpg = nh // SSD_GROUPS
    order = list(range(ns))
    lane = lax.broadcasted_iota(jnp.int32, (L, LANES), 1)
    xs_bf = {j: xsc_ref[_rows(j), :] for j in order}
    bc_bf = {j: bcc_ref[_rows(j), :] for j in order}
    bm_bf = {j: _split_bc(bc_bf[j])[0] for j in order}
    cm_bf = {j: _split_bc(bc_bf[j])[1] for j in order}
    cb = {j: [_dot_nt(cm_bf[j][g], bm_bf[j][g]) for g in range(SSD_GROUPS)] for j in order}
    rm = {j: _dt_row_major(dt_ref[_rows(j), :], dtb_ref[...], alog_ref[...]) for j in order}
    hm = {j: _dt_head_major(dtT_ref[:, _rows(j)], dtbT_ref[...], alogT_ref[...]) for j in order}
    fac = {j: _state_factors(*rm[j], ef_ref) for j in order}
    xs = {j: xs_bf[j].astype(F32) for j in order}
    bm, coef_exp = {}, {}
    for j in order:
        bm[j], cm = _split_bc(bc_bf[j].astype(F32))
        cbd = jnp.where((lane & (nh - 1)) < hpg, jnp.sum(cm[0] * bm[j][0], axis=1, keepdims=True),
                        jnp.sum(cm[1] * bm[j][1], axis=1, keepdims=True))
        coef_exp[j] = _expand(dskip_ref[...] + cbd * rm[j][0], eb_ref)
    y_off = _state_sweep(st_ref, order, xs, bm, cm_bf, fac)

    causal = _tri(False)
    left = lane < SSD_HEAD_DIM
    zero_bf = jnp.zeros((L, LANES), BF16)
    acum2 = {j: rm[j][1] * _LOG2E for j in order}
    y_loc = {j: [] for j in order}
    for k in range(nh // 2):
        g = (2 * k) // hpg
        for j in order:
            lhs = []
            for h in (2 * k, 2 * k + 1):
                col_f = jnp.broadcast_to(acum2[j][:, h:h + 1], (L, L))
                col_b = jnp.broadcast_to(acum2[j][:, nh + h:nh + h + 1], (L, L))
                e = jnp.where(causal, col_f - hm[j][0][h:h + 1, :], col_b - hm[j][1][h:h + 1, :])
                lhs.append((cb[j][g] * jnp.exp2(e)).astype(BF16))
            xs_pair = xs_bf[j][:, k * LANES:(k + 1) * LANES]
            rhs = jnp.concatenate([jnp.where(left, xs_pair, zero_bf), jnp.where(left, zero_bf, xs_pair)], axis=0)
            y_loc[j].append(_dot(jnp.concatenate(lhs, axis=1), rhs))
    for j in order:
        y = jnp.concatenate(y_loc[j], axis=1) + coef_exp[j] * xs[j] + yob_ref[_rows(j), :].astype(F32) + y_off[j]
        y_ref[_rows(j), :] = y.astype(BF16)


def _ssd(proj, dt, dtT, p, e_mats, lay):
    t = proj.shape[0]
    ns = _SSD_TILE_CHUNKS
    assert lay.per_seq_first % ns == 0 and lay.per_seq_second % ns == 0 and lay.n_first % ns == 0
    nb = lay.n_blocks // ns
    rows = ns * BLK
    xs_blk, bc_blk = COL_XS // SSD_INNER, COL_BC // SSD_BC_WIDTH
    state = pltpu.VMEM((SSD_GROUPS, SSD_STATE, SSD_INNER // SSD_GROUPS), F32)

    def rev(c):
        return nb - 1 - c

    yob = pl.pallas_call(
        functools.partial(_ssd_bwd_body, lay=lay),
        grid=(nb,),
        in_specs=[
            pl.BlockSpec((rows, SSD_INNER), lambda c: (rev(c), xs_blk)),
            pl.BlockSpec((rows, SSD_BC_WIDTH), lambda c: (rev(c), bc_blk)),
            pl.BlockSpec((rows, LANES), lambda c: (rev(c), 0)),
            _const_spec((1, LANES)), _const_spec((1, LANES)),
            _const_spec((LANES, SSD_INNER)),
        ],
        out_specs=pl.BlockSpec((rows, SSD_INNER), lambda c: (rev(c), 0)),
        out_shape=jax.ShapeDtypeStruct((t, SSD_INNER), BF16),
        scratch_shapes=[state],
        compiler_params=_params(("arbitrary",)),
        name="ssd_bwd",
    )(proj, proj, dt, p["dtb"], p["alog"], e_mats[1])

    return pl.pallas_call(
        functools.partial(_ssd_fwd_body, lay=lay),
        grid=(nb,),
        in_specs=[
            pl.BlockSpec((rows, SSD_INNER), lambda c: (c, xs_blk)),
            pl.BlockSpec((rows, SSD_BC_WIDTH), lambda c: (c, bc_blk)),
            pl.BlockSpec((rows, LANES), lambda c: (c, 0)),
            pl.BlockSpec((DT_REP, rows), lambda c: (0, c)),
            pl.BlockSpec((rows, SSD_INNER), lambda c: (c, 0)),
            _const_spec((1, LANES)), _const_spec((1, LANES)),
            _const_spec((DT_REP, 1)), _const_spec((DT_REP, 1)),
            _const_spec((1, LANES)),
            _const_spec((LANES, SSD_INNER)), _const_spec((LANES, SSD_INNER)),
        ],
        out_specs=pl.BlockSpec((rows, SSD_INNER), lambda c: (c, 0)),
        out_shape=jax.ShapeDtypeStruct((t, SSD_INNER), BF16),
        scratch_shapes=[state],
        compiler_params=_params(("arbitrary",)),
        name="ssd_fwd",
    )(proj, proj, dt, dtT, yob, p["dtb"], p["alog"], p["dtbT"], p["alogT"], p["dskip"], e_mats[0], e_mats[1])


def _merge_body(x_ref, ya_ref, yb_ref, yc_ref, z_ref, g0_ref, g1_ref, g2_ref, ng_ref,
                wa_ref, wb_ref, wc_ref, wo_ref, h_ref):
    y = yc_ref[...].astype(F32) * z_ref[...].astype(F32)
    gw = SSD_INNER // SSD_GROUPS
    yn = jnp.concatenate([_rms(y[:, g * gw:(g + 1) * gw], ng_ref[:, g * gw:(g + 1) * gw])
                          for g in range(SSD_GROUPS)], axis=1).astype(BF16)
    m = g0_ref[...].astype(F32) * _dot(ya_ref[...], wa_ref[...])
    m = m + g1_ref[...].astype(F32) * _dot(yb_ref[...], wb_ref[...])
    m = m + g2_ref[...].astype(F32) * _dot(yn, wc_ref[...])
    h_ref[...] = x_ref[...] + _dot(m.astype(BF16), wo_ref[...])


def _merge(x, ya, yb, yc, proj, ng, wa, wb, wc, wo, tm=512):
    t = x.shape[0]
    g_blk = COL_GATE // D_MODEL

    def row(width):
        return pl.BlockSpec((tm, width), lambda i: (i, 0))

    return pl.pallas_call(
        _merge_body,
        grid=(t // tm,),
        in_specs=[
            row(D_MODEL), row(ATTN_Q_WIDTH), row(GMLP_WIDTH), row(SSD_INNER),
            pl.BlockSpec((tm, SSD_INNER), lambda i: (i, COL_Z // SSD_INNER)),
            pl.BlockSpec((tm, D_MODEL), lambda i: (i, g_blk)),
            pl.BlockSpec((tm, D_MODEL), lambda i: (i, g_blk + 1)),
            pl.BlockSpec((tm, D_MODEL), lambda i: (i, g_blk + 2)),
            _const_spec((1, SSD_INNER)),
            _const_spec((ATTN_Q_WIDTH, D_MODEL)), _const_spec((GMLP_WIDTH, D_MODEL)),
            _const_spec((SSD_INNER, D_MODEL)), _const_spec((D_MODEL, D_MODEL)),
        ],
        out_specs=row(D_MODEL),
        out_shape=jax.ShapeDtypeStruct((t, D_MODEL), F32),
        compiler_params=_params(("parallel",)),
        name="merge",
    )(x, ya, yb, yc, proj, proj, proj, proj, ng, wa, wb, wc, wo)


_FF_CHUNK = 1024


def _ffn_body(h_ref, g_ref, w1_ref, w2_ref, o_ref):
    h = h_ref[...]
    hn = _rms(h, g_ref[...]).astype(BF16)
    acc = h
    for c0 in range(0, FF_DIM, _FF_CHUNK):
        a = jnp.maximum(_dot(hn, w1_ref[:, c0:c0 + _FF_CHUNK]), 0.0)
        acc = acc + _dot((a * a).astype(BF16), w2_ref[c0:c0 + _FF_CHUNK, :])
    o_ref[...] = acc


def _ffn(h, g, w1, w2, tm=512):
    t = h.shape[0]
    return pl.pallas_call(
        _ffn_body,
        grid=(t // tm,),
        in_specs=[
            pl.BlockSpec((tm, D_MODEL), lambda i: (i, 0)),
            _const_spec((1, D_MODEL)),
            _const_spec((D_MODEL, FF_DIM)),
            _const_spec((FF_DIM, D_MODEL)),
        ],
        out_specs=pl.BlockSpec((tm, D_MODEL), lambda i: (i, 0)),
        out_shape=jax.ShapeDtypeStruct((t, D_MODEL), F32),
        compiler_params=_params(("parallel",)),
        name="ffn",
    )(h, g, w1, w2)


def _t5_bucket(rel):
    nb = NUM_BUCKETS // 2
    max_exact = nb // 2
    ret = jnp.where(rel > 0, nb, 0)
    n = jnp.abs(rel)
    n_safe = jnp.maximum(n, 1).astype(F32)
    large = max_exact + (jnp.log(n_safe / max_exact) / math.log(MAX_DISTANCE / max_exact)
                         * (nb - max_exact)).astype(jnp.int32)
    large = jnp.minimum(large, nb - 1)
    return ret + jnp.where(n < max_exact, n, large)


def _bias_table(rel_bias):
    qi = jnp.arange(BLK)[:, None]
    ki = jnp.arange(3 * BLK)[None, :]
    rel = ki - BLK - qi
    onehot = (_t5_bucket(rel)[..., None] == jnp.arange(NUM_BUCKETS)).astype(F32)
    bias = jnp.einsum("qkb,bh->hkq", onehot, rel_bias.astype(F32), precision=lax.Precision.HIGHEST) * _LOG2E
    return jnp.where((jnp.abs(rel) <= WINDOW).T[None], bias, NEG_INF)


def _expand_matrix(d):
    k = np.arange(LANES)[:, None]
    col = np.arange(SSD_INNER)[None, :]
    hit = ((k % DT_REP) == d * SSD_HEADS + col // SSD_HEAD_DIM)
    return jnp.asarray(hit, dtype=BF16)


def _prep_layer_params(w_in, norm_mix_g, q_norm_g, k_norm_g, attn_sink, gmlp_ln_g, gmlp_ln_b, w_spatial,
                       b_spatial, conv_w, conv_b, dt_bias, a_log, d_skip, ssd_norm_g, w_up_attn, w_up_gmlp,
                       w_up_ssd, w_out, norm_ff_g, w_ff1, w_ff2):
    depth = w_in.shape[0]

    def cols(a, b):
        return w_in[:, :, a:b]

    w_main = jnp.concatenate([
        cols(_R_Z, _R_XS), cols(_R_XS, _R_B), cols(_R_GATE, _R_END), cols(_R_Q, _R_K),
        cols(_R_GU, _R_GV), cols(_R_GV, _R_Z), cols(_R_B, _R_DT), cols(_R_K, _R_GU)], axis=-1).astype(BF16)
    w_dt = cols(_R_DT, _R_GATE)
    rep = LANES // DT_REP
    gw = GMLP_WIDTH // GMLP_GROUPS

    def dt_lanes(v):
        return jnp.tile(v.reshape(depth, 1, DT_REP), (1, 1, rep))

    return dict(
        norm_mix_g=norm_mix_g[:, None, :],
        w_main=w_main,
        w_dt=jnp.tile(w_dt, (1, 1, rep)).astype(BF16),
        w_dtT=jnp.swapaxes(w_dt, 1, 2).astype(BF16),
        kscale=jnp.tile(q_norm_g * k_norm_g * (HEAD_DIM ** -0.5 * _LOG2E), (1, ATTN_KV_HEADS))[:, None, :],
        sink=attn_sink[:, None, :] * _LOG2E,
        ln_g=gmlp_ln_g[:, None, :], ln_b=gmlp_ln_b[:, None, :],
        w_s=w_spatial.astype(BF16),
        b_s=jnp.repeat(jnp.swapaxes(b_spatial, 1, 2), gw, axis=-1),
        conv_w=conv_w, conv_b=conv_b[:, None, :],
        dtb=dt_lanes(dt_bias), alog=dt_lanes(a_log),
        dtbT=dt_bias.reshape(depth, DT_REP, 1), alogT=a_log.reshape(depth, DT_REP, 1),
        dskip=dt_lanes(jnp.concatenate([d_skip, d_skip], axis=-1)),
        ssd_norm_g=ssd_norm_g[:, None, :],
        w_a=w_up_attn.astype(BF16), w_b=w_up_gmlp.astype(BF16), w_c=w_up_ssd.astype(BF16),
        w_o=w_out.astype(BF16),
        norm_ff_g=norm_ff_g[:, None, :],
        w1=w_ff1.astype(BF16), w2=w_ff2.astype(BF16),
    )


def _layer(x, p, bias, e_mats, lay):
    proj, dt, dtT = _inproj(x, p, lay)
    ya = _attention(proj, bias, p["sink"], lay)
    yb = _gmlp(proj, p["w_s"], p["b_s"])
    yc = _ssd(proj, dt, dtT, p, e_mats, lay)
    h = _merge(x, ya, yb, yc, proj, p["ssd_norm_g"], p["w_a"], p["w_b"], p["w_c"], p["w_o"])
    return _ffn(h, p["norm_ff_g"], p["w1"], p["w2"])


def _encoder(x_first, x_second, rel_bias, layer_params):
    b1, s1, _ = x_first.shape
    b2, s2, _ = x_second.shape
    t1, t2 = b1 * s1, b2 * s2
    lay = Layout(n_blocks=(t1 + t2) // BLK, n_first=t1 // BLK, per_seq_first=s1 // BLK,
                 per_seq_second=s2 // BLK)
    x = jnp.concatenate([x_first.reshape(t1, D_MODEL), x_second.reshape(t2, D_MODEL)], axis=0)
    bias = _bias_table(rel_bias)
    e_mats = (_expand_matrix(0), _expand_matrix(1))
    params = _prep_layer_params(**layer_params)

    def step(carry, p):
        return _layer(carry, p, bias, e_mats, lay), None

    x, _ = lax.scan(step, x, params)
    return x[:t1].reshape(b1, s1, D_MODEL), x[t1:].reshape(b2, s2, D_MODEL)


def kernel(x_prompt, x_sample, rel_bias, norm_mix_g, w_in, q_norm_g, k_norm_g, attn_sink, gmlp_ln_g, gmlp_ln_b,
           w_spatial, b_spatial, conv_w, conv_b, dt_bias, a_log, d_skip, ssd_norm_g, w_up_attn, w_up_gmlp,
           w_up_ssd, w_out, norm_ff_g, w_ff1, w_ff2):
    layer_params = dict(
        w_in=w_in, norm_mix_g=norm_mix_g, q_norm_g=q_norm_g, k_norm_g=k_norm_g, attn_sink=attn_sink,
        gmlp_ln_g=gmlp_ln_g, gmlp_ln_b=gmlp_ln_b, w_spatial=w_spatial, b_spatial=b_spatial, conv_w=conv_w,
        conv_b=conv_b, dt_bias=dt_bias, a_log=a_log, d_skip=d_skip, ssd_norm_g=ssd_norm_g,
        w_up_attn=w_up_attn, w_up_gmlp=w_up_gmlp, w_up_ssd=w_up_ssd, w_out=w_out, norm_ff_g=norm_ff_g,
        w_ff1=w_ff1, w_ff2=w_ff2)
    return _encoder(x_prompt, x_sample, rel_bias, layer_params)
```

```python
import functools
import math
from typing import NamedTuple

import numpy as np
import jax
import jax.numpy as jnp
from jax import lax
from jax.experimental import pallas as pl
from jax.experimental.pallas import tpu as pltpu

F32 = jnp.float32
BF16 = jnp.bfloat16

D_MODEL = 1024
ATTN_HEADS = 8
ATTN_KV_HEADS = 2
HEAD_DIM = 64
ATTN_Q_WIDTH = ATTN_HEADS * HEAD_DIM
ATTN_KV_WIDTH = ATTN_KV_HEADS * HEAD_DIM
WINDOW = 128
NUM_BUCKETS = 32
MAX_DISTANCE = 128
GMLP_WIDTH = 512
GMLP_GROUPS = 4
SSD_INNER = 1024
SSD_HEAD_DIM = 64
SSD_HEADS = SSD_INNER // SSD_HEAD_DIM
SSD_GROUPS = 2
SSD_STATE = 128
N_DIR = 2
FF_DIM = 4 * D_MODEL
EPS = 1e-6
NEG_INF = -1e30

BLK = 128
LANES = 128
BF16_SUBLANES = 16
VMEM_LIMIT = 56 * 1024 * 1024
_LOG2E = math.log2(math.e)

COL_Z = 0
COL_XS = 1024
COL_GATE = 2048
COL_Q = 5120
COL_GU = 5632
COL_GV = 6144
COL_BC = 6656
COL_KV = 7168
P_WIDTH = 7424
DT_REP = 32
SSD_BC_WIDTH = 2 * SSD_GROUPS * SSD_STATE

_R_Q, _R_K, _R_V, _R_GU, _R_GV, _R_Z, _R_XS, _R_B, _R_C, _R_DT, _R_GATE, _R_END = (
    0, 512, 640, 768, 1280, 1792, 2816, 3840, 4096, 4352, 4384, 7456)


class Layout(NamedTuple):
    n_blocks: int
    n_first: int
    per_seq_first: int
    per_seq_second: int


def _seq_pos(blk, lay):
    in_first = blk < lay.n_first
    pos = jnp.where(in_first, lax.rem(blk, lay.per_seq_first),
                    lax.rem(blk - lay.n_first, lay.per_seq_second))
    last = jnp.where(in_first, lay.per_seq_first - 1, lay.per_seq_second - 1)
    return pos == 0, pos == last


def _sigmoid(x):
    return 1.0 / (1.0 + jnp.exp(-x))


def _silu(x):
    return x * _sigmoid(x)


def _gelu(x):
    return x * (0.5 * (1.0 + jnp.tanh(math.sqrt(2.0 / math.pi) * (x + 0.044715 * (x * x * x)))))


def _softplus(x):
    return jnp.maximum(x, 0.0) + jnp.log1p(jnp.exp(-jnp.abs(x)))


def _rms(x, g):
    return x * lax.rsqrt(jnp.mean(x * x, axis=-1, keepdims=True) + EPS) * g


def _dot(a, b):
    return jnp.dot(a, b, preferred_element_type=F32)


def _dot_nt(a, b):
    return lax.dot_general(a, b, (((1,), (1,)), ((), ())), preferred_element_type=F32)


def _const_spec(shape):
    nd = len(shape)
    return pl.BlockSpec(shape, lambda *_: (0,) * nd)


def _params(semantics):
    return pltpu.CompilerParams(dimension_semantics=semantics, vmem_limit_bytes=VMEM_LIMIT)


_PROJ_TILE = 512
_PROJ_LOOKAHEAD = 1
_X_HALO = 8


def _split2(x):
    hi = x.astype(BF16)
    return hi, (x - hi.astype(F32)).astype(BF16)


def _inproj_body(xa_ref, xb_ref, xa_prev_ref, xb_prev_ref, xa_next_ref, xb_next_ref, g_ref, w_ref, wdt_ref, wdtT_ref,
                 cw_ref, cb_ref, lng_ref, lnb_ref, kscale_ref, qsum_ref, ksum_ref, o_ref, dt_ref, dtT_ref,
                 xn_ref, xh_ref, *, lay):
    tm = _PROJ_TILE
    nblk = tm // BLK
    i = pl.program_id(0)
    keep_prev = jnp.where(_seq_pos(i * nblk, lay)[0], 0.0, 1.0)
    keep_next = jnp.where(_seq_pos(i * nblk + nblk - 1, lay)[1], 0.0, 1.0)
    in_first = i < lay.n_first // nblk

    def normalise(x_ref, prev_ref, next_ref):
        xn_ref[...] = _rms(x_ref[...], g_ref[...]).astype(BF16)
        xh_ref[...] = _rms(jnp.concatenate([prev_ref[...], next_ref[...]], axis=0), g_ref[...]).astype(BF16)

    pl.when(in_first)(lambda: normalise(xa_ref, xa_prev_ref, xa_next_ref))
    pl.when(jnp.logical_not(in_first))(lambda: normalise(xb_ref, xb_prev_ref, xb_next_ref))
    xn = xn_ref[...]
    xh = xh_ref[...]

    def put(c0, val):
        o_ref[:, c0:c0 + val.shape[1]] = val.astype(BF16)

    def conv_silu(k0):
        def post(c0, y):
            width = y.shape[1]
            yh = _dot(xh, w_ref[:, c0:c0 + width])
            halo_p = yh[_X_HALO - 1:_X_HALO] * keep_prev
            halo_n = yh[_X_HALO:_X_HALO + 1] * keep_next
            up = pltpu.roll(y, 1, axis=0)
            dn = pltpu.roll(y, tm - 1, axis=0)
            r8 = lax.broadcasted_iota(jnp.int32, (8, width), 0)
            y_m1 = jnp.concatenate([jnp.where(r8 == 0, halo_p, up[:8]), up[8:]], axis=0)
            y_p1 = jnp.concatenate([dn[:tm - 8], jnp.where(r8 == 7, halo_n, dn[tm - 8:])], axis=0)
            w = cw_ref[:, k0:k0 + width]
            put(c0, _silu(cb_ref[:, k0:k0 + width] + y_m1 * w[0:1] + y * w[1:2] + y_p1 * w[2:3]))
        return post

    def q_norm(c0, q):
        qss = _dot(jnp.concatenate(_split2(q * q), axis=1), qsum_ref[...])
        put(c0, q * lax.rsqrt(qss * (1.0 / HEAD_DIM) + EPS))

    def gelu_ln(c0, y):
        vf = _gelu(y)
        vc = vf - jnp.mean(vf, axis=-1, keepdims=True)
        var = jnp.mean(vc * vc, axis=-1, keepdims=True)
        put(c0, vc * lax.rsqrt(var + EPS) * lng_ref[...] + lnb_ref[...])

    def kv_norm(c0, kv):
        kf = kv[:, :ATTN_KV_WIDTH]
        kss = _dot(jnp.concatenate(_split2(kf * kf), axis=1), ksum_ref[...])
        put(c0, kf * lax.rsqrt(kss * (1.0 / HEAD_DIM) + EPS) * kscale_ref[...])
        put(c0 + ATTN_KV_WIDTH, kv[:, ATTN_KV_WIDTH:])

    def gate(k):
        return (COL_GATE + k * D_MODEL, D_MODEL, lambda c0, y: put(c0, _sigmoid(y)))

    items = [
        (COL_XS, SSD_INNER, conv_silu(0)),
        (COL_Z, SSD_INNER, lambda c0, y: put(c0, _silu(y))),
        (COL_BC, SSD_BC_WIDTH, conv_silu(SSD_INNER)),
        gate(0),
        (COL_GV, GMLP_WIDTH, gelu_ln),
        gate(1),
        (COL_GU, GMLP_WIDTH, lambda c0, y: put(c0, _gelu(y))),
        gate(2),
        (COL_Q, ATTN_Q_WIDTH, q_norm),
        (COL_KV, 2 * ATTN_KV_WIDTH, kv_norm),
    ]

    def proj(k):
        c0, width, _ = items[k]
        return _dot(xn, w_ref[:, c0:c0 + width])

    pending = [proj(k) for k in range(_PROJ_LOOKAHEAD)]
    for k, (c0, _, post) in enumerate(items):
        y = pending.pop(0)
        if k + _PROJ_LOOKAHEAD < len(items):
            pending.append(proj(k + _PROJ_LOOKAHEAD))
        post(c0, y)
    dt_ref[...] = _dot(xn, wdt_ref[...])
    dtT_ref[...] = _dot_nt(wdtT_ref[...], xn)


def _head_sum_matrix(width):
    k = np.arange(2 * width)[:, None]
    n = np.arange(width)[None, :]
    return jnp.asarray((k % width) // HEAD_DIM == n // HEAD_DIM, dtype=BF16)


def _two_stream_specs(tm, n_a, n_b):
    return (pl.BlockSpec((tm, D_MODEL), lambda i: (jnp.minimum(i, n_a - 1), 0)),
            pl.BlockSpec((tm, D_MODEL), lambda i: (jnp.clip(i - n_a, 0, n_b - 1), 0)))


def _inproj(xa, xb, p, lay):
    tm = _PROJ_TILE
    n_a, n_b = xa.shape[0] // tm, xb.shape[0] // tm
    t = xa.shape[0] + xb.shape[0]
    assert lay.per_seq_first % (tm // BLK) == 0 and lay.per_seq_second % (tm // BLK) == 0
    hal = tm // _X_HALO
    conv_w = SSD_INNER + SSD_BC_WIDTH

    def halo(tile0, n_tiles, shift):
        last = n_tiles * hal - 1
        return pl.BlockSpec((_X_HALO, D_MODEL),
                            lambda i: (jnp.clip((i - tile0 + shift) * hal - 1 + shift, 0, last), 0))

    return pl.pallas_call(
        functools.partial(_inproj_body, lay=lay),
        grid=(n_a + n_b,),
        in_specs=[
            *_two_stream_specs(tm, n_a, n_b),
            halo(0, n_a, 0), halo(n_a, n_b, 0), halo(0, n_a, 1), halo(n_a, n_b, 1),
            _const_spec((1, D_MODEL)),
            _const_spec((D_MODEL, P_WIDTH)),
            _const_spec((D_MODEL, LANES)),
            _const_spec((DT_REP, D_MODEL)),
            _const_spec((3, conv_w)), _const_spec((1, conv_w)),
            _const_spec((1, GMLP_WIDTH)), _const_spec((1, GMLP_WIDTH)),
            _const_spec((1, ATTN_KV_WIDTH)),
            _const_spec((2 * ATTN_Q_WIDTH, ATTN_Q_WIDTH)),
            _const_spec((2 * ATTN_KV_WIDTH, ATTN_KV_WIDTH)),
        ],
        out_specs=[
            pl.BlockSpec((tm, P_WIDTH), lambda i: (i, 0)),
            pl.BlockSpec((tm, LANES), lambda i: (i, 0)),
            pl.BlockSpec((DT_REP, tm), lambda i: (0, i)),
        ],
        out_shape=[
            jax.ShapeDtypeStruct((t, P_WIDTH), BF16),
            jax.ShapeDtypeStruct((t, LANES), F32),
            jax.ShapeDtypeStruct((DT_REP, t), F32),
        ],
        scratch_shapes=[pltpu.VMEM((tm, D_MODEL), BF16), pltpu.VMEM((2 * _X_HALO, D_MODEL), BF16)],
        compiler_params=_params(("parallel",)),
        name="inproj",
    )(xa, xb, xa, xb, xa, xb, p["norm_mix_g"], p["w_main"], p["w_dt"], p["w_dtT"], p["conv_w"], p["conv_b"], p["ln_g"], p["ln_b"],
      p["kscale"], _head_sum_matrix(ATTN_Q_WIDTH), _head_sum_matrix(ATTN_KV_WIDTH))


_ATTN_TILE_BLOCKS = 4
_ATTN_LOOKAHEAD = 4


def _attn_body(q_ref, kvp_ref, kvm_ref, kvn_ref, bias_ref, sink_ref, o_ref, *, lay):
    i = pl.program_id(0)
    nt = _ATTN_TILE_BLOCKS
    grp = ATTN_HEADS // ATTN_KV_HEADS
    kv = jnp.concatenate([kvp_ref[...], kvm_ref[...], kvn_ref[...]], axis=0)
    kn = [kv[:, g * HEAD_DIM:(g + 1) * HEAD_DIM] for g in range(ATTN_KV_HEADS)]
    vT = jnp.transpose(kv[:, ATTN_KV_WIDTH:].astype(F32)).astype(BF16)
    neg_prev = jnp.where(_seq_pos(i * nt, lay)[0], NEG_INF, 0.0)
    neg_next = jnp.where(_seq_pos(i * nt + nt - 1, lay)[1], NEG_INF, 0.0)

    qbs = [q_ref[jb * BLK:(jb + 1) * BLK, :] for jb in range(nt)]

    def scores(jb, h):
        s = _dot_nt(kn[h // grp][jb * BLK:(jb + 3) * BLK], qbs[jb][:, h * HEAD_DIM:(h + 1) * HEAD_DIM])
        return s + bias_ref[h]

    items = [(jb, h) for jb in range(nt) for h in range(ATTN_HEADS)]
    pending = [scores(*it) for it in items[:_ATTN_LOOKAHEAD]]
    outs = []
    for idx, (jb, h) in enumerate(items):
        s = pending.pop(0)
        if idx + _ATTN_LOOKAHEAD < len(items):
            pending.append(scores(*items[idx + _ATTN_LOOKAHEAD]))
        g = h // grp
        if jb == 0:
            s = jnp.concatenate([s[:BLK] + neg_prev, s[BLK:]], axis=0)
        if jb == nt - 1:
            s = jnp.concatenate([s[:2 * BLK], s[2 * BLK:] + neg_next], axis=0)
        snk = sink_ref[0:1, h:h + 1]
        m = jnp.maximum(jnp.max(s, axis=0, keepdims=True), snk)
        p = jnp.exp2(s - m)
        denom = jnp.sum(p, axis=0, keepdims=True) + jnp.exp2(snk - m)
        o = _dot(vT[g * HEAD_DIM:(g + 1) * HEAD_DIM, jb * BLK:(jb + 3) * BLK], p.astype(BF16))
        outs.append(o * (1.0 / denom))
        if h == ATTN_HEADS - 1:
            o_ref[jb * BLK:(jb + 1) * BLK, :] = jnp.transpose(jnp.concatenate(outs, axis=0)).astype(BF16)
            outs = []


def _attention(proj, bias, sink, lay):
    t = proj.shape[0]
    nt = _ATTN_TILE_BLOCKS
    rows = nt * BLK
    q_blk = COL_Q // ATTN_Q_WIDTH
    kv_w = 2 * ATTN_KV_WIDTH
    kv_blk = COL_KV // kv_w
    last_blk = lay.n_blocks - 1
    assert lay.per_seq_first % nt == 0 and lay.per_seq_second % nt == 0 and lay.n_first % nt == 0
    return pl.pallas_call(
        functools.partial(_attn_body, lay=lay),
        grid=(t // rows,),
        in_specs=[
            pl.BlockSpec((rows, ATTN_Q_WIDTH), lambda i: (i, q_blk)),
            pl.BlockSpec((BLK, kv_w), lambda i: (jnp.maximum(i * nt - 1, 0), kv_blk)),
            pl.BlockSpec((rows, kv_w), lambda i: (i, kv_blk)),
            pl.BlockSpec((BLK, kv_w), lambda i: (jnp.minimum(i * nt + nt, last_blk), kv_blk)),
            _const_spec((ATTN_HEADS, 3 * BLK, BLK)),
            _const_spec((1, ATTN_HEADS)),
        ],
        out_specs=pl.BlockSpec((rows, ATTN_Q_WIDTH), lambda i: (i, 0)),
        out_shape=jax.ShapeDtypeStruct((t, ATTN_Q_WIDTH), BF16),
        compiler_params=_params(("parallel",)),
        name="attention",
    )(proj, proj, proj, proj, bias, sink)


_SPLIT_TERMS = 4


def _lane_split(x):
    rep = lax.broadcasted_iota(jnp.int32, x.shape, 1) // DT_REP
    out = jnp.zeros(x.shape, BF16)
    rem = x
    for r in range(_SPLIT_TERMS):
        term = rem.astype(BF16)
        out = jnp.where(rep == r, term, out)
        rem = rem - term.astype(F32)
    return out


def _sublane_split(x):
    terms, rem = [], x
    for _ in range(_SPLIT_TERMS):
        term = rem.astype(BF16)
        terms.append(term)
        rem = rem - term.astype(F32)
    return jnp.concatenate(terms, axis=0)


def _expand(x, e_ref):
    return _dot(_lane_split(x), e_ref[...])


def _tri(upper):
    r = lax.broadcasted_iota(jnp.int32, (BLK, BLK), 0)
    c = lax.broadcasted_iota(jnp.int32, (BLK, BLK), 1)
    return (c >= r) if upper else (c <= r)


def _ones_where(mask):
    return jnp.where(mask, 1.0, 0.0).astype(BF16)


_SSD_TILE_CHUNKS = 4


def _rows(j):
    return slice(j * BLK, (j + 1) * BLK)


def _dt_row_major(dt_raw, dtb, alog):
    L = BLK
    dt = _softplus(dt_raw + dtb)
    sp = _lane_split(dt * (-jnp.exp(alog)))
    lane = lax.broadcasted_iota(jnp.int32, (L, LANES), 1)
    is_fwd = (lane & (DT_REP - 1)) < SSD_HEADS
    part = jnp.where(is_fwd, _dot(_ones_where(_tri(False)), sp), _dot(_ones_where(_tri(True)), sp))
    acum = part
    for r in range(1, _SPLIT_TERMS):
        acum = acum + pltpu.roll(part, r * DT_REP, axis=1)
    a_end = jnp.where(is_fwd[0:1], acum[L - 1:L], acum[0:1])
    return dt, acum, a_end


def _dt_head_major(dtT_raw, dtbT, alogT):
    nh = SSD_HEADS
    dtT = _softplus(dtT_raw + dtbT)
    spT = _sublane_split(dtT * (-jnp.exp(alogT)))
    pf = _dot(spT, _ones_where(_tri(True)))
    pb = _dot(spT, _ones_where(_tri(False)))
    acumT_f = pf[0:nh]
    acumT_b = pb[nh:2 * nh]
    for r in range(1, _SPLIT_TERMS):
        acumT_f = acumT_f + pf[r * DT_REP:r * DT_REP + nh]
        acumT_b = acumT_b + pb[r * DT_REP + nh:(r + 1) * DT_REP]
    log2_dtT = jnp.log(dtT) * _LOG2E
    return acumT_f * _LOG2E - log2_dtT[0:nh], acumT_b * _LOG2E - log2_dtT[nh:2 * nh]


def _state_factors(dt, acum, a_end, e_ref):
    ea_exp = _expand(jnp.exp(acum), e_ref)
    w_exp = _expand(dt * jnp.exp(a_end - acum), e_ref)
    cd_exp = _expand(jnp.broadcast_to(jnp.exp(a_end), (8, LANES)), e_ref)[0:1]
    return ea_exp, w_exp, cd_exp


def _state_sweep(st_ref, order, xs, bm, cm_bf, fac):
    gw = SSD_INNER // SSD_GROUPS
    upd = {}
    for j in order:
        wx = (xs[j] * fac[j][1]).astype(BF16)
        upd[j] = [_dot(jnp.transpose(bm[j][g]).astype(BF16), wx[:, g * gw:(g + 1) * gw])
                  for g in range(SSD_GROUPS)]
    sg = [st_ref[g] for g in range(SSD_GROUPS)]
    out = {}
    for j in order:
        y_off = [_dot(cm_bf[j][g], sg[g].astype(BF16)) for g in range(SSD_GROUPS)]
        neg_cd = -fac[j][2]
        sg = [upd[j][g] - sg[g] * neg_cd[:, g * gw:(g + 1) * gw] for g in range(SSD_GROUPS)]
        out[j] = jnp.concatenate(y_off, axis=1) * fac[j][0]
    for g in range(SSD_GROUPS):
        st_ref[g] = sg[g]
    return out


def _split_bc(bc):
    n = SSD_STATE
    return ([bc[:, g * n:(g + 1) * n] for g in range(SSD_GROUPS)],
            [bc[:, (SSD_GROUPS + g) * n:(SSD_GROUPS + g + 1) * n] for g in range(SSD_GROUPS)])


def _ssd_bwd_body(xs_ref, bc_ref, dt_ref, dtb_ref, alog_ref, eb_ref, yo_ref, st_ref, *, lay):
    ns = _SSD_TILE_CHUNKS
    blk0 = (lay.n_blocks // ns - 1 - pl.program_id(0)) * ns
    last = _seq_pos(blk0 + ns - 1, lay)[1]

    @pl.when(last)
    def _():
        st_ref[...] = jnp.zeros(st_ref.shape, F32)

    order = list(reversed(range(ns)))
    rm = {j: _dt_row_major(dt_ref[_rows(j), :], dtb_ref[...], alog_ref[...]) for j in order}
    fac = {j: _state_factors(*rm[j], eb_ref) for j in order}
    xs = {j: xs_ref[_rows(j), :].astype(F32) for j in order}
    bm = {j: _split_bc(bc_ref[_rows(j), :].astype(F32))[0] for j in order}
    cm_bf = {j: _split_bc(bc_ref[_rows(j), :])[1] for j in order}
    y_off = _state_sweep(st_ref, order, xs, bm, cm_bf, fac)
    for j in order:
        yo_ref[_rows(j), :] = y_off[j].astype(BF16)


def _ssd_fwd_body(xsc_ref, bcc_ref, dt_ref, dtT_ref, yob_ref, dtb_ref, alog_ref, dtbT_ref, alogT_ref,
                  dskip_ref, ef_ref, eb_ref, y_ref, st_ref, *, lay):
    ns = _SSD_TILE_CHUNKS
    first = _seq_pos(pl.program_id(0) * ns, lay)[0]

    @pl.when(first)
    def _():
        st_ref[...] = jnp.zeros(st_ref.shape, F32)

    L, nh = BLK, SSD_HEADS
    hpg = nh // SSD_GROUPS
    order = list(range(ns))
    lane = lax.broadcasted_iota(jnp.int32, (L, LANES), 1)
    xs_bf = {j: xsc_ref[_rows(j), :] for j in order}
    bc_bf = {j: bcc_ref[_rows(j), :] for j in order}
    bm_bf = {j: _split_bc(bc_bf[j])[0] for j in order}
    cm_bf = {j: _split_bc(bc_bf[j])[1] for j in order}
    cb = {j: [_dot_nt(cm_bf[j][g], bm_bf[j][g]) for g in range(SSD_GROUPS)] for j in order}
    rm = {j: _dt_row_major(dt_ref[_rows(j), :], dtb_ref[...], alog_ref[...]) for j in order}
    hm = {j: _dt_head_major(dtT_ref[:, _rows(j)], dtbT_ref[...], alogT_ref[...]) for j in order}
    fac = {j: _state_factors(*rm[j], ef_ref) for j in order}
    xs = {j: xs_bf[j].astype(F32) for j in order}
    bm, coef_exp = {}, {}
    for j in order:
        bm[j], cm = _split_bc(bc_bf[j].astype(F32))
        cbd = jnp.where((lane & (nh - 1)) < hpg, jnp.sum(cm[0] * bm[j][0], axis=1, keepdims=True),
                        jnp.sum(cm[1] * bm[j][1], axis=1, keepdims=True))
        coef_exp[j] = _expand(dskip_ref[...] + cbd * rm[j][0], eb_ref)
    y_off = _state_sweep(st_ref, order, xs, bm, cm_bf, fac)

    causal = _tri(False)
    left = lane < SSD_HEAD_DIM
    zero_bf = jnp.zeros((L, LANES), BF16)
    acum2 = {j: rm[j][1] * _LOG2E for j in order}
    y_loc = {j: [] for j in order}
    for k in range(nh // 2):
        g = (2 * k) // hpg
        for j in order:
            lhs = []
            for h in (2 * k, 2 * k + 1):
                col_f = jnp.broadcast_to(acum2[j][:, h:h + 1], (L, L))
                col_b = jnp.broadcast_to(acum2[j][:, nh + h:nh + h + 1], (L, L))
                e = jnp.where(causal, col_f - hm[j][0][h:h + 1, :], col_b - hm[j][1][h:h + 1, :])
                lhs.append((cb[j][g] * jnp.exp2(e)).astype(BF16))
            xs_pair = xs_bf[j][:, k * LANES:(k + 1) * LANES]
            rhs = jnp.concatenate([jnp.where(left, xs_pair, zero_bf), jnp.where(left, zero_bf, xs_pair)], axis=0)
            y_loc[j].append(_dot(jnp.concatenate(lhs, axis=1), rhs))
    for j in order:
        y = jnp.concatenate(y_loc[j], axis=1) + coef_exp[j] * xs[j] + yob_ref[_rows(j), :].astype(F32) + y_off[j]
        y_ref[_rows(j), :] = y.astype(BF16)


def _ssd(proj, dt, dtT, p, e_mats, lay):
    t = proj.shape[0]
    ns = _SSD_TILE_CHUNKS
    assert lay.per_seq_first % ns == 0 and lay.per_seq_second % ns == 0 and lay.n_first % ns == 0
    nb = lay.n_blocks // ns
    rows = ns * BLK
    xs_blk, bc_blk = COL_XS // SSD_INNER, COL_BC // SSD_BC_WIDTH
    state = pltpu.VMEM((SSD_GROUPS, SSD_STATE, SSD_INNER // SSD_GROUPS), F32)

    def rev(c):
        return nb - 1 - c

    yob = pl.pallas_call(
        functools.partial(_ssd_bwd_body, lay=lay),
        grid=(nb,),
        in_specs=[
            pl.BlockSpec((rows, SSD_INNER), lambda c: (rev(c), xs_blk)),
            pl.BlockSpec((rows, SSD_BC_WIDTH), lambda c: (rev(c), bc_blk)),
            pl.BlockSpec((rows, LANES), lambda c: (rev(c), 0)),
            _const_spec((1, LANES)), _const_spec((1, LANES)),
            _const_spec((LANES, SSD_INNER)),
        ],
        out_specs=pl.BlockSpec((rows, SSD_INNER), lambda c: (rev(c), 0)),
        out_shape=jax.ShapeDtypeStruct((t, SSD_INNER), BF16),
        scratch_shapes=[state],
        compiler_params=_params(("arbitrary",)),
        name="ssd_bwd",
    )(proj, proj, dt, p["dtb"], p["alog"], e_mats[1])

    return pl.pallas_call(
        functools.partial(_ssd_fwd_body, lay=lay),
        grid=(nb,),
        in_specs=[
            pl.BlockSpec((rows, SSD_INNER), lambda c: (c, xs_blk)),
            pl.BlockSpec((rows, SSD_BC_WIDTH), lambda c: (c, bc_blk)),
            pl.BlockSpec((rows, LANES), lambda c: (c, 0)),
            pl.BlockSpec((DT_REP, rows), lambda c: (0, c)),
            pl.BlockSpec((rows, SSD_INNER), lambda c: (c, 0)),
            _const_spec((1, LANES)), _const_spec((1, LANES)),
            _const_spec((DT_REP, 1)), _const_spec((DT_REP, 1)),
            _const_spec((1, LANES)),
            _const_spec((LANES, SSD_INNER)), _const_spec((LANES, SSD_INNER)),
        ],
        out_specs=pl.BlockSpec((rows, SSD_INNER), lambda c: (c, 0)),
        out_shape=jax.ShapeDtypeStruct((t, SSD_INNER), BF16),
        scratch_shapes=[state],
        compiler_params=_params(("arbitrary",)),
        name="ssd_fwd",
    )(proj, proj, dt, dtT, yob, p["dtb"], p["alog"], p["dtbT"], p["alogT"], p["dskip"], e_mats[0], e_mats[1])


_ROW_TILE = 512


def _merge_body(xa_ref, xb_ref, ya_ref, gu_ref, gv_ref, yc_ref, z_ref, g0_ref, g1_ref, g2_ref, ng_ref, ws_ref, bs_ref,
                wa_ref, wb_ref, wc_ref, wo_ref, ha_ref, hb_ref, *, n_a):
    gw = GMLP_WIDTH // GMLP_GROUPS
    chunks = []
    for c in range(_ROW_TILE // BLK):
        r = slice(c * BLK, (c + 1) * BLK)
        chunks.append(jnp.concatenate(
            [gu_ref[r, g * gw:(g + 1) * gw].astype(F32)
             * (_dot(ws_ref[g], gv_ref[r, g * gw:(g + 1) * gw]) + bs_ref[:, g * gw:(g + 1) * gw])
             for g in range(GMLP_GROUPS)], axis=1))
    yb = jnp.concatenate(chunks, axis=0).astype(BF16)
    y = yc_ref[...].astype(F32) * z_ref[...].astype(F32)
    sw = SSD_INNER // SSD_GROUPS
    yn = jnp.concatenate([_rms(y[:, g * sw:(g + 1) * sw], ng_ref[:, g * sw:(g + 1) * sw])
                          for g in range(SSD_GROUPS)], axis=1).astype(BF16)
    m = g0_ref[...].astype(F32) * _dot(ya_ref[...], wa_ref[...])
    m = m + g1_ref[...].astype(F32) * _dot(yb, wb_ref[...])
    m = m + g2_ref[...].astype(F32) * _dot(yn, wc_ref[...])
    m = m.astype(BF16)
    in_first = pl.program_id(0) < n_a

    @pl.when(in_first)
    def _():
        ha_ref[...] = xa_ref[...] + _dot(m, wo_ref[...])

    @pl.when(jnp.logical_not(in_first))
    def _():
        hb_ref[...] = xb_ref[...] + _dot(m, wo_ref[...])


def _merge(xa, xb, ya, yc, proj, p):
    tm = _ROW_TILE
    n_a, n_b = xa.shape[0] // tm, xb.shape[0] // tm
    g_blk = COL_GATE // D_MODEL

    def row(width, col_blk=0):
        return pl.BlockSpec((tm, width), lambda i: (i, col_blk))

    stream = _two_stream_specs(tm, n_a, n_b)
    return pl.pallas_call(
        functools.partial(_merge_body, n_a=n_a),
        grid=(n_a + n_b,),
        in_specs=[
            *stream, row(ATTN_Q_WIDTH), row(GMLP_WIDTH, COL_GU // GMLP_WIDTH), row(GMLP_WIDTH, COL_GV // GMLP_WIDTH),
            row(SSD_INNER), row(SSD_INNER, COL_Z // SSD_INNER),
            row(D_MODEL, g_blk), row(D_MODEL, g_blk + 1), row(D_MODEL, g_blk + 2),
            _const_spec((1, SSD_INNER)),
            _const_spec((GMLP_GROUPS, BLK, BLK)), _const_spec((BLK, GMLP_WIDTH)),
            _const_spec((ATTN_Q_WIDTH, D_MODEL)), _const_spec((GMLP_WIDTH, D_MODEL)),
            _const_spec((SSD_INNER, D_MODEL)), _const_spec((D_MODEL, D_MODEL)),
        ],
        out_specs=list(stream),
        out_shape=[jax.ShapeDtypeStruct(xa.shape, F32), jax.ShapeDtypeStruct(xb.shape, F32)],
        compiler_params=_params(("arbitrary",)),
        name="merge",
    )(xa, xb, ya, proj, proj, yc, proj, proj, proj, proj, p["ssd_norm_g"], p["w_s"], p["b_s"],
      p["w_a"], p["w_b"], p["w_c"], p["w_o"])


_FF_CHUNK = 1024


def _ffn_body(ha_ref, hb_ref, g_ref, w1_ref, w2_ref, oa_ref, ob_ref, *, n_a):
    def run(h_ref, o_ref):
        h = h_ref[...]
        hn = _rms(h, g_ref[...]).astype(BF16)
        acc = h
        for c0 in range(0, FF_DIM, _FF_CHUNK):
            a = jnp.maximum(_dot(hn, w1_ref[:, c0:c0 + _FF_CHUNK]), 0.0)
            acc = acc + _dot((a * a).astype(BF16), w2_ref[c0:c0 + _FF_CHUNK, :])
        o_ref[...] = acc

    in_first = pl.program_id(0) < n_a
    pl.when(in_first)(lambda: run(ha_ref, oa_ref))
    pl.when(jnp.logical_not(in_first))(lambda: run(hb_ref, ob_ref))


def _ffn(ha, hb, g, w1, w2):
    tm = _ROW_TILE
    n_a, n_b = ha.shape[0] // tm, hb.shape[0] // tm
    stream = _two_stream_specs(tm, n_a, n_b)
    return pl.pallas_call(
        functools.partial(_ffn_body, n_a=n_a),
        grid=(n_a + n_b,),
        in_specs=[*stream, _const_spec((1, D_MODEL)), _const_spec((D_MODEL, FF_DIM)), _const_spec((FF_DIM, D_MODEL))],
        out_specs=list(stream),
        out_shape=[jax.ShapeDtypeStruct(ha.shape, F32), jax.ShapeDtypeStruct(hb.shape, F32)],
        compiler_params=_params(("arbitrary",)),
        name="ffn",
    )(ha, hb, g, w1, w2)


def _t5_bucket(rel):
    nb = NUM_BUCKETS // 2
    max_exact = nb // 2
    ret = jnp.where(rel > 0, nb, 0)
    n = jnp.abs(rel)
    n_safe = jnp.maximum(n, 1).astype(F32)
    large = max_exact + (jnp.log(n_safe / max_exact) / math.log(MAX_DISTANCE / max_exact)
                         * (nb - max_exact)).astype(jnp.int32)
    large = jnp.minimum(large, nb - 1)
    return ret + jnp.where(n < max_exact, n, large)


def _bias_table(rel_bias):
    qi = jnp.arange(BLK)[:, None]
    ki = jnp.arange(3 * BLK)[None, :]
    rel = ki - BLK - qi
    onehot = (_t5_bucket(rel)[..., None] == jnp.arange(NUM_BUCKETS)).astype(F32)
    bias = jnp.einsum("qkb,bh->hkq", onehot, rel_bias.astype(F32), precision=lax.Precision.HIGHEST) * _LOG2E
    return jnp.where((jnp.abs(rel) <= WINDOW).T[None], bias, NEG_INF)


def _expand_matrix(d):
    k = np.arange(LANES)[:, None]
    col = np.arange(SSD_INNER)[None, :]
    hit = ((k % DT_REP) == d * SSD_HEADS + col // SSD_HEAD_DIM)
    return jnp.asarray(hit, dtype=BF16)


def _prep_layer_params(w_in, norm_mix_g, q_norm_g, k_norm_g, attn_sink, gmlp_ln_g, gmlp_ln_b, w_spatial,
                       b_spatial, conv_w, conv_b, dt_bias, a_log, d_skip, ssd_norm_g, w_up_attn, w_up_gmlp,
                       w_up_ssd, w_out, norm_ff_g, w_ff1, w_ff2):
    depth = w_in.shape[0]

    def cols(a, b):
        return w_in[:, :, a:b]

    w_main = jnp.concatenate([
        cols(_R_Z, _R_XS), cols(_R_XS, _R_B), cols(_R_GATE, _R_END), cols(_R_Q, _R_K),
        cols(_R_GU, _R_GV), cols(_R_GV, _R_Z), cols(_R_B, _R_DT), cols(_R_K, _R_GU)], axis=-1).astype(BF16)
    w_dt = cols(_R_DT, _R_GATE)
    rep = LANES // DT_REP
    gw = GMLP_WIDTH // GMLP_GROUPS

    def dt_lanes(v):
        return jnp.tile(v.reshape(depth, 1, DT_REP), (1, 1, rep))

    return dict(
        norm_mix_g=norm_mix_g[:, None, :],
        w_main=w_main,
        w_dt=jnp.tile(w_dt, (1, 1, rep)).astype(BF16),
        w_dtT=jnp.swapaxes(w_dt, 1, 2).astype(BF16),
        kscale=jnp.tile(q_norm_g * k_norm_g * (HEAD_DIM ** -0.5 * _LOG2E), (1, ATTN_KV_HEADS))[:, None, :],
        sink=attn_sink[:, None, :] * _LOG2E,
        ln_g=gmlp_ln_g[:, None, :], ln_b=gmlp_ln_b[:, None, :],
        w_s=w_spatial.astype(BF16),
        b_s=jnp.repeat(jnp.swapaxes(b_spatial, 1, 2), gw, axis=-1),
        conv_w=conv_w, conv_b=conv_b[:, None, :],
        dtb=dt_lanes(dt_bias), alog=dt_lanes(a_log),
        dtbT=dt_bias.reshape(depth, DT_REP, 1), alogT=a_log.reshape(depth, DT_REP, 1),
        dskip=dt_lanes(jnp.concatenate([d_skip, d_skip], axis=-1)),
        ssd_norm_g=ssd_norm_g[:, None, :],
        w_a=w_up_attn.astype(BF16), w_b=w_up_gmlp.astype(BF16), w_c=w_up_ssd.astype(BF16),
        w_o=w_out.astype(BF16),
        norm_ff_g=norm_ff_g[:, None, :],
        w1=w_ff1.astype(BF16), w2=w_ff2.astype(BF16),
    )


def _layer(xa, xb, p, bias, e_mats, lay):
    proj, dt, dtT = _inproj(xa, xb, p, lay)
    ya = _attention(proj, bias, p["sink"], lay)
    yc = _ssd(proj, dt, dtT, p, e_mats, lay)
    ha, hb = _merge(xa, xb, ya, yc, proj, p)
    return _ffn(ha, hb, p["norm_ff_g"], p["w1"], p["w2"])


def _encoder(x_first, x_second, rel_bias, layer_params):
    b1, s1, _ = x_first.shape
    b2, s2, _ = x_second.shape
    t1, t2 = b1 * s1, b2 * s2
    lay = Layout(n_blocks=(t1 + t2) // BLK, n_first=t1 // BLK, per_seq_first=s1 // BLK,
                 per_seq_second=s2 // BLK)
    bias = _bias_table(rel_bias)
    e_mats = (_expand_matrix(0), _expand_matrix(1))
    params = _prep_layer_params(**layer_params)

    def step(carry, p):
        return _layer(*carry, p, bias, e_mats, lay), None

    (xa, xb), _ = lax.scan(step, (x_first.reshape(t1, D_MODEL), x_second.reshape(t2, D_MODEL)), params)
    return xa.reshape(b1, s1, D_MODEL), xb.reshape(b2, s2, D_MODEL)


def kernel(x_prompt, x_sample, rel_bias, norm_mix_g, w_in, q_norm_g, k_norm_g, attn_sink, gmlp_ln_g, gmlp_ln_b,
           w_spatial, b_spatial, conv_w, conv_b, dt_bias, a_log, d_skip, ssd_norm_g, w_up_attn, w_up_gmlp,
           w_up_ssd, w_out, norm_ff_g, w_ff1, w_ff2):
    layer_params = dict(
        w_in=w_in, norm_mix_g=norm_mix_g, q_norm_g=q_norm_g, k_norm_g=k_norm_g, attn_sink=attn_sink,
        gmlp_ln_g=gmlp_ln_g, gmlp_ln_b=gmlp_ln_b, w_spatial=w_spatial, b_spatial=b_spatial, conv_w=conv_w,
        conv_b=conv_b, dt_bias=dt_bias, a_log=a_log, d_skip=d_skip, ssd_norm_g=ssd_norm_g,
        w_up_attn=w_up_attn, w_up_gmlp=w_up_gmlp, w_up_ssd=w_up_ssd, w_out=w_out, norm_ff_g=norm_ff_g,
        w_ff1=w_ff1, w_ff2=w_ff2)
    return _encoder(x_prompt, x_sample, rel_bias, layer_params)
```

```python
import functools
import math
from typing import NamedTuple

import numpy as np
import jax
import jax.numpy as jnp
from jax import lax
from jax.experimental import pallas as pl
from jax.experimental.pallas import tpu as pltpu

F32 = jnp.float32
BF16 = jnp.bfloat16

D_MODEL = 1024
ATTN_HEADS = 8
ATTN_KV_HEADS = 2
HEAD_DIM = 64
ATTN_Q_WIDTH = ATTN_HEADS * HEAD_DIM
ATTN_KV_WIDTH = ATTN_KV_HEADS * HEAD_DIM
WINDOW = 128
NUM_BUCKETS = 32
MAX_DISTANCE = 128
GMLP_WIDTH = 512
GMLP_GROUPS = 4
SSD_INNER = 1024
SSD_HEAD_DIM = 64
SSD_HEADS = SSD_INNER // SSD_HEAD_DIM
SSD_GROUPS = 2
SSD_STATE = 128
N_DIR = 2
FF_DIM = 4 * D_MODEL
EPS = 1e-6
NEG_INF = -1e30

BLK = 128
LANES = 128
BF16_SUBLANES = 16
VMEM_LIMIT = 56 * 1024 * 1024
_LOG2E = math.log2(math.e)

COL_Z = 0
COL_XS = 1024
COL_GATE = 2048
COL_Q = 5120
COL_GU = 5632
COL_GV = 6144
COL_BC = 6656
COL_KV = 7168
P_WIDTH = 7424
DT_REP = 32
SSD_BC_WIDTH = 2 * SSD_GROUPS * SSD_STATE

_R_Q, _R_K, _R_V, _R_GU, _R_GV, _R_Z, _R_XS, _R_B, _R_C, _R_DT, _R_GATE, _R_END = (
    0, 512, 640, 768, 1280, 1792, 2816, 3840, 4096, 4352, 4384, 7456)


class Layout(NamedTuple):
    n_blocks: int
    n_first: int
    per_seq_first: int
    per_seq_second: int


def _seq_pos(blk, lay):
    in_first = blk < lay.n_first
    pos = jnp.where(in_first, lax.rem(blk, lay.per_seq_first),
                    lax.rem(blk - lay.n_first, lay.per_seq_second))
    last = jnp.where(in_first, lay.per_seq_first - 1, lay.per_seq_second - 1)
    return pos == 0, pos == last


def _sigmoid(x):
    return 0.5 * jnp.tanh(0.5 * x) + 0.5


def _silu(x):
    return x * _sigmoid(x)


def _gelu(x):
    return x * (0.5 * (1.0 + jnp.tanh(math.sqrt(2.0 / math.pi) * (x + 0.044715 * (x * x * x)))))


def _softplus(x):
    return jnp.maximum(x, 0.0) + jnp.log1p(jnp.exp(-jnp.abs(x)))


def _rms(x, g):
    return x * lax.rsqrt(jnp.mean(x * x, axis=-1, keepdims=True) + EPS) * g


def _dot(a, b):
    return jnp.dot(a, b, preferred_element_type=F32)


def _dot_nt(a, b):
    return lax.dot_general(a, b, (((1,), (1,)), ((), ())), preferred_element_type=F32)


def _const_spec(shape):
    nd = len(shape)
    return pl.BlockSpec(shape, lambda *_: (0,) * nd)


def _params(semantics):
    return pltpu.CompilerParams(dimension_semantics=semantics, vmem_limit_bytes=VMEM_LIMIT)


_PROJ_TILE = 512
_PROJ_LOOKAHEAD = 1
_X_HALO = 8


def _split2(x):
    hi = x.astype(BF16)
    return hi, (x - hi.astype(F32)).astype(BF16)


def _inproj_body(xa_ref, xb_ref, xa_prev_ref, xb_prev_ref, xa_next_ref, xb_next_ref, g_ref, w_ref, wdt_ref, wdtT_ref,
                 cw_ref, cb_ref, lng_ref, lnb_ref, kscale_ref, qsum_ref, ksum_ref, o_ref, dt_ref, dtT_ref,
                 xn_ref, xh_ref, *, lay):
    tm = _PROJ_TILE
    nblk = tm // BLK
    i = pl.program_id(0)
    keep_prev = jnp.where(_seq_pos(i * nblk, lay)[0], 0.0, 1.0)
    keep_next = jnp.where(_seq_pos(i * nblk + nblk - 1, lay)[1], 0.0, 1.0)
    in_first = i < lay.n_first // nblk

    def normalise(x_ref, prev_ref, next_ref):
        xn_ref[...] = _rms(x_ref[...], g_ref[...]).astype(BF16)
        xh_ref[...] = _rms(jnp.concatenate([prev_ref[...], next_ref[...]], axis=0), g_ref[...]).astype(BF16)

    pl.when(in_first)(lambda: normalise(xa_ref, xa_prev_ref, xa_next_ref))
    pl.when(jnp.logical_not(in_first))(lambda: normalise(xb_ref, xb_prev_ref, xb_next_ref))
    xn = xn_ref[...]
    xh = xh_ref[...]

    def put(c0, val):
        o_ref[:, c0:c0 + val.shape[1]] = val.astype(BF16)

    def conv_silu(k0):
        def post(c0, y):
            width = y.shape[1]
            yh = _dot(xh, w_ref[:, c0:c0 + width])
            halo_p = yh[_X_HALO - 1:_X_HALO] * keep_prev
            halo_n = yh[_X_HALO:_X_HALO + 1] * keep_next
            up = pltpu.roll(y, 1, axis=0)
            dn = pltpu.roll(y, tm - 1, axis=0)
            r8 = lax.broadcasted_iota(jnp.int32, (8, width), 0)
            y_m1 = jnp.concatenate([jnp.where(r8 == 0, halo_p, up[:8]), up[8:]], axis=0)
            y_p1 = jnp.concatenate([dn[:tm - 8], jnp.where(r8 == 7, halo_n, dn[tm - 8:])], axis=0)
            w = cw_ref[:, k0:k0 + width]
            put(c0, _silu(cb_ref[:, k0:k0 + width] + y_m1 * w[0:1] + y * w[1:2] + y_p1 * w[2:3]))
        return post

    def q_norm(c0, q):
        qss = _dot(jnp.concatenate(_split2(q * q), axis=1), qsum_ref[...])
        put(c0, q * lax.rsqrt(qss * (1.0 / HEAD_DIM) + EPS))

    def gelu_ln(c0, y):
        vf = _gelu(y)
        vc = vf - jnp.mean(vf, axis=-1, keepdims=True)
        var = jnp.mean(vc * vc, axis=-1, keepdims=True)
        put(c0, vc * lax.rsqrt(var + EPS) * lng_ref[...] + lnb_ref[...])

    def kv_norm(c0, kv):
        kf = kv[:, :ATTN_KV_WIDTH]
        kss = _dot(jnp.concatenate(_split2(kf * kf), axis=1), ksum_ref[...])
        put(c0, kf * lax.rsqrt(kss * (1.0 / HEAD_DIM) + EPS) * kscale_ref[...])
        put(c0 + ATTN_KV_WIDTH, kv[:, ATTN_KV_WIDTH:])

    def gate(k):
        return (COL_GATE + k * D_MODEL, D_MODEL, lambda c0, y: put(c0, _sigmoid(y)))

    items = [
        (COL_XS, SSD_INNER, conv_silu(0)),
        (COL_Z, SSD_INNER, lambda c0, y: put(c0, _silu(y))),
        (COL_BC, SSD_BC_WIDTH, conv_silu(SSD_INNER)),
        gate(0),
        (COL_GV, GMLP_WIDTH, gelu_ln),
        gate(1),
        (COL_GU, GMLP_WIDTH, lambda c0, y: put(c0, _gelu(y))),
        gate(2),
        (COL_Q, ATTN_Q_WIDTH, q_norm),
        (COL_KV, 2 * ATTN_KV_WIDTH, kv_norm),
    ]

    def proj(k):
        c0, width, _ = items[k]
        return _dot(xn, w_ref[:, c0:c0 + width])

    pending = [proj(k) for k in range(_PROJ_LOOKAHEAD)]
    for k, (c0, _, post) in enumerate(items):
        y = pending.pop(0)
        if k + _PROJ_LOOKAHEAD < len(items):
            pending.append(proj(k + _PROJ_LOOKAHEAD))
        post(c0, y)
    dt_ref[...] = _dot(xn, wdt_ref[...])
    dtT_ref[...] = _dot_nt(wdtT_ref[...], xn)


def _head_sum_matrix(width):
    k = np.arange(2 * width)[:, None]
    n = np.arange(width)[None, :]
    return jnp.asarray((k % width) // HEAD_DIM == n // HEAD_DIM, dtype=BF16)


def _two_stream_specs(tm, n_a, n_b):
    return (pl.BlockSpec((tm, D_MODEL), lambda i: (jnp.minimum(i, n_a - 1), 0)),
            pl.BlockSpec((tm, D_MODEL), lambda i: (jnp.clip(i - n_a, 0, n_b - 1), 0)))


def _inproj(xa, xb, p, lay):
    tm = _PROJ_TILE
    n_a, n_b = xa.shape[0] // tm, xb.shape[0] // tm
    t = xa.shape[0] + xb.shape[0]
    assert lay.per_seq_first % (tm // BLK) == 0 and lay.per_seq_second % (tm // BLK) == 0
    hal = tm // _X_HALO
    conv_w = SSD_INNER + SSD_BC_WIDTH

    def halo(tile0, n_tiles, shift):
        last = n_tiles * hal - 1
        return pl.BlockSpec((_X_HALO, D_MODEL),
                            lambda i: (jnp.clip((i - tile0 + shift) * hal - 1 + shift, 0, last), 0))

    return pl.pallas_call(
        functools.partial(_inproj_body, lay=lay),
        grid=(n_a + n_b,),
        in_specs=[
            *_two_stream_specs(tm, n_a, n_b),
            halo(0, n_a, 0), halo(n_a, n_b, 0), halo(0, n_a, 1), halo(n_a, n_b, 1),
            _const_spec((1, D_MODEL)),
            _const_spec((D_MODEL, P_WIDTH)),
            _const_spec((D_MODEL, LANES)),
            _const_spec((DT_REP, D_MODEL)),
            _const_spec((3, conv_w)), _const_spec((1, conv_w)),
            _const_spec((1, GMLP_WIDTH)), _const_spec((1, GMLP_WIDTH)),
            _const_spec((1, ATTN_KV_WIDTH)),
            _const_spec((2 * ATTN_Q_WIDTH, ATTN_Q_WIDTH)),
            _const_spec((2 * ATTN_KV_WIDTH, ATTN_KV_WIDTH)),
        ],
        out_specs=[
            pl.BlockSpec((tm, P_WIDTH), lambda i: (i, 0)),
            pl.BlockSpec((tm, LANES), lambda i: (i, 0)),
            pl.BlockSpec((DT_REP, tm), lambda i: (0, i)),
        ],
        out_shape=[
            jax.ShapeDtypeStruct((t, P_WIDTH), BF16),
            jax.ShapeDtypeStruct((t, LANES), F32),
            jax.ShapeDtypeStruct((DT_REP, t), F32),
        ],
        scratch_shapes=[pltpu.VMEM((tm, D_MODEL), BF16), pltpu.VMEM((2 * _X_HALO, D_MODEL), BF16)],
        compiler_params=_params(("parallel",)),
        name="inproj",
    )(xa, xb, xa, xb, xa, xb, p["norm_mix_g"], p["w_main"], p["w_dt"], p["w_dtT"], p["conv_w"], p["conv_b"], p["ln_g"], p["ln_b"],
      p["kscale"], _head_sum_matrix(ATTN_Q_WIDTH), _head_sum_matrix(ATTN_KV_WIDTH))


_ATTN_TILE_BLOCKS = 4
_ATTN_LOOKAHEAD = 4


def _attn_body(q_ref, kvp_ref, kvm_ref, kvn_ref, bias_ref, sink_ref, o_ref, *, lay):
    i = pl.program_id(0)
    nt = _ATTN_TILE_BLOCKS
    grp = ATTN_HEADS // ATTN_KV_HEADS
    kv = jnp.concatenate([kvp_ref[...], kvm_ref[...], kvn_ref[...]], axis=0)
    kn = [kv[:, g * HEAD_DIM:(g + 1) * HEAD_DIM] for g in range(ATTN_KV_HEADS)]
    vT = jnp.transpose(kv[:, ATTN_KV_WIDTH:].astype(F32)).astype(BF16)
    neg_prev = jnp.where(_seq_pos(i * nt, lay)[0], NEG_INF, 0.0)
    neg_next = jnp.where(_seq_pos(i * nt + nt - 1, lay)[1], NEG_INF, 0.0)

    qbs = [q_ref[jb * BLK:(jb + 1) * BLK, :] for jb in range(nt)]

    def scores(jb, h):
        s = _dot_nt(kn[h // grp][jb * BLK:(jb + 3) * BLK], qbs[jb][:, h * HEAD_DIM:(h + 1) * HEAD_DIM])
        return s + bias_ref[h]

    items = [(jb, h) for jb in range(nt) for h in range(ATTN_HEADS)]
    pending = [scores(*it) for it in items[:_ATTN_LOOKAHEAD]]
    outs = []
    for idx, (jb, h) in enumerate(items):
        s = pending.pop(0)
        if idx + _ATTN_LOOKAHEAD < len(items):
            pending.append(scores(*items[idx + _ATTN_LOOKAHEAD]))
        g = h // grp
        if jb == 0:
            s = jnp.concatenate([s[:BLK] + neg_prev, s[BLK:]], axis=0)
        if jb == nt - 1:
            s = jnp.concatenate([s[:2 * BLK], s[2 * BLK:] + neg_next], axis=0)
        snk = sink_ref[0:1, h:h + 1]
        m = jnp.maximum(jnp.max(s, axis=0, keepdims=True), snk)
        p = jnp.exp2(s - m)
        denom = jnp.sum(p, axis=0, keepdims=True) + jnp.exp2(snk - m)
        o = _dot(vT[g * HEAD_DIM:(g + 1) * HEAD_DIM, jb * BLK:(jb + 3) * BLK], p.astype(BF16))
        outs.append(o * (1.0 / denom))
        if h == ATTN_HEADS - 1:
            o_ref[jb * BLK:(jb + 1) * BLK, :] = jnp.transpose(jnp.concatenate(outs, axis=0)).astype(BF16)
            outs = []


def _attention(proj, bias, sink, lay):
    t = proj.shape[0]
    nt = _ATTN_TILE_BLOCKS
    rows = nt * BLK
    q_blk = COL_Q // ATTN_Q_WIDTH
    kv_w = 2 * ATTN_KV_WIDTH
    kv_blk = COL_KV // kv_w
    last_blk = lay.n_blocks - 1
    assert lay.per_seq_first % nt == 0 and lay.per_seq_second % nt == 0 and lay.n_first % nt == 0
    return pl.pallas_call(
        functools.partial(_attn_body, lay=lay),
        grid=(t // rows,),
        in_specs=[
            pl.BlockSpec((rows, ATTN_Q_WIDTH), lambda i: (i, q_blk)),
            pl.BlockSpec((BLK, kv_w), lambda i: (jnp.maximum(i * nt - 1, 0), kv_blk)),
            pl.BlockSpec((rows, kv_w), lambda i: (i, kv_blk)),
            pl.BlockSpec((BLK, kv_w), lambda i: (jnp.minimum(i * nt + nt, last_blk), kv_blk)),
            _const_spec((ATTN_HEADS, 3 * BLK, BLK)),
            _const_spec((1, ATTN_HEADS)),
        ],
        out_specs=pl.BlockSpec((rows, ATTN_Q_WIDTH), lambda i: (i, 0)),
        out_shape=jax.ShapeDtypeStruct((t, ATTN_Q_WIDTH), BF16),
        compiler_params=_params(("parallel",)),
        name="attention",
    )(proj, proj, proj, proj, bias, sink)


_SPLIT_TERMS = 4


def _lane_split(x):
    rep = lax.broadcasted_iota(jnp.int32, x.shape, 1) // DT_REP
    out = jnp.zeros(x.shape, BF16)
    rem = x
    for r in range(_SPLIT_TERMS):
        term = rem.astype(BF16)
        out = jnp.where(rep == r, term, out)
        rem = rem - term.astype(F32)
    return out


def _sublane_split(x):
    terms, rem = [], x
    for _ in range(_SPLIT_TERMS):
        term = rem.astype(BF16)
        terms.append(term)
        rem = rem - term.astype(F32)
    return jnp.concatenate(terms, axis=0)


def _expand(x, e_ref):
    return _dot(_lane_split(x), e_ref[...])


def _tri(upper):
    r = lax.broadcasted_iota(jnp.int32, (BLK, BLK), 0)
    c = lax.broadcasted_iota(jnp.int32, (BLK, BLK), 1)
    return (c >= r) if upper else (c <= r)


def _ones_where(mask):
    return jnp.where(mask, 1.0, 0.0).astype(BF16)


_SSD_TILE_CHUNKS = 4


def _rows(j):
    return slice(j * BLK, (j + 1) * BLK)


def _dt_row_major(dt_raw, dtb, alog):
    L = BLK
    dt = _softplus(dt_raw + dtb)
    sp = _lane_split(dt * (-jnp.exp(alog)))
    lane = lax.broadcasted_iota(jnp.int32, (L, LANES), 1)
    is_fwd = (lane & (DT_REP - 1)) < SSD_HEADS
    part = jnp.where(is_fwd, _dot(_ones_where(_tri(False)), sp), _dot(_ones_where(_tri(True)), sp))
    acum = part
    for r in range(1, _SPLIT_TERMS):
        acum = acum + pltpu.roll(part, r * DT_REP, axis=1)
    a_end = jnp.where(is_fwd[0:1], acum[L - 1:L], acum[0:1])
    return dt, acum, a_end


def _dt_head_major(dtT_raw, dtbT, alogT):
    nh = SSD_HEADS
    dtT = _softplus(dtT_raw + dtbT)
    spT = _sublane_split(dtT * (-jnp.exp(alogT)))
    pf = _dot(spT, _ones_where(_tri(True)))
    pb = _dot(spT, _ones_where(_tri(False)))
    acumT_f = pf[0:nh]
    acumT_b = pb[nh:2 * nh]
    for r in range(1, _SPLIT_TERMS):
        acumT_f = acumT_f + pf[r * DT_REP:r * DT_REP + nh]
        acumT_b = acumT_b + pb[r * DT_REP + nh:(r + 1) * DT_REP]
    log2_dtT = jnp.log(dtT) * _LOG2E
    return acumT_f * _LOG2E - log2_dtT[0:nh], acumT_b * _LOG2E - log2_dtT[nh:2 * nh]


def _state_factors(dt, acum, a_end, e_ref):
    ea_exp = _expand(jnp.exp(acum), e_ref)
    w_exp = _expand(dt * jnp.exp(a_end - acum), e_ref)
    cd_exp = _expand(jnp.broadcast_to(jnp.exp(a_end), (8, LANES)), e_ref)[0:1]
    return ea_exp, w_exp, cd_exp


def _state_sweep(st_ref, order, xs, bm, cm_bf, fac):
    gw = SSD_INNER // SSD_GROUPS
    upd = {}
    for j in order:
        wx = (xs[j] * fac[j][1]).astype(BF16)
        upd[j] = [_dot(jnp.transpose(bm[j][g]).astype(BF16), wx[:, g * gw:(g + 1) * gw])
                  for g in range(SSD_GROUPS)]
    sg = [st_ref[g] for g in range(SSD_GROUPS)]
    out = {}
    for j in order:
        y_off = [_dot(cm_bf[j][g], sg[g].astype(BF16)) for g in range(SSD_GROUPS)]
        neg_cd = -fac[j][2]
        sg = [upd[j][g] - sg[g] * neg_cd[:, g * gw:(g + 1) * gw] for g in range(SSD_GROUPS)]
        out[j] = jnp.concatenate(y_off, axis=1) * fac[j][0]
    for g in range(SSD_GROUPS):
        st_ref[g] = sg[g]
    return out


def _split_bc(bc):
    n = SSD_STATE
    return ([bc[:, g * n:(g + 1) * n] for g in range(SSD_GROUPS)],
            [bc[:, (SSD_GROUPS + g) * n:(SSD_GROUPS + g + 1) * n] for g in range(SSD_GROUPS)])


def _ssd_bwd_body(xs_ref, bc_ref, dt_ref, dtb_ref, alog_ref, eb_ref, yo_ref, st_ref, *, lay):
    ns = _SSD_TILE_CHUNKS
    blk0 = (lay.n_blocks // ns - 1 - pl.program_id(0)) * ns
    last = _seq_pos(blk0 + ns - 1, lay)[1]

    @pl.when(last)
    def _():
        st_ref[...] = jnp.zeros(st_ref.shape, F32)

    order = list(reversed(range(ns)))
    rm = {j: _dt_row_major(dt_ref[_rows(j), :], dtb_ref[...], alog_ref[...]) for j in order}
    fac = {j: _state_factors(*rm[j], eb_ref) for j in order}
    xs = {j: xs_ref[_rows(j), :].astype(F32) for j in order}
    bm = {j: _split_bc(bc_ref[_rows(j), :].astype(F32))[0] for j in order}
    cm_bf = {j: _split_bc(bc_ref[_rows(j), :])[1] for j in order}
    y_off = _state_sweep(st_ref, order, xs, bm, cm_bf, fac)
    for j in order:
        yo_ref[_rows(j), :] = y_off[j].astype(BF16)


def _ssd_fwd_body(xsc_ref, bcc_ref, dt_ref, dtT_ref, yob_ref, dtb_ref, alog_ref, dtbT_ref, alogT_ref,
                  dskip_ref, ef_ref, eb_ref, y_ref, st_ref, *, lay):
    ns = _SSD_TILE_CHUNKS
    first = _seq_pos(pl.program_id(0) * ns, lay)[0]

    @pl.when(first)
    def _():
        st_ref[...] = jnp.zeros(st_ref.shape, F32)

    L, nh = BLK, SSD_HEADS
    hpg = nh // SSD_GROUPS
    order = list(range(ns))
    lane = lax.broadcasted_iota(jnp.int32, (L, LANES), 1)
    xs_bf = {j: xsc_ref[_rows(j), :] for j in order}
    bc_bf = {j: bcc_ref[_rows(j), :] for j in order}
    bm_bf = {j: _split_bc(bc_bf[j])[0] for j in order}
    cm_bf = {j: _split_bc(bc_bf[j])[1] for j in order}
    cb = {j: [_dot_nt(cm_bf[j][g], bm_bf[j][g]) for g in range(SSD_GROUPS)] for j in order}
    rm = {j: _dt_row_major(dt_ref[_rows(j), :], dtb_ref[...], alog_ref[...]) for j in order}
    hm = {j: _dt_head_major(dtT_ref[:, _rows(j)], dtbT_ref[...], alogT_ref[...]) for j in order}
    fac = {j: _state_factors(*rm[j], ef_ref) for j in order}
    xs = {j: xs_bf[j].astype(F32) for j in order}
    bm, coef_exp = {}, {}
    for j in order:
        bm[j], cm = _split_bc(bc_bf[j].astype(F32))
        cbd = jnp.where((lane & (nh - 1)) < hpg, jnp.sum(cm[0] * bm[j][0], axis=1, keepdims=True),
                        jnp.sum(cm[1] * bm[j][1], axis=1, keepdims=True))
        coef_exp[j] = _expand(dskip_ref[...] + cbd * rm[j][0], eb_ref)
    y_off = _state_sweep(st_ref, order, xs, bm, cm_bf, fac)

    causal = _tri(False)
    left = lane < SSD_HEAD_DIM
    zero_bf = jnp.zeros((L, LANES), BF16)
    acum2 = {j: rm[j][1] * _LOG2E for j in order}
    y_loc = {j: [] for j in order}
    for k in range(nh // 2):
        g = (2 * k) // hpg
        for j in order:
            lhs = []
            for h in (2 * k, 2 * k + 1):
                col_f = jnp.broadcast_to(acum2[j][:, h:h + 1], (L, L))
                col_b = jnp.broadcast_to(acum2[j][:, nh + h:nh + h + 1], (L, L))
                e = jnp.where(causal, col_f - hm[j][0][h:h + 1, :], col_b - hm[j][1][h:h + 1, :])
                lhs.append((cb[j][g] * jnp.exp2(e)).astype(BF16))
            xs_pair = xs_bf[j][:, k * LANES:(k + 1) * LANES]
            rhs = jnp.concatenate([jnp.where(left, xs_pair, zero_bf), jnp.where(left, zero_bf, xs_pair)], axis=0)
            y_loc[j].append(_dot(jnp.concatenate(lhs, axis=1), rhs))
    for j in order:
        y = jnp.concatenate(y_loc[j], axis=1) + coef_exp[j] * xs[j] + yob_ref[_rows(j), :].astype(F32) + y_off[j]
        y_ref[_rows(j), :] = y.astype(BF16)


def _ssd(proj, dt, dtT, p, e_mats, lay):
    t = proj.shape[0]
    ns = _SSD_TILE_CHUNKS
    assert lay.per_seq_first % ns == 0 and lay.per_seq_second % ns == 0 and lay.n_first % ns == 0
    nb = lay.n_blocks // ns
    rows = ns * BLK
    xs_blk, bc_blk = COL_XS // SSD_INNER, COL_BC // SSD_BC_WIDTH
    state = pltpu.VMEM((SSD_GROUPS, SSD_STATE, SSD_INNER // SSD_GROUPS), F32)

    def rev(c):
        return nb - 1 - c

    yob = pl.pallas_call(
        functools.partial(_ssd_bwd_body, lay=lay),
        grid=(nb,),
        in_specs=[
            pl.BlockSpec((rows, SSD_INNER), lambda c: (rev(c), xs_blk)),
            pl.BlockSpec((rows, SSD_BC_WIDTH), lambda c: (rev(c), bc_blk)),
            pl.BlockSpec((rows, LANES), lambda c: (rev(c), 0)),
            _const_spec((1, LANES)), _const_spec((1, LANES)),
            _const_spec((LANES, SSD_INNER)),
        ],
        out_specs=pl.BlockSpec((rows, SSD_INNER), lambda c: (rev(c), 0)),
        out_shape=jax.ShapeDtypeStruct((t, SSD_INNER), BF16),
        scratch_shapes=[state],
        compiler_params=_params(("arbitrary",)),
        name="ssd_bwd",
    )(proj, proj, dt, p["dtb"], p["alog"], e_mats[1])

    return pl.pallas_call(
        functools.partial(_ssd_fwd_body, lay=lay),
        grid=(nb,),
        in_specs=[
            pl.BlockSpec((rows, SSD_INNER), lambda c: (c, xs_blk)),
            pl.BlockSpec((rows, SSD_BC_WIDTH), lambda c: (c, bc_blk)),
            pl.BlockSpec((rows, LANES), lambda c: (c, 0)),
            pl.BlockSpec((DT_REP, rows), lambda c: (0, c)),
            pl.BlockSpec((rows, SSD_INNER), lambda c: (c, 0)),
            _const_spec((1, LANES)), _const_spec((1, LANES)),
            _const_spec((DT_REP, 1)), _const_spec((DT_REP, 1)),
            _const_spec((1, LANES)),
            _const_spec((LANES, SSD_INNER)), _const_spec((LANES, SSD_INNER)),
        ],
        out_specs=pl.BlockSpec((rows, SSD_INNER), lambda c: (c, 0)),
        out_shape=jax.ShapeDtypeStruct((t, SSD_INNER), BF16),
        scratch_shapes=[state],
        compiler_params=_params(("arbitrary",)),
        name="ssd_fwd",
    )(proj, proj, dt, dtT, yob, p["dtb"], p["alog"], p["dtbT"], p["alogT"], p["dskip"], e_mats[0], e_mats[1])


_ROW_TILE = 512


def _merge_body(xa_ref, xb_ref, ya_ref, gu_ref, gv_ref, yc_ref, z_ref, g0_ref, g1_ref, g2_ref, ng_ref, ws_ref, bs_ref,
                wa_ref, wb_ref, wc_ref, wo_ref, ha_ref, hb_ref, *, n_a):
    gw = GMLP_WIDTH // GMLP_GROUPS
    chunks = []
    for c in range(_ROW_TILE // BLK):
        r = slice(c * BLK, (c + 1) * BLK)
        chunks.append(jnp.concatenate(
            [gu_ref[r, g * gw:(g + 1) * gw].astype(F32)
             * (_dot(ws_ref[g], gv_ref[r, g * gw:(g + 1) * gw]) + bs_ref[:, g * gw:(g + 1) * gw])
             for g in range(GMLP_GROUPS)], axis=1))
    yb = jnp.concatenate(chunks, axis=0).astype(BF16)
    y = yc_ref[...].astype(F32) * z_ref[...].astype(F32)
    sw = SSD_INNER // SSD_GROUPS
    yn = jnp.concatenate([_rms(y[:, g * sw:(g + 1) * sw], ng_ref[:, g * sw:(g + 1) * sw])
                          for g in range(SSD_GROUPS)], axis=1).astype(BF16)
    m = g0_ref[...].astype(F32) * _dot(ya_ref[...], wa_ref[...])
    m = m + g1_ref[...].astype(F32) * _dot(yb, wb_ref[...])
    m = m + g2_ref[...].astype(F32) * _dot(yn, wc_ref[...])
    m = m.astype(BF16)
    in_first = pl.program_id(0) < n_a

    @pl.when(in_first)
    def _():
        ha_ref[...] = xa_ref[...] + _dot(m, wo_ref[...])

    @pl.when(jnp.logical_not(in_first))
    def _():
        hb_ref[...] = xb_ref[...] + _dot(m, wo_ref[...])


def _merge(xa, xb, ya, yc, proj, p):
    tm = _ROW_TILE
    n_a, n_b = xa.shape[0] // tm, xb.shape[0] // tm
    g_blk = COL_GATE // D_MODEL

    def row(width, col_blk=0):
        return pl.BlockSpec((tm, width), lambda i: (i, col_blk))

    stream = _two_stream_specs(tm, n_a, n_b)
    return pl.pallas_call(
        functools.partial(_merge_body, n_a=n_a),
        grid=(n_a + n_b,),
        in_specs=[
            *stream, row(ATTN_Q_WIDTH), row(GMLP_WIDTH, COL_GU // GMLP_WIDTH), row(GMLP_WIDTH, COL_GV // GMLP_WIDTH),
            row(SSD_INNER), row(SSD_INNER, COL_Z // SSD_INNER),
            row(D_MODEL, g_blk), row(D_MODEL, g_blk + 1), row(D_MODEL, g_blk + 2),
            _const_spec((1, SSD_INNER)),
            _const_spec((GMLP_GROUPS, BLK, BLK)), _const_spec((BLK, GMLP_WIDTH)),
            _const_spec((ATTN_Q_WIDTH, D_MODEL)), _const_spec((GMLP_WIDTH, D_MODEL)),
            _const_spec((SSD_INNER, D_MODEL)), _const_spec((D_MODEL, D_MODEL)),
        ],
        out_specs=list(stream),
        out_shape=[jax.ShapeDtypeStruct(xa.shape, F32), jax.ShapeDtypeStruct(xb.shape, F32)],
        compiler_params=_params(("arbitrary",)),
        name="merge",
    )(xa, xb, ya, proj, proj, yc, proj, proj, proj, proj, p["ssd_norm_g"], p["w_s"], p["b_s"],
      p["w_a"], p["w_b"], p["w_c"], p["w_o"])


_FF_CHUNK = 1024


def _ffn_body(ha_ref, hb_ref, g_ref, w1_ref, w2_ref, oa_ref, ob_ref, *, n_a):
    def run(h_ref, o_ref):
        h = h_ref[...]
        hn = _rms(h, g_ref[...]).astype(BF16)
        acc = h
        for c0 in range(0, FF_DIM, _FF_CHUNK):
            a = jnp.maximum(_dot(hn, w1_ref[:, c0:c0 + _FF_CHUNK]), 0.0)
            acc = acc + _dot((a * a).astype(BF16), w2_ref[c0:c0 + _FF_CHUNK, :])
        o_ref[...] = acc

    in_first = pl.program_id(0) < n_a
    pl.when(in_first)(lambda: run(ha_ref, oa_ref))
    pl.when(jnp.logical_not(in_first))(lambda: run(hb_ref, ob_ref))


def _ffn(ha, hb, g, w1, w2):
    tm = _ROW_TILE
    n_a, n_b = ha.shape[0] // tm, hb.shape[0] // tm
    stream = _two_stream_specs(tm, n_a, n_b)
    return pl.pallas_call(
        functools.partial(_ffn_body, n_a=n_a),
        grid=(n_a + n_b,),
        in_specs=[*stream, _const_spec((1, D_MODEL)), _const_spec((D_MODEL, FF_DIM)), _const_spec((FF_DIM, D_MODEL))],
        out_specs=list(stream),
        out_shape=[jax.ShapeDtypeStruct(ha.shape, F32), jax.ShapeDtypeStruct(hb.shape, F32)],
        compiler_params=_params(("arbitrary",)),
        name="ffn",
    )(ha, hb, g, w1, w2)


def _t5_bucket(rel):
    nb = NUM_BUCKETS // 2
    max_exact = nb // 2
    ret = jnp.where(rel > 0, nb, 0)
    n = jnp.abs(rel)
    n_safe = jnp.maximum(n, 1).astype(F32)
    large = max_exact + (jnp.log(n_safe / max_exact) / math.log(MAX_DISTANCE / max_exact)
                         * (nb - max_exact)).astype(jnp.int32)
    large = jnp.minimum(large, nb - 1)
    return ret + jnp.where(n < max_exact, n, large)


def _bias_table(rel_bias):
    qi = jnp.arange(BLK)[:, None]
    ki = jnp.arange(3 * BLK)[None, :]
    rel = ki - BLK - qi
    onehot = (_t5_bucket(rel)[..., None] == jnp.arange(NUM_BUCKETS)).astype(F32)
    bias = jnp.einsum("qkb,bh->hkq", onehot, rel_bias.astype(F32), precision=lax.Precision.HIGHEST) * _LOG2E
    return jnp.where((jnp.abs(rel) <= WINDOW).T[None], bias, NEG_INF)


def _expand_matrix(d):
    k = np.arange(LANES)[:, None]
    col = np.arange(SSD_INNER)[None, :]
    hit = ((k % DT_REP) == d * SSD_HEADS + col // SSD_HEAD_DIM)
    return jnp.asarray(hit, dtype=BF16)


def _prep_layer_params(w_in, norm_mix_g, q_norm_g, k_norm_g, attn_sink, gmlp_ln_g, gmlp_ln_b, w_spatial,
                       b_spatial, conv_w, conv_b, dt_bias, a_log, d_skip, ssd_norm_g, w_up_attn, w_up_gmlp,
                       w_up_ssd, w_out, norm_ff_g, w_ff1, w_ff2):
    depth = w_in.shape[0]

    def cols(a, b):
        return w_in[:, :, a:b]

    w_main = jnp.concatenate([
        cols(_R_Z, _R_XS), cols(_R_XS, _R_B), cols(_R_GATE, _R_END), cols(_R_Q, _R_K),
        cols(_R_GU, _R_GV), cols(_R_GV, _R_Z), cols(_R_B, _R_DT), cols(_R_K, _R_GU)], axis=-1).astype(BF16)
    w_dt = cols(_R_DT, _R_GATE)
    rep = LANES // DT_REP
    gw = GMLP_WIDTH // GMLP_GROUPS

    def dt_lanes(v):
        return jnp.tile(v.reshape(depth, 1, DT_REP), (1, 1, rep))

    return dict(
        norm_mix_g=norm_mix_g[:, None, :],
        w_main=w_main,
        w_dt=jnp.tile(w_dt, (1, 1, rep)).astype(BF16),
        w_dtT=jnp.swapaxes(w_dt, 1, 2).astype(BF16),
        kscale=jnp.tile(q_norm_g * k_norm_g * (HEAD_DIM ** -0.5 * _LOG2E), (1, ATTN_KV_HEADS))[:, None, :],
        sink=attn_sink[:, None, :] * _LOG2E,
        ln_g=gmlp_ln_g[:, None, :], ln_b=gmlp_ln_b[:, None, :],
        w_s=w_spatial.astype(BF16),
        b_s=jnp.repeat(jnp.swapaxes(b_spatial, 1, 2), gw, axis=-1),
        conv_w=conv_w, conv_b=conv_b[:, None, :],
        dtb=dt_lanes(dt_bias), alog=dt_lanes(a_log),
        dtbT=dt_bias.reshape(depth, DT_REP, 1), alogT=a_log.reshape(depth, DT_REP, 1),
        dskip=dt_lanes(jnp.concatenate([d_skip, d_skip], axis=-1)),
        ssd_norm_g=ssd_norm_g[:, None, :],
        w_a=w_up_attn.astype(BF16), w_b=w_up_gmlp.astype(BF16), w_c=w_up_ssd.astype(BF16),
        w_o=w_out.astype(BF16),
        norm_ff_g=norm_ff_g[:, None, :],
        w1=w_ff1.astype(BF16), w2=w_ff2.astype(BF16),
    )


def _layer(xa, xb, p, bias, e_mats, lay):
    proj, dt, dtT = _inproj(xa, xb, p, lay)
    ya = _attention(proj, bias, p["sink"], lay)
    yc = _ssd(proj, dt, dtT, p, e_mats, lay)
    ha, hb = _merge(xa, xb, ya, yc, proj, p)
    return _ffn(ha, hb, p["norm_ff_g"], p["w1"], p["w2"])


def _encoder(x_first, x_second, rel_bias, layer_params):
    b1, s1, _ = x_first.shape
    b2, s2, _ = x_second.shape
    t1, t2 = b1 * s1, b2 * s2
    lay = Layout(n_blocks=(t1 + t2) // BLK, n_first=t1 // BLK, per_seq_first=s1 // BLK,
                 per_seq_second=s2 // BLK)
    bias = _bias_table(rel_bias)
    e_mats = (_expand_matrix(0), _expand_matrix(1))
    params = _prep_layer_params(**layer_params)

    xa, xb = x_first.reshape(t1, D_MODEL), x_second.reshape(t2, D_MODEL)
    for layer in range(layer_params["w_in"].shape[0]):
        xa, xb = _layer(xa, xb, {k: v[layer] for k, v in params.items()}, bias, e_mats, lay)
    return xa.reshape(b1, s1, D_MODEL), xb.reshape(b2, s2, D_MODEL)


def kernel(x_prompt, x_sample, rel_bias, norm_mix_g, w_in, q_norm_g, k_norm_g, attn_sink, gmlp_ln_g, gmlp_ln_b,
           w_spatial, b_spatial, conv_w, conv_b, dt_bias, a_log, d_skip, ssd_norm_g, w_up_attn, w_up_gmlp,
           w_up_ssd, w_out, norm_ff_g, w_ff1, w_ff2):
    layer_params = dict(
        w_in=w_in, norm_mix_g=norm_mix_g, q_norm_g=q_norm_g, k_norm_g=k_norm_g, attn_sink=attn_sink,
        gmlp_ln_g=gmlp_ln_g, gmlp_ln_b=gmlp_ln_b, w_spatial=w_spatial, b_spatial=b_spatial, conv_w=conv_w,
        conv_b=conv_b, dt_bias=dt_bias, a_log=a_log, d_skip=d_skip, ssd_norm_g=ssd_norm_g,
        w_up_attn=w_up_attn, w_up_gmlp=w_up_gmlp, w_up_ssd=w_up_ssd, w_out=w_out, norm_ff_g=norm_ff_g,
        w_ff1=w_ff1, w_ff2=w_ff2)
    return _encoder(x_prompt, x_sample, rel_bias, layer_params)
```

```python
import functools
import math
from typing import NamedTuple

import numpy as np
import jax
import jax.numpy as jnp
from jax import lax
from jax.experimental import pallas as pl
from jax.experimental.pallas import tpu as pltpu

F32 = jnp.float32
BF16 = jnp.bfloat16

D_MODEL = 1024
ATTN_HEADS = 8
ATTN_KV_HEADS = 2
HEAD_DIM = 64
ATTN_Q_WIDTH = ATTN_HEADS * HEAD_DIM
ATTN_KV_WIDTH = ATTN_KV_HEADS * HEAD_DIM
WINDOW = 128
NUM_BUCKETS = 32
MAX_DISTANCE = 128
GMLP_WIDTH = 512
GMLP_GROUPS = 4
SSD_INNER = 1024
SSD_HEAD_DIM = 64
SSD_HEADS = SSD_INNER // SSD_HEAD_DIM
SSD_GROUPS = 2
SSD_STATE = 128
N_DIR = 2
FF_DIM = 4 * D_MODEL
EPS = 1e-6
NEG_INF = -1e30

BLK = 128
LANES = 128
BF16_SUBLANES = 16
VMEM_LIMIT = 56 * 1024 * 1024
_LOG2E = math.log2(math.e)

COL_Z = 0
COL_XS = 1024
COL_GATE = 2048
COL_Q = 5120
COL_GU = 5632
COL_GV = 6144
COL_BC = 6656
COL_KV = 7168
P_WIDTH = 7424
DT_REP = 32
SSD_BC_WIDTH = 2 * SSD_GROUPS * SSD_STATE

_R_Q, _R_K, _R_V, _R_GU, _R_GV, _R_Z, _R_XS, _R_B, _R_C, _R_DT, _R_GATE, _R_END = (
    0, 512, 640, 768, 1280, 1792, 2816, 3840, 4096, 4352, 4384, 7456)


class Layout(NamedTuple):
    n_blocks: int
    n_first: int
    per_seq_first: int
    per_seq_second: int


def _seq_pos(blk, lay):
    in_first = blk < lay.n_first
    pos = jnp.where(in_first, lax.rem(blk, lay.per_seq_first),
                    lax.rem(blk - lay.n_first, lay.per_seq_second))
    last = jnp.where(in_first, lay.per_seq_first - 1, lay.per_seq_second - 1)
    return pos == 0, pos == last


def _sigmoid(x):
    return 0.5 * jnp.tanh(0.5 * x) + 0.5


def _silu(x):
    return x * _sigmoid(x)


def _gelu(x):
    return x * (0.5 * (1.0 + jnp.tanh(math.sqrt(2.0 / math.pi) * (x + 0.044715 * (x * x * x)))))


def _softplus(x):
    return jnp.maximum(x, 0.0) + jnp.log1p(jnp.exp(-jnp.abs(x)))


def _rms(x, g):
    return x * lax.rsqrt(jnp.mean(x * x, axis=-1, keepdims=True) + EPS) * g


def _dot(a, b):
    return jnp.dot(a, b, preferred_element_type=F32)


def _dot_nt(a, b):
    return lax.dot_general(a, b, (((1,), (1,)), ((), ())), preferred_element_type=F32)


def _const_spec(shape):
    nd = len(shape)
    return pl.BlockSpec(shape, lambda *_: (0,) * nd)


def _params(semantics):
    return pltpu.CompilerParams(dimension_semantics=semantics, vmem_limit_bytes=VMEM_LIMIT)


_PROJ_TILE = 512
_PROJ_LOOKAHEAD = 1
_X_HALO = 8


def _split2(x):
    hi = x.astype(BF16)
    return hi, (x - hi.astype(F32)).astype(BF16)


def _inproj_body(xa_ref, xb_ref, xa_prev_ref, xb_prev_ref, xa_next_ref, xb_next_ref, g_ref, w_ref, wdt_ref, wdtT_ref,
                 cw_ref, cb_ref, lng_ref, lnb_ref, kscale_ref, qsum_ref, ksum_ref, o_ref, dt_ref, dtT_ref,
                 xn_ref, xh_ref, *, lay):
    tm = _PROJ_TILE
    nblk = tm // BLK
    i = pl.program_id(0)
    keep_prev = jnp.where(_seq_pos(i * nblk, lay)[0], 0.0, 1.0)
    keep_next = jnp.where(_seq_pos(i * nblk + nblk - 1, lay)[1], 0.0, 1.0)
    in_first = i < lay.n_first // nblk

    def normalise(x_ref, prev_ref, next_ref):
        xn_ref[...] = _rms(x_ref[...], g_ref[...]).astype(BF16)
        xh_ref[...] = _rms(jnp.concatenate([prev_ref[...], next_ref[...]], axis=0), g_ref[...]).astype(BF16)

    pl.when(in_first)(lambda: normalise(xa_ref, xa_prev_ref, xa_next_ref))
    pl.when(jnp.logical_not(in_first))(lambda: normalise(xb_ref, xb_prev_ref, xb_next_ref))
    xn = xn_ref[...]
    xh = xh_ref[...]

    def put(c0, val):
        o_ref[:, c0:c0 + val.shape[1]] = val.astype(BF16)

    def conv_silu(k0):
        def post(c0, y):
            width = y.shape[1]
            yh = _dot(xh, w_ref[:, c0:c0 + width])
            halo_p = yh[_X_HALO - 1:_X_HALO] * keep_prev
            halo_n = yh[_X_HALO:_X_HALO + 1] * keep_next
            up = pltpu.roll(y, 1, axis=0)
            dn = pltpu.roll(y, tm - 1, axis=0)
            r8 = lax.broadcasted_iota(jnp.int32, (8, width), 0)
            y_m1 = jnp.concatenate([jnp.where(r8 == 0, halo_p, up[:8]), up[8:]], axis=0)
            y_p1 = jnp.concatenate([dn[:tm - 8], jnp.where(r8 == 7, halo_n, dn[tm - 8:])], axis=0)
            w = cw_ref[:, k0:k0 + width]
            put(c0, _silu(cb_ref[:, k0:k0 + width] + y_m1 * w[0:1] + y * w[1:2] + y_p1 * w[2:3]))
        return post

    def q_norm(c0, q):
        sq = jnp.concatenate(_split2(q * q), axis=1)

        def tail():
            qss = _dot(sq, qsum_ref[...])
            put(c0, q * lax.rsqrt(qss * (1.0 / HEAD_DIM) + EPS))
        return tail

    def gelu_ln(c0, y):
        vf = _gelu(y)
        vc = vf - jnp.mean(vf, axis=-1, keepdims=True)
        var = jnp.mean(vc * vc, axis=-1, keepdims=True)
        put(c0, vc * lax.rsqrt(var + EPS) * lng_ref[...] + lnb_ref[...])

    def kv_norm(c0, kv):
        kf = kv[:, :ATTN_KV_WIDTH]
        sq = jnp.concatenate(_split2(kf * kf), axis=1)
        put(c0 + ATTN_KV_WIDTH, kv[:, ATTN_KV_WIDTH:])

        def tail():
            kss = _dot(sq, ksum_ref[...])
            put(c0, kf * lax.rsqrt(kss * (1.0 / HEAD_DIM) + EPS) * kscale_ref[...])
        return tail

    def gate(k):
        return (COL_GATE + k * D_MODEL, D_MODEL, lambda c0, y: put(c0, _sigmoid(y)))

    items = [
        (COL_XS, SSD_INNER, conv_silu(0)),
        (COL_Z, SSD_INNER, lambda c0, y: put(c0, _silu(y))),
        (COL_BC, SSD_BC_WIDTH, conv_silu(SSD_INNER)),
        gate(0),
        (COL_GV, GMLP_WIDTH, gelu_ln),
        gate(1),
        (COL_GU, GMLP_WIDTH, lambda c0, y: put(c0, _gelu(y))),
        gate(2),
        (COL_Q, ATTN_Q_WIDTH, q_norm),
        (COL_KV, 2 * ATTN_KV_WIDTH, kv_norm),
    ]

    def proj(k):
        c0, width, _ = items[k]
        return _dot(xn, w_ref[:, c0:c0 + width])

    pending = [proj(k) for k in range(_PROJ_LOOKAHEAD)]
    tail = None
    for k, (c0, _, post) in enumerate(items):
        y = pending.pop(0)
        if k + _PROJ_LOOKAHEAD < len(items):
            pending.append(proj(k + _PROJ_LOOKAHEAD))
        if tail is not None:
            tail()
        tail = post(c0, y)
    dt_ref[...] = _dot(xn, wdt_ref[...])
    dtT_ref[...] = _dot_nt(wdtT_ref[...], xn)
    if tail is not None:
        tail()


def _head_sum_matrix(width):
    k = np.arange(2 * width)[:, None]
    n = np.arange(width)[None, :]
    return jnp.asarray((k % width) // HEAD_DIM == n // HEAD_DIM, dtype=BF16)


def _two_stream_specs(tm, n_a, n_b):
    return (pl.BlockSpec((tm, D_MODEL), lambda i: (jnp.minimum(i, n_a - 1), 0)),
            pl.BlockSpec((tm, D_MODEL), lambda i: (jnp.clip(i - n_a, 0, n_b - 1), 0)))


def _inproj(xa, xb, p, lay):
    tm = _PROJ_TILE
    n_a, n_b = xa.shape[0] // tm, xb.shape[0] // tm
    t = xa.shape[0] + xb.shape[0]
    assert lay.per_seq_first % (tm // BLK) == 0 and lay.per_seq_second % (tm // BLK) == 0
    hal = tm // _X_HALO
    conv_w = SSD_INNER + SSD_BC_WIDTH

    def halo(tile0, n_tiles, shift):
        last = n_tiles * hal - 1
        return pl.BlockSpec((_X_HALO, D_MODEL),
                            lambda i: (jnp.clip((i - tile0 + shift) * hal - 1 + shift, 0, last), 0))

    return pl.pallas_call(
        functools.partial(_inproj_body, lay=lay),
        grid=(n_a + n_b,),
        in_specs=[
            *_two_stream_specs(tm, n_a, n_b),
            halo(0, n_a, 0), halo(n_a, n_b, 0), halo(0, n_a, 1), halo(n_a, n_b, 1),
            _const_spec((1, D_MODEL)),
            _const_spec((D_MODEL, P_WIDTH)),
            _const_spec((D_MODEL, LANES)),
            _const_spec((DT_REP, D_MODEL)),
            _const_spec((3, conv_w)), _const_spec((1, conv_w)),
            _const_spec((1, GMLP_WIDTH)), _const_spec((1, GMLP_WIDTH)),
            _const_spec((1, ATTN_KV_WIDTH)),
            _const_spec((2 * ATTN_Q_WIDTH, ATTN_Q_WIDTH)),
            _const_spec((2 * ATTN_KV_WIDTH, ATTN_KV_WIDTH)),
        ],
        out_specs=[
            pl.BlockSpec((tm, P_WIDTH), lambda i: (i, 0)),
            pl.BlockSpec((tm, LANES), lambda i: (i, 0)),
            pl.BlockSpec((DT_REP, tm), lambda i: (0, i)),
        ],
        out_shape=[
            jax.ShapeDtypeStruct((t, P_WIDTH), BF16),
            jax.ShapeDtypeStruct((t, LANES), F32),
            jax.ShapeDtypeStruct((DT_REP, t), F32),
        ],
        scratch_shapes=[pltpu.VMEM((tm, D_MODEL), BF16), pltpu.VMEM((2 * _X_HALO, D_MODEL), BF16)],
        compiler_params=_params(("parallel",)),
        name="inproj",
    )(xa, xb, xa, xb, xa, xb, p["norm_mix_g"], p["w_main"], p["w_dt"], p["w_dtT"], p["conv_w"], p["conv_b"], p["ln_g"], p["ln_b"],
      p["kscale"], _head_sum_matrix(ATTN_Q_WIDTH), _head_sum_matrix(ATTN_KV_WIDTH))


_ATTN_TILE_BLOCKS = 4
_ATTN_LOOKAHEAD = 4


def _attention_tile(tile, q_ref, kvp_ref, kvm_ref, kvn_ref, bias_ref, sink_ref, put_block, after_item, lay):
    nt = _ATTN_TILE_BLOCKS
    grp = ATTN_HEADS // ATTN_KV_HEADS
    kv = jnp.concatenate([kvp_ref[...], kvm_ref[...], kvn_ref[...]], axis=0)
    kn = [kv[:, g * HEAD_DIM:(g + 1) * HEAD_DIM] for g in range(ATTN_KV_HEADS)]
    vT = jnp.transpose(kv[:, ATTN_KV_WIDTH:].astype(F32)).astype(BF16)
    neg_prev = jnp.where(_seq_pos(tile * nt, lay)[0], NEG_INF, 0.0)
    neg_next = jnp.where(_seq_pos(tile * nt + nt - 1, lay)[1], NEG_INF, 0.0)

    qbs = [q_ref[jb * BLK:(jb + 1) * BLK, :] for jb in range(nt)]

    def scores(jb, hp):
        slab = qbs[jb][:, hp * LANES:(hp + 1) * LANES]
        q2 = jnp.concatenate([slab[:, :HEAD_DIM], slab[:, HEAD_DIM:]], axis=0)
        return _dot_nt(kn[2 * hp // grp][jb * BLK:(jb + 3) * BLK], q2) + bias_ref[hp]

    n_pairs = ATTN_HEADS // 2
    items = [(jb, hp) for jb in range(nt) for hp in range(n_pairs)]
    pending = [scores(*it) for it in items[:_ATTN_LOOKAHEAD]]
    outs = []
    for idx, (jb, hp) in enumerate(items):
        s = pending.pop(0)
        if idx + _ATTN_LOOKAHEAD < len(items):
            pending.append(scores(*items[idx + _ATTN_LOOKAHEAD]))
        g = 2 * hp // grp
        if jb == 0:
            s = jnp.concatenate([s[:BLK] + neg_prev, s[BLK:]], axis=0)
        if jb == nt - 1:
            s = jnp.concatenate([s[:2 * BLK], s[2 * BLK:] + neg_next], axis=0)
        snk = sink_ref[hp:hp + 1, :]
        m = jnp.maximum(jnp.max(s, axis=0, keepdims=True), snk)
        p = jnp.exp2(s - m)
        denom = jnp.sum(p, axis=0, keepdims=True) + jnp.exp2(snk - m)
        o = _dot(vT[g * HEAD_DIM:(g + 1) * HEAD_DIM, jb * BLK:(jb + 3) * BLK], p.astype(BF16))
        o = o * (1.0 / denom)
        outs += [o[:, :BLK], o[:, BLK:]]
        if hp == n_pairs - 1:
            put_block(jb, jnp.transpose(jnp.concatenate(outs, axis=0)).astype(BF16))
            outs = []
        after_item(idx)


_SPLIT_TERMS = 4


def _lane_split(x, terms=_SPLIT_TERMS):
    rep = lax.broadcasted_iota(jnp.int32, x.shape, 1) // DT_REP
    out = jnp.zeros(x.shape, BF16)
    rem = x
    for r in range(terms):
        term = rem.astype(BF16)
        out = jnp.where(rep == r, term, out)
        rem = rem - term.astype(F32)
    return out


def _sublane_split(x):
    terms, rem = [], x
    for _ in range(_SPLIT_TERMS):
        term = rem.astype(BF16)
        terms.append(term)
        rem = rem - term.astype(F32)
    return jnp.concatenate(terms, axis=0)


_EXPAND_TERMS = 2


def _expand(x, e_ref):
    return _dot(_lane_split(x, _EXPAND_TERMS), e_ref[...])


def _tri(upper):
    r = lax.broadcasted_iota(jnp.int32, (BLK, BLK), 0)
    c = lax.broadcasted_iota(jnp.int32, (BLK, BLK), 1)
    return (c >= r) if upper else (c <= r)


def _ones_where(mask):
    return jnp.where(mask, 1.0, 0.0).astype(BF16)


_SSD_TILE_CHUNKS = 4


def _rows(j):
    return slice(j * BLK, (j + 1) * BLK)


def _dt_row_major(dt_raw, dtb, alog):
    L = BLK
    dt = _softplus(dt_raw + dtb)
    sp = _lane_split(dt * (-jnp.exp(alog)))
    lane = lax.broadcasted_iota(jnp.int32, (L, LANES), 1)
    is_fwd = (lane & (DT_REP - 1)) < SSD_HEADS
    part = jnp.where(is_fwd, _dot(_ones_where(_tri(False)), sp), _dot(_ones_where(_tri(True)), sp))
    acum = part
    for r in range(1, _SPLIT_TERMS):
        acum = acum + pltpu.roll(part, r * DT_REP, axis=1)
    a_end = jnp.where(is_fwd[0:1], acum[L - 1:L], acum[0:1])
    return dt, acum, a_end


def _dt_head_major(dtT_raw, dtbT, alogT):
    nh = SSD_HEADS
    dtT = _softplus(dtT_raw + dtbT)
    spT = _sublane_split(dtT * (-jnp.exp(alogT)))
    pf = _dot(spT, _ones_where(_tri(True)))
    pb = _dot(spT, _ones_where(_tri(False)))
    acumT_f = pf[0:nh]
    acumT_b = pb[nh:2 * nh]
    for r in range(1, _SPLIT_TERMS):
        acumT_f = acumT_f + pf[r * DT_REP:r * DT_REP + nh]
        acumT_b = acumT_b + pb[r * DT_REP + nh:(r + 1) * DT_REP]
    log2_dtT = jnp.log(dtT) * _LOG2E
    return acumT_f * _LOG2E - log2_dtT[0:nh], acumT_b * _LOG2E - log2_dtT[nh:2 * nh]


def _state_factors(dt, acum, a_end, e_ref):
    ea_exp = _expand(jnp.exp(acum), e_ref)
    w_exp = _expand(dt * jnp.exp(a_end - acum), e_ref)
    cd_exp = _expand(jnp.broadcast_to(jnp.exp(a_end), (8, LANES)), e_ref)[0:1]
    return ea_exp, w_exp, cd_exp


def _state_sweep(st_ref, order, xs, bm, cm_bf, fac):
    gw = SSD_INNER // SSD_GROUPS
    upd = {}
    for j in order:
        wx = (xs[j] * fac[j][1]).astype(BF16)
        upd[j] = [_dot(jnp.transpose(bm[j][g]).astype(BF16), wx[:, g * gw:(g + 1) * gw])
                  for g in range(SSD_GROUPS)]
    sg = [st_ref[g] for g in range(SSD_GROUPS)]
    out = {}
    for j in order:
        y_off = [_dot(cm_bf[j][g], sg[g].astype(BF16)) for g in range(SSD_GROUPS)]
        neg_cd = -fac[j][2]
        sg = [upd[j][g] - sg[g] * neg_cd[:, g * gw:(g + 1) * gw] for g in range(SSD_GROUPS)]
        out[j] = jnp.concatenate(y_off, axis=1) * fac[j][0]
    for g in range(SSD_GROUPS):
        st_ref[g] = sg[g]
    return out


def _split_bc(bc):
    n = SSD_STATE
    return ([bc[:, g * n:(g + 1) * n] for g in range(SSD_GROUPS)],
            [bc[:, (SSD_GROUPS + g) * n:(SSD_GROUPS + g + 1) * n] for g in range(SSD_GROUPS)])


def _ssd_bwd_body(xs_ref, bc_ref, dt_ref, dtb_ref, alog_ref, eb_ref, yo_ref, st_ref, *, lay):
    ns = _SSD_TILE_CHUNKS
    blk0 = (lay.n_blocks // ns - 1 - pl.program_id(0)) * ns
    last = _seq_pos(blk0 + ns - 1, lay)[1]

    @pl.when(last)
    def _():
        st_ref[...] = jnp.zeros(st_ref.shape, F32)

    order = list(reversed(range(ns)))
    rm = {j: _dt_row_major(dt_ref[_rows(j), :], dtb_ref[...], alog_ref[...]) for j in order}
    fac = {j: _state_factors(*rm[j], eb_ref) for j in order}
    xs = {j: xs_ref[_rows(j), :].astype(F32) for j in order}
    bm = {j: _split_bc(bc_ref[_rows(j), :].astype(F32))[0] for j in order}
    cm_bf = {j: _split_bc(bc_ref[_rows(j), :])[1] for j in order}
    y_off = _state_sweep(st_ref, order, xs, bm, cm_bf, fac)
    for j in order:
        yo_ref[_rows(j), :] = y_off[j].astype(BF16)


def _ssd_fwd_body(xsc_ref, bcc_ref, dt_ref, dtT_ref, yob_ref, dtb_ref, alog_ref, dtbT_ref, alogT_ref,
                  dskip_ref, ef_ref, eb_ref, y_ref, st_ref, *, lay):
    ns = _SSD_TILE_CHUNKS
    first = _seq_pos(pl.program_id(0) * ns, lay)[0]

    @pl.when(first)
    def _():
        st_ref[...] = jnp.zeros(st_ref.shape, F32)

    L, nh = BLK, SSD_HEADS
    hpg = nh // SSD_GROUPS
    order = list(range(ns))
    lane = lax.broadcasted_iota(jnp.int32, (L, LANES), 1)
    xs_bf = {j: xsc_ref[_rows(j), :] for j in order}
    bc_bf = {j: bcc_ref[_rows(j), :] for j in order}
    bm_bf = {j: _split_bc(bc_bf[j])[0] for j in order}
    cm_bf = {j: _split_bc(bc_bf[j])[1] for j in order}
    cb = {j: [_dot_nt(cm_bf[j][g], bm_bf[j][g]) for g in range(SSD_GROUPS)] for j in order}
    rm = {j: _dt_row_major(dt_ref[_rows(j), :], dtb_ref[...], alog_ref[...]) for j in order}
    hm = {j: _dt_head_major(dtT_ref[:, _rows(j)], dtbT_ref[...], alogT_ref[...]) for j in order}
    fac = {j: _state_factors(*rm[j], ef_ref) for j in order}
    xs = {j: xs_bf[j].astype(F32) for j in order}
    bm, coef_exp = {}, {}
    for j in order:
        bm[j], cm = _split_bc(bc_bf[j].astype(F32))
        cbd = jnp.where((lane & (nh - 1)) < hpg, jnp.sum(cm[0] * bm[j][0], axis=1, keepdims=True),
                        jnp.sum(cm[1] * bm[j][1], axis=1, keepdims=True))
        coef_exp[j] = _expand(dskip_ref[...] + cbd * rm[j][0], eb_ref)
    y_off = _state_sweep(st_ref, order, xs, bm, cm_bf, fac)

    causal = _tri(False)
    left = lane < SSD_HEAD_DIM
    zero_bf = jnp.zeros((L, LANES), BF16)
    acum2 = {j: rm[j][1] * _LOG2E for j in order}
    y_loc = {j: [] for j in order}
    for k in range(nh // 2):
        g = (2 * k) // hpg
        for j in order:
            lhs = []
            for h in (2 * k, 2 * k + 1):
                col_f = jnp.broadcast_to(acum2[j][:, h:h + 1], (L, L))
                col_b = jnp.broadcast_to(acum2[j][:, nh + h:nh + h + 1], (L, L))
                e = jnp.where(causal, col_f - hm[j][0][h:h + 1, :], col_b - hm[j][1][h:h + 1, :])
                lhs.append((cb[j][g] * jnp.exp2(e)).astype(BF16))
            xs_pair = xs_bf[j][:, k * LANES:(k + 1) * LANES]
            rhs = jnp.concatenate([jnp.where(left, xs_pair, zero_bf), jnp.where(left, zero_bf, xs_pair)], axis=0)
            y_loc[j].append(_dot(jnp.concatenate(lhs, axis=1), rhs))
    for j in order:
        y = jnp.concatenate(y_loc[j], axis=1) + coef_exp[j] * xs[j] + yob_ref[_rows(j), :].astype(F32) + y_off[j]
        y_ref[_rows(j), :] = y.astype(BF16)


def _ssd(proj, dt, dtT, p, e_mats, lay):
    t = proj.shape[0]
    ns = _SSD_TILE_CHUNKS
    assert lay.per_seq_first % ns == 0 and lay.per_seq_second % ns == 0 and lay.n_first % ns == 0
    nb = lay.n_blocks // ns
    rows = ns * BLK
    xs_blk, bc_blk = COL_XS // SSD_INNER, COL_BC // SSD_BC_WIDTH
    state = pltpu.VMEM((SSD_GROUPS, SSD_STATE, SSD_INNER // SSD_GROUPS), F32)

    def rev(c):
        return nb - 1 - c

    yob = pl.pallas_call(
        functools.partial(_ssd_bwd_body, lay=lay),
        grid=(nb,),
        in_specs=[
            pl.BlockSpec((rows, SSD_INNER), lambda c: (rev(c), xs_blk)),
            pl.BlockSpec((rows, SSD_BC_WIDTH), lambda c: (rev(c), bc_blk)),
            pl.BlockSpec((rows, LANES), lambda c: (rev(c), 0)),
            _const_spec((1, LANES)), _const_spec((1, LANES)),
            _const_spec((LANES, SSD_INNER)),
        ],
        out_specs=pl.BlockSpec((rows, SSD_INNER), lambda c: (rev(c), 0)),
        out_shape=jax.ShapeDtypeStruct((t, SSD_INNER), BF16),
        scratch_shapes=[state],
        compiler_params=_params(("arbitrary",)),
        name="ssd_bwd",
    )(proj, proj, dt, p["dtb"], p["alog"], e_mats[1])

    return pl.pallas_call(
        functools.partial(_ssd_fwd_body, lay=lay),
        grid=(nb,),
        in_specs=[
            pl.BlockSpec((rows, SSD_INNER), lambda c: (c, xs_blk)),
            pl.BlockSpec((rows, SSD_BC_WIDTH), lambda c: (c, bc_blk)),
            pl.BlockSpec((rows, LANES), lambda c: (c, 0)),
            pl.BlockSpec((DT_REP, rows), lambda c: (0, c)),
            pl.BlockSpec((rows, SSD_INNER), lambda c: (c, 0)),
            _const_spec((1, LANES)), _const_spec((1, LANES)),
            _const_spec((DT_REP, 1)), _const_spec((DT_REP, 1)),
            _const_spec((1, LANES)),
            _const_spec((LANES, SSD_INNER)), _const_spec((LANES, SSD_INNER)),
        ],
        out_specs=pl.BlockSpec((rows, SSD_INNER), lambda c: (c, 0)),
        out_shape=jax.ShapeDtypeStruct((t, SSD_INNER), BF16),
        scratch_shapes=[state],
        compiler_params=_params(("arbitrary",)),
        name="ssd_fwd",
    )(proj, proj, dt, dtT, yob, p["dtb"], p["alog"], p["dtbT"], p["alogT"], p["dskip"], e_mats[0], e_mats[1])


_ROW_TILE = 512
_MERGE_COLS = 256


def _attn_merge_body(q_ref, kvp_ref, kvm_ref, kvn_ref, bias_ref, sink_ref,
                     xa_ref, xb_ref, gu_ref, gv_ref, yc_ref, z_ref, g0_ref, g1_ref, g2_ref, ng_ref, ws_ref, bs_ref,
                     wa_ref, wb_ref, wc_ref, wo_ref, ha_ref, hb_ref, ya_ref, m_ref, *, lay, n_a, n_tiles):
    i = pl.program_id(0)
    slot = lax.rem(i, 2)

    @pl.when(i == 0)
    def _():
        ya_ref[...] = jnp.zeros(ya_ref.shape, BF16)

    ya = ya_ref[1 - slot]
    gw = GMLP_WIDTH // GMLP_GROUPS
    chunks = []
    for c in range(_ROW_TILE // BLK):
        r = slice(c * BLK, (c + 1) * BLK)
        chunks.append(jnp.concatenate(
            [gu_ref[r, g * gw:(g + 1) * gw].astype(F32)
             * (_dot(ws_ref[g], gv_ref[r, g * gw:(g + 1) * gw]) + bs_ref[:, g * gw:(g + 1) * gw])
             for g in range(GMLP_GROUPS)], axis=1))
    yb = jnp.concatenate(chunks, axis=0).astype(BF16)
    y = yc_ref[...].astype(F32) * z_ref[...].astype(F32)
    sw = SSD_INNER // SSD_GROUPS
    yn = jnp.concatenate([_rms(y[:, g * sw:(g + 1) * sw], ng_ref[:, g * sw:(g + 1) * sw])
                          for g in range(SSD_GROUPS)], axis=1).astype(BF16)

    acc = {}

    def piece(c, k):
        cols = slice(c * _MERGE_COLS, (c + 1) * _MERGE_COLS)

        def run():
            if k == 0:
                acc[c] = g0_ref[:, cols].astype(F32) * _dot(ya, wa_ref[:, cols])
            elif k == 1:
                acc[c] = acc[c] + g1_ref[:, cols].astype(F32) * _dot(yb, wb_ref[:, cols])
            else:
                m_ref[:, cols] = (acc[c] + g2_ref[:, cols].astype(F32) * _dot(yn, wc_ref[:, cols])).astype(BF16)
        return run

    pieces = [piece(c, k) for c in range(D_MODEL // _MERGE_COLS) for k in range(3)]
    n_items = _ATTN_TILE_BLOCKS * ATTN_HEADS // 2
    done = [0]

    def after_item(idx):
        while done[0] < (idx + 1) * len(pieces) // n_items:
            pieces[done[0]]()
            done[0] += 1

    def put_block(jb, val):
        ya_ref[slot, jb * BLK:(jb + 1) * BLK, :] = val

    _attention_tile(jnp.minimum(i, n_tiles - 1), q_ref, kvp_ref, kvm_ref, kvn_ref, bias_ref, sink_ref,
                    put_block, after_item, lay)

    m = m_ref[...]
    in_first = jnp.maximum(i - 1, 0) < n_a

    @pl.when(in_first)
    def _():
        ha_ref[...] = xa_ref[...] + _dot(m, wo_ref[...])

    @pl.when(jnp.logical_not(in_first))
    def _():
        hb_ref[...] = xb_ref[...] + _dot(m, wo_ref[...])


def _attn_merge(xa, xb, yc, proj, bias, p, lay):
    tm = _ROW_TILE
    nt = _ATTN_TILE_BLOCKS
    assert tm == nt * BLK
    assert lay.per_seq_first % nt == 0 and lay.per_seq_second % nt == 0 and lay.n_first % nt == 0
    n_a, n_b = xa.shape[0] // tm, xb.shape[0] // tm
    n_tiles = n_a + n_b
    g_blk = COL_GATE // D_MODEL
    kv_w = 2 * ATTN_KV_WIDTH
    kv_blk = COL_KV // kv_w
    last_blk = lay.n_blocks - 1

    def att(i):
        return jnp.minimum(i, n_tiles - 1)

    def mrg(i):
        return jnp.maximum(i - 1, 0)

    def row(width, col_blk=0):
        return pl.BlockSpec((tm, width), lambda i: (mrg(i), col_blk))

    stream = (pl.BlockSpec((tm, D_MODEL), lambda i: (jnp.minimum(mrg(i), n_a - 1), 0)),
              pl.BlockSpec((tm, D_MODEL), lambda i: (jnp.clip(mrg(i) - n_a, 0, n_b - 1), 0)))
    return pl.pallas_call(
        functools.partial(_attn_merge_body, lay=lay, n_a=n_a, n_tiles=n_tiles),
        grid=(n_tiles + 1,),
        in_specs=[
            pl.BlockSpec((tm, ATTN_Q_WIDTH), lambda i: (att(i), COL_Q // ATTN_Q_WIDTH)),
            pl.BlockSpec((BLK, kv_w), lambda i: (jnp.maximum(att(i) * nt - 1, 0), kv_blk)),
            pl.BlockSpec((tm, kv_w), lambda i: (att(i), kv_blk)),
            pl.BlockSpec((BLK, kv_w), lambda i: (jnp.minimum(att(i) * nt + nt, last_blk), kv_blk)),
            _const_spec((ATTN_HEADS // 2, 3 * BLK, 2 * BLK)),
            _const_spec((ATTN_HEADS // 2, 2 * BLK)),
            *stream, row(GMLP_WIDTH, COL_GU // GMLP_WIDTH), row(GMLP_WIDTH, COL_GV // GMLP_WIDTH),
            row(SSD_INNER), row(SSD_INNER, COL_Z // SSD_INNER),
            row(D_MODEL, g_blk), row(D_MODEL, g_blk + 1), row(D_MODEL, g_blk + 2),
            _const_spec((1, SSD_INNER)),
            _const_spec((GMLP_GROUPS, BLK, BLK)), _const_spec((BLK, GMLP_WIDTH)),
            _const_spec((ATTN_Q_WIDTH, D_MODEL)), _const_spec((GMLP_WIDTH, D_MODEL)),
            _const_spec((SSD_INNER, D_MODEL)), _const_spec((D_MODEL, D_MODEL)),
        ],
        out_specs=list(stream),
        out_shape=[jax.ShapeDtypeStruct(xa.shape, F32), jax.ShapeDtypeStruct(xb.shape, F32)],
        scratch_shapes=[pltpu.VMEM((2, tm, ATTN_Q_WIDTH), BF16), pltpu.VMEM((tm, D_MODEL), BF16)],
        compiler_params=_params(("arbitrary",)),
        name="attn_merge",
    )(proj, proj, proj, proj, bias, p["sink"],
      xa, xb, proj, proj, yc, proj, proj, proj, proj, p["ssd_norm_g"], p["w_s"], p["b_s"],
      p["w_a"], p["w_b"], p["w_c"], p["w_o"])


_FF_CHUNK = 1024


def _ffn_body(ha_ref, hb_ref, g_ref, w1_ref, w2_ref, oa_ref, ob_ref, *, n_a):
    def run(h_ref, o_ref):
        h = h_ref[...]
        hn = _rms(h, g_ref[...]).astype(BF16)
        acc = h
        for c0 in range(0, FF_DIM, _FF_CHUNK):
            a = jnp.maximum(_dot(hn, w1_ref[:, c0:c0 + _FF_CHUNK]), 0.0)
            acc = acc + _dot((a * a).astype(BF16), w2_ref[c0:c0 + _FF_CHUNK, :])
        o_ref[...] = acc

    in_first = pl.program_id(0) < n_a
    pl.when(in_first)(lambda: run(ha_ref, oa_ref))
    pl.when(jnp.logical_not(in_first))(lambda: run(hb_ref, ob_ref))


def _ffn(ha, hb, g, w1, w2):
    tm = _ROW_TILE
    n_a, n_b = ha.shape[0] // tm, hb.shape[0] // tm
    stream = _two_stream_specs(tm, n_a, n_b)
    return pl.pallas_call(
        functools.partial(_ffn_body, n_a=n_a),
        grid=(n_a + n_b,),
        in_specs=[*stream, _const_spec((1, D_MODEL)), _const_spec((D_MODEL, FF_DIM)), _const_spec((FF_DIM, D_MODEL))],
        out_specs=list(stream),
        out_shape=[jax.ShapeDtypeStruct(ha.shape, F32), jax.ShapeDtypeStruct(hb.shape, F32)],
        compiler_params=_params(("arbitrary",)),
        name="ffn",
    )(ha, hb, g, w1, w2)


def _t5_bucket(rel):
    nb = NUM_BUCKETS // 2
    max_exact = nb // 2
    ret = jnp.where(rel > 0, nb, 0)
    n = jnp.abs(rel)
    n_safe = jnp.maximum(n, 1).astype(F32)
    large = max_exact + (jnp.log(n_safe / max_exact) / math.log(MAX_DISTANCE / max_exact)
                         * (nb - max_exact)).astype(jnp.int32)
    large = jnp.minimum(large, nb - 1)
    return ret + jnp.where(n < max_exact, n, large)


def _bias_table(rel_bias):
    qi = jnp.arange(BLK)[:, None]
    ki = jnp.arange(3 * BLK)[None, :]
    rel = ki - BLK - qi
    onehot = (_t5_bucket(rel)[..., None] == jnp.arange(NUM_BUCKETS)).astype(F32)
    bias = jnp.einsum("qkb,bh->hkq", onehot, rel_bias.astype(F32), precision=lax.Precision.HIGHEST) * _LOG2E
    bias = jnp.where((jnp.abs(rel) <= WINDOW).T[None], bias, NEG_INF)
    return jnp.concatenate([bias[0::2], bias[1::2]], axis=-1)


def _expand_matrix(d):
    k = np.arange(LANES)[:, None]
    col = np.arange(SSD_INNER)[None, :]
    hit = ((k % DT_REP) == d * SSD_HEADS + col // SSD_HEAD_DIM)
    return jnp.asarray(hit, dtype=BF16)


def _prep_layer_params(w_in, norm_mix_g, q_norm_g, k_norm_g, attn_sink, gmlp_ln_g, gmlp_ln_b, w_spatial,
                       b_spatial, conv_w, conv_b, dt_bias, a_log, d_skip, ssd_norm_g, w_up_attn, w_up_gmlp,
                       w_up_ssd, w_out, norm_ff_g, w_ff1, w_ff2):
    depth = w_in.shape[0]

    def cols(a, b):
        return w_in[:, :, a:b]

    w_main = jnp.concatenate([
        cols(_R_Z, _R_XS), cols(_R_XS, _R_B), cols(_R_GATE, _R_END), cols(_R_Q, _R_K),
        cols(_R_GU, _R_GV), cols(_R_GV, _R_Z), cols(_R_B, _R_DT), cols(_R_K, _R_GU)], axis=-1).astype(BF16)
    w_dt = cols(_R_DT, _R_GATE)
    rep = LANES // DT_REP
    gw = GMLP_WIDTH // GMLP_GROUPS

    def dt_lanes(v):
        return jnp.tile(v.reshape(depth, 1, DT_REP), (1, 1, rep))

    return dict(
        norm_mix_g=norm_mix_g[:, None, :],
        w_main=w_main,
        w_dt=jnp.tile(w_dt, (1, 1, rep)).astype(BF16),
        w_dtT=jnp.swapaxes(w_dt, 1, 2).astype(BF16),
        kscale=jnp.tile(q_norm_g * k_norm_g * (HEAD_DIM ** -0.5 * _LOG2E), (1, ATTN_KV_HEADS))[:, None, :],
        sink=jnp.repeat(attn_sink * _LOG2E, BLK, axis=-1).reshape(depth, ATTN_HEADS // 2, 2 * BLK),
        ln_g=gmlp_ln_g[:, None, :], ln_b=gmlp_ln_b[:, None, :],
        w_s=w_spatial.astype(BF16),
        b_s=jnp.repeat(jnp.swapaxes(b_spatial, 1, 2), gw, axis=-1),
        conv_w=conv_w, conv_b=conv_b[:, None, :],
        dtb=dt_lanes(dt_bias), alog=dt_lanes(a_log),
        dtbT=dt_bias.reshape(depth, DT_REP, 1), alogT=a_log.reshape(depth, DT_REP, 1),
        dskip=dt_lanes(jnp.concatenate([d_skip, d_skip], axis=-1)),
        ssd_norm_g=ssd_norm_g[:, None, :],
        w_a=w_up_attn.astype(BF16), w_b=w_up_gmlp.astype(BF16), w_c=w_up_ssd.astype(BF16),
        w_o=w_out.astype(BF16),
        norm_ff_g=norm_ff_g[:, None, :],
        w1=w_ff1.astype(BF16), w2=w_ff2.astype(BF16),
    )


def _layer(xa, xb, p, bias, e_mats, lay):
    proj, dt, dtT = _inproj(xa, xb, p, lay)
    yc = _ssd(proj, dt, dtT, p, e_mats, lay)
    ha, hb = _attn_merge(xa, xb, yc, proj, bias, p, lay)
    return _ffn(ha, hb, p["norm_ff_g"], p["w1"], p["w2"])


def _encoder(x_first, x_second, rel_bias, layer_params):
    b1, s1, _ = x_first.shape
    b2, s2, _ = x_second.shape
    t1, t2 = b1 * s1, b2 * s2
    lay = Layout(n_blocks=(t1 + t2) // BLK, n_first=t1 // BLK, per_seq_first=s1 // BLK,
                 per_seq_second=s2 // BLK)
    bias = _bias_table(rel_bias)
    e_mats = (_expand_matrix(0), _expand_matrix(1))
    params = _prep_layer_params(**layer_params)

    xa, xb = x_first.reshape(t1, D_MODEL), x_second.reshape(t2, D_MODEL)
    for layer in range(layer_params["w_in"].shape[0]):
        xa, xb = _layer(xa, xb, {k: v[layer] for k, v in params.items()}, bias, e_mats, lay)
    return xa.reshape(b1, s1, D_MODEL), xb.reshape(b2, s2, D_MODEL)


def kernel(x_prompt, x_sample, rel_bias, norm_mix_g, w_in, q_norm_g, k_norm_g, attn_sink, gmlp_ln_g, gmlp_ln_b,
           w_spatial, b_spatial, conv_w, conv_b, dt_bias, a_log, d_skip, ssd_norm_g, w_up_attn, w_up_gmlp,
           w_up_ssd, w_out, norm_ff_g, w_ff1, w_ff2):
    layer_params = dict(
        w_in=w_in, norm_mix_g=norm_mix_g, q_norm_g=q_norm_g, k_norm_g=k_norm_g, attn_sink=attn_sink,
        gmlp_ln_g=gmlp_ln_g, gmlp_ln_b=gmlp_ln_b, w_spatial=w_spatial, b_spatial=b_spatial, conv_w=conv_w,
        conv_b=conv_b, dt_bias=dt_bias, a_log=a_log, d_skip=d_skip, ssd_norm_g=ssd_norm_g,
        w_up_attn=w_up_attn, w_up_gmlp=w_up_gmlp, w_up_ssd=w_up_ssd, w_out=w_out, norm_ff_g=norm_ff_g,
        w_ff1=w_ff1, w_ff2=w_ff2)
    return _encoder(x_prompt, x_sample, rel_bias, layer_params)
```

```python
import functools
import math
from typing import NamedTuple

import numpy as np
import jax
import jax.numpy as jnp
from jax import lax
from jax.experimental import pallas as pl
from jax.experimental.pallas import tpu as pltpu

F32 = jnp.float32
BF16 = jnp.bfloat16

D_MODEL = 1024
ATTN_HEADS = 8
ATTN_KV_HEADS = 2
HEAD_DIM = 64
ATTN_Q_WIDTH = ATTN_HEADS * HEAD_DIM
ATTN_KV_WIDTH = ATTN_KV_HEADS * HEAD_DIM
WINDOW = 128
NUM_BUCKETS = 32
MAX_DISTANCE = 128
GMLP_WIDTH = 512
GMLP_GROUPS = 4
SSD_INNER = 1024
SSD_HEAD_DIM = 64
SSD_HEADS = SSD_INNER // SSD_HEAD_DIM
SSD_GROUPS = 2
SSD_STATE = 128
N_DIR = 2
FF_DIM = 4 * D_MODEL
EPS = 1e-6
NEG_INF = -1e30

BLK = 128
LANES = 128
BF16_SUBLANES = 16
VMEM_LIMIT = 56 * 1024 * 1024
_LOG2E = math.log2(math.e)

COL_Z = 0
COL_XS = 1024
COL_GATE = 2048
COL_Q = 5120
COL_GU = 5632
COL_GV = 6144
COL_BC = 6656
COL_KV = 7168
P_WIDTH = 7424
DT_REP = 32
SSD_BC_WIDTH = 2 * SSD_GROUPS * SSD_STATE

_R_Q, _R_K, _R_V, _R_GU, _R_GV, _R_Z, _R_XS, _R_B, _R_C, _R_DT, _R_GATE, _R_END = (
    0, 512, 640, 768, 1280, 1792, 2816, 3840, 4096, 4352, 4384, 7456)


class Layout(NamedTuple):
    n_blocks: int
    n_first: int
    per_seq_first: int
    per_seq_second: int


def _seq_pos(blk, lay):
    in_first = blk < lay.n_first
    pos = jnp.where(in_first, lax.rem(blk, lay.per_seq_first),
                    lax.rem(blk - lay.n_first, lay.per_seq_second))
    last = jnp.where(in_first, lay.per_seq_first - 1, lay.per_seq_second - 1)
    return pos == 0, pos == last


def _sigmoid(x):
    return 0.5 * jnp.tanh(0.5 * x) + 0.5


def _silu(x):
    return x * _sigmoid(x)


def _gelu(x):
    return x * (0.5 * (1.0 + jnp.tanh(math.sqrt(2.0 / math.pi) * (x + 0.044715 * (x * x * x)))))


def _softplus(x):
    return jnp.maximum(x, 0.0) + jnp.log1p(jnp.exp(-jnp.abs(x)))


def _rms(x, g):
    return x * lax.rsqrt(jnp.mean(x * x, axis=-1, keepdims=True) + EPS) * g


def _dot(a, b):
    return jnp.dot(a, b, preferred_element_type=F32)


def _dot_nt(a, b):
    return lax.dot_general(a, b, (((1,), (1,)), ((), ())), preferred_element_type=F32)


def _const_spec(shape):
    nd = len(shape)
    return pl.BlockSpec(shape, lambda *_: (0,) * nd)


def _params(semantics):
    return pltpu.CompilerParams(dimension_semantics=semantics, vmem_limit_bytes=VMEM_LIMIT)


_PROJ_TILE = 512
_PROJ_LOOKAHEAD = 1
_X_HALO = 8


def _split2(x):
    hi = x.astype(BF16)
    return hi, (x - hi.astype(F32)).astype(BF16)


def _inproj_body(xa_ref, xb_ref, xa_prev_ref, xb_prev_ref, xa_next_ref, xb_next_ref, g_ref, w_ref, wdt_ref, wdtT_ref,
                 cw_ref, cb_ref, lng_ref, lnb_ref, kscale_ref, qsum_ref, ksum_ref, o_ref, dt_ref, dtT_ref, *, lay):
    tm = _PROJ_TILE
    nblk = tm // BLK
    i = pl.program_id(0)
    keep_prev = jnp.where(_seq_pos(i * nblk, lay)[0], 0.0, 1.0)
    keep_next = jnp.where(_seq_pos(i * nblk + nblk - 1, lay)[1], 0.0, 1.0)
    in_first = i < lay.n_first // nblk

    row_limit = jnp.where(in_first, tm, 0)

    def pick(a_ref, b_ref):
        a = a_ref[...]
        return jnp.where(lax.broadcasted_iota(jnp.int32, a.shape, 0) < row_limit, a, b_ref[...])

    xn = _rms(pick(xa_ref, xb_ref), g_ref[...]).astype(BF16)
    xh = _rms(jnp.concatenate([pick(xa_prev_ref, xb_prev_ref), pick(xa_next_ref, xb_next_ref)], axis=0),
              g_ref[...]).astype(BF16)

    def put(c0, val):
        o_ref[:, c0:c0 + val.shape[1]] = val.astype(BF16)

    def conv_silu(k0):
        def post(c0, y):
            width = y.shape[1]
            yh = _dot(xh, w_ref[:, c0:c0 + width])
            halo_p = yh[_X_HALO - 1:_X_HALO] * keep_prev
            halo_n = yh[_X_HALO:_X_HALO + 1] * keep_next
            up = pltpu.roll(y, 1, axis=0)
            dn = pltpu.roll(y, tm - 1, axis=0)
            r8 = lax.broadcasted_iota(jnp.int32, (8, width), 0)
            y_m1 = jnp.concatenate([jnp.where(r8 == 0, halo_p, up[:8]), up[8:]], axis=0)
            y_p1 = jnp.concatenate([dn[:tm - 8], jnp.where(r8 == 7, halo_n, dn[tm - 8:])], axis=0)
            w = cw_ref[:, k0:k0 + width]
            put(c0, _silu(cb_ref[:, k0:k0 + width] + y_m1 * w[0:1] + y * w[1:2] + y_p1 * w[2:3]))
        return post

    def q_norm(c0, q):
        sq = jnp.concatenate(_split2(q * q), axis=1)

        def tail():
            qss = _dot(sq, qsum_ref[...])
            put(c0, q * lax.rsqrt(qss * (1.0 / HEAD_DIM) + EPS))
        return tail

    def gelu_ln(c0, y):
        vf = _gelu(y)
        vc = vf - jnp.mean(vf, axis=-1, keepdims=True)
        var = jnp.mean(vc * vc, axis=-1, keepdims=True)
        put(c0, vc * lax.rsqrt(var + EPS) * lng_ref[...] + lnb_ref[...])

    def kv_norm(c0, kv):
        kf = kv[:, :ATTN_KV_WIDTH]
        sq = jnp.concatenate(_split2(kf * kf), axis=1)
        put(c0 + ATTN_KV_WIDTH, kv[:, ATTN_KV_WIDTH:])

        def tail():
            kss = _dot(sq, ksum_ref[...])
            put(c0, kf * lax.rsqrt(kss * (1.0 / HEAD_DIM) + EPS) * kscale_ref[...])
        return tail

    def gate(k):
        return (COL_GATE + k * D_MODEL, D_MODEL, lambda c0, y: put(c0, _sigmoid(y)))

    items = [
        (COL_XS, SSD_INNER, conv_silu(0)),
        (COL_Z, SSD_INNER, lambda c0, y: put(c0, _silu(y))),
        (COL_BC, SSD_BC_WIDTH, conv_silu(SSD_INNER)),
        gate(0),
        (COL_GV, GMLP_WIDTH, gelu_ln),
        gate(1),
        (COL_GU, GMLP_WIDTH, lambda c0, y: put(c0, _gelu(y))),
        gate(2),
        (COL_Q, ATTN_Q_WIDTH, q_norm),
        (COL_KV, 2 * ATTN_KV_WIDTH, kv_norm),
    ]

    def proj(k):
        c0, width, _ = items[k]
        return _dot(xn, w_ref[:, c0:c0 + width])

    pending = [proj(k) for k in range(_PROJ_LOOKAHEAD)]
    tail = None
    for k, (c0, _, post) in enumerate(items):
        y = pending.pop(0)
        if k + _PROJ_LOOKAHEAD < len(items):
            pending.append(proj(k + _PROJ_LOOKAHEAD))
        if tail is not None:
            tail()
        tail = post(c0, y)
    dt_ref[...] = _dot(xn, wdt_ref[...])
    dtT_ref[...] = _dot_nt(wdtT_ref[...], xn)
    if tail is not None:
        tail()


def _head_sum_matrix(width):
    k = np.arange(2 * width)[:, None]
    n = np.arange(width)[None, :]
    return jnp.asarray((k % width) // HEAD_DIM == n // HEAD_DIM, dtype=BF16)


def _two_stream_specs(tm, n_a, n_b):
    return (pl.BlockSpec((tm, D_MODEL), lambda i: (jnp.minimum(i, n_a - 1), 0)),
            pl.BlockSpec((tm, D_MODEL), lambda i: (jnp.clip(i - n_a, 0, n_b - 1), 0)))


def _inproj(xa, xb, p, lay):
    tm = _PROJ_TILE
    n_a, n_b = xa.shape[0] // tm, xb.shape[0] // tm
    t = xa.shape[0] + xb.shape[0]
    assert lay.per_seq_first % (tm // BLK) == 0 and lay.per_seq_second % (tm // BLK) == 0
    hal = tm // _X_HALO
    conv_w = SSD_INNER + SSD_BC_WIDTH

    def halo(tile0, n_tiles, shift):
        last = n_tiles * hal - 1
        return pl.BlockSpec((_X_HALO, D_MODEL),
                            lambda i: (jnp.clip((i - tile0 + shift) * hal - 1 + shift, 0, last), 0))

    return pl.pallas_call(
        functools.partial(_inproj_body, lay=lay),
        grid=(n_a + n_b,),
        in_specs=[
            *_two_stream_specs(tm, n_a, n_b),
            halo(0, n_a, 0), halo(n_a, n_b, 0), halo(0, n_a, 1), halo(n_a, n_b, 1),
            _const_spec((1, D_MODEL)),
            _const_spec((D_MODEL, P_WIDTH)),
            _const_spec((D_MODEL, LANES)),
            _const_spec((DT_REP, D_MODEL)),
            _const_spec((3, conv_w)), _const_spec((1, conv_w)),
            _const_spec((1, GMLP_WIDTH)), _const_spec((1, GMLP_WIDTH)),
            _const_spec((1, ATTN_KV_WIDTH)),
            _const_spec((2 * ATTN_Q_WIDTH, ATTN_Q_WIDTH)),
            _const_spec((2 * ATTN_KV_WIDTH, ATTN_KV_WIDTH)),
        ],
        out_specs=[
            pl.BlockSpec((tm, P_WIDTH), lambda i: (i, 0)),
            pl.BlockSpec((tm, LANES), lambda i: (i, 0)),
            pl.BlockSpec((DT_REP, tm), lambda i: (0, i)),
        ],
        out_shape=[
            jax.ShapeDtypeStruct((t, P_WIDTH), BF16),
            jax.ShapeDtypeStruct((t, LANES), F32),
            jax.ShapeDtypeStruct((DT_REP, t), F32),
        ],
        compiler_params=_params(("parallel",)),
        name="inproj",
    )(xa, xb, xa, xb, xa, xb, p["norm_mix_g"], p["w_main"], p["w_dt"], p["w_dtT"], p["conv_w"], p["conv_b"], p["ln_g"], p["ln_b"],
      p["kscale"], _head_sum_matrix(ATTN_Q_WIDTH), _head_sum_matrix(ATTN_KV_WIDTH))


_ATTN_TILE_BLOCKS = 4
_PV_DELAY = 1
_ATTN_LOOKAHEAD = 4


def _attention_tile(tile, q_ref, kvp_ref, kvm_ref, kvn_ref, bias_ref, sink_ref, put_block, after_item, lay):
    nt = _ATTN_TILE_BLOCKS
    grp = ATTN_HEADS // ATTN_KV_HEADS
    kv = jnp.concatenate([kvp_ref[...], kvm_ref[...], kvn_ref[...]], axis=0)
    kn = [kv[:, g * HEAD_DIM:(g + 1) * HEAD_DIM] for g in range(ATTN_KV_HEADS)]
    vT = jnp.transpose(kv[:, ATTN_KV_WIDTH:].astype(F32)).astype(BF16)
    neg_prev = jnp.where(_seq_pos(tile * nt, lay)[0], NEG_INF, 0.0)
    neg_next = jnp.where(_seq_pos(tile * nt + nt - 1, lay)[1], NEG_INF, 0.0)

    qbs = [q_ref[jb * BLK:(jb + 1) * BLK, :] for jb in range(nt)]

    def scores(jb, hp):
        slab = qbs[jb][:, hp * LANES:(hp + 1) * LANES]
        q2 = jnp.concatenate([slab[:, :HEAD_DIM], slab[:, HEAD_DIM:]], axis=0)
        return _dot_nt(kn[2 * hp // grp][jb * BLK:(jb + 3) * BLK], q2) + bias_ref[hp]

    n_pairs = ATTN_HEADS // 2
    items = [(jb, hp) for jb in range(nt) for hp in range(n_pairs)]
    pending = [scores(*it) for it in items[:_ATTN_LOOKAHEAD]]
    outs = []
    waiting = []

    def value_matmul():
        jb, hp, p, inv = waiting.pop(0)
        g = 2 * hp // grp
        o = _dot(vT[g * HEAD_DIM:(g + 1) * HEAD_DIM, jb * BLK:(jb + 3) * BLK], p) * inv
        outs.extend([o[:, :BLK], o[:, BLK:]])
        if hp == n_pairs - 1:
            put_block(jb, jnp.transpose(jnp.concatenate(outs, axis=0)).astype(BF16))
            outs.clear()

    for idx, (jb, hp) in enumerate(items):
        s = pending.pop(0)
        if idx + _ATTN_LOOKAHEAD < len(items):
            pending.append(scores(*items[idx + _ATTN_LOOKAHEAD]))
        if len(waiting) > _PV_DELAY:
            value_matmul()
        if jb == 0:
            s = jnp.concatenate([s[:BLK] + neg_prev, s[BLK:]], axis=0)
        if jb == nt - 1:
            s = jnp.concatenate([s[:2 * BLK], s[2 * BLK:] + neg_next], axis=0)
        snk = sink_ref[hp:hp + 1, :]
        m = jnp.maximum(jnp.max(s, axis=0, keepdims=True), snk)
        p = jnp.exp2(s - m)
        denom = jnp.sum(p, axis=0, keepdims=True) + jnp.exp2(snk - m)
        waiting.append((jb, hp, p.astype(BF16), 1.0 / denom))
        after_item(idx)
    while waiting:
        value_matmul()


_SPLIT_TERMS = 4


def _lane_split(x, terms=_SPLIT_TERMS):
    rep = lax.broadcasted_iota(jnp.int32, x.shape, 1) // DT_REP
    out = jnp.zeros(x.shape, BF16)
    rem = x
    for r in range(terms):
        term = rem.astype(BF16)
        out = jnp.where(rep == r, term, out)
        rem = rem - term.astype(F32)
    return out


def _sublane_split(x):
    terms, rem = [], x
    for _ in range(_SPLIT_TERMS):
        term = rem.astype(BF16)
        terms.append(term)
        rem = rem - term.astype(F32)
    return jnp.concatenate(terms, axis=0)


_EXPAND_TERMS = 2


def _expand(x, e_ref):
    return _dot(_lane_split(x, _EXPAND_TERMS), e_ref[...])


def _tri(upper):
    r = lax.broadcasted_iota(jnp.int32, (BLK, BLK), 0)
    c = lax.broadcasted_iota(jnp.int32, (BLK, BLK), 1)
    return (c >= r) if upper else (c <= r)


def _ones_where(mask):
    return jnp.where(mask, 1.0, 0.0).astype(BF16)


_SSD_TILE_CHUNKS = 4


def _rows(j):
    return slice(j * BLK, (j + 1) * BLK)


def _dt_row_major(dt_raw, dtb, alog):
    L = BLK
    dt = _softplus(dt_raw + dtb)
    sp = _lane_split(dt * (-jnp.exp(alog)))
    lane = lax.broadcasted_iota(jnp.int32, (L, LANES), 1)
    is_fwd = (lane & (DT_REP - 1)) < SSD_HEADS
    part = jnp.where(is_fwd, _dot(_ones_where(_tri(False)), sp), _dot(_ones_where(_tri(True)), sp))
    acum = part
    for r in range(1, _SPLIT_TERMS):
        acum = acum + pltpu.roll(part, r * DT_REP, axis=1)
    a_end = jnp.where(is_fwd[0:1], acum[L - 1:L], acum[0:1])
    return dt, acum, a_end


def _dt_head_major(dtT_raw, dtbT, alogT):
    nh = SSD_HEADS
    dtT = _softplus(dtT_raw + dtbT)
    spT = _sublane_split(dtT * (-jnp.exp(alogT)))
    pf = _dot(spT, _ones_where(_tri(True)))
    pb = _dot(spT, _ones_where(_tri(False)))
    acumT_f = pf[0:nh]
    acumT_b = pb[nh:2 * nh]
    for r in range(1, _SPLIT_TERMS):
        acumT_f = acumT_f + pf[r * DT_REP:r * DT_REP + nh]
        acumT_b = acumT_b + pb[r * DT_REP + nh:(r + 1) * DT_REP]
    log2_dtT = jnp.log(dtT) * _LOG2E
    return acumT_f * _LOG2E - log2_dtT[0:nh], acumT_b * _LOG2E - log2_dtT[nh:2 * nh]


def _state_factors(dt, acum, a_end, e_ref):
    ea_exp = _expand(jnp.exp(acum), e_ref)
    w_exp = _expand(dt * jnp.exp(a_end - acum), e_ref)
    cd_exp = _expand(jnp.broadcast_to(jnp.exp(a_end), (8, LANES)), e_ref)[0:1]
    return ea_exp, w_exp, cd_exp


def _state_sweep(st_ref, order, xs, bm, cm_bf, fac):
    gw = SSD_INNER // SSD_GROUPS
    upd = {}
    for j in order:
        wx = (xs[j] * fac[j][1]).astype(BF16)
        upd[j] = [_dot(jnp.transpose(bm[j][g]).astype(BF16), wx[:, g * gw:(g + 1) * gw])
                  for g in range(SSD_GROUPS)]
    sg = [st_ref[g] for g in range(SSD_GROUPS)]
    out = {}
    for j in order:
        y_off = [_dot(cm_bf[j][g], sg[g].astype(BF16)) for g in range(SSD_GROUPS)]
        neg_cd = -fac[j][2]
        sg = [upd[j][g] - sg[g] * neg_cd[:, g * gw:(g + 1) * gw] for g in range(SSD_GROUPS)]
        out[j] = jnp.concatenate(y_off, axis=1) * fac[j][0]
    for g in range(SSD_GROUPS):
        st_ref[g] = sg[g]
    return out


def _split_bc(bc):
    n = SSD_STATE
    return ([bc[:, g * n:(g + 1) * n] for g in range(SSD_GROUPS)],
            [bc[:, (SSD_GROUPS + g) * n:(SSD_GROUPS + g + 1) * n] for g in range(SSD_GROUPS)])


def _ssd_bwd_sweep(xs_ref, bc_ref, dt_ref, dtb_ref, alog_ref, eb_ref, yo_ref, st_ref):
    order = list(reversed(range(_SSD_TILE_CHUNKS)))
    rm = {j: _dt_row_major(dt_ref[_rows(j), :], dtb_ref[...], alog_ref[...]) for j in order}
    fac = {j: _state_factors(*rm[j], eb_ref) for j in order}
    xs = {j: xs_ref[_rows(j), :].astype(F32) for j in order}
    bm = {j: _split_bc(bc_ref[_rows(j), :].astype(F32))[0] for j in order}
    cm_bf = {j: _split_bc(bc_ref[_rows(j), :])[1] for j in order}
    y_off = _state_sweep(st_ref, order, xs, bm, cm_bf, fac)
    for j in order:
        yo_ref[_rows(j), :] = y_off[j].astype(BF16)


def _ssd_body(xsf_ref, bcf_ref, dtf_ref, dtTf_ref, xsb_ref, bcb_ref, dtb_ref, bias_ref, alog_ref, biasT_ref, alogT_ref,
              dskip_ref, ef_ref, eb_ref, y_ref, yob_ref, stf_ref, stb_ref, *, lay):
    ns = _SSD_TILE_CHUNKS
    i = pl.program_id(0)
    n_steps = lay.n_blocks // ns

    @pl.when(_seq_pos(i * ns, lay)[0])
    def _():
        stf_ref[...] = jnp.zeros(stf_ref.shape, F32)

    @pl.when(_seq_pos((n_steps - 1 - i) * ns + ns - 1, lay)[1])
    def _():
        stb_ref[...] = jnp.zeros(stb_ref.shape, F32)

    L, nh = BLK, SSD_HEADS
    hpg = nh // SSD_GROUPS
    order = list(range(ns))
    lane = lax.broadcasted_iota(jnp.int32, (L, LANES), 1)
    xs_bf = {j: xsf_ref[_rows(j), :] for j in order}
    bc_bf = {j: bcf_ref[_rows(j), :] for j in order}
    bm_bf = {j: _split_bc(bc_bf[j])[0] for j in order}
    cm_bf = {j: _split_bc(bc_bf[j])[1] for j in order}
    cb = {j: [_dot_nt(cm_bf[j][g], bm_bf[j][g]) for g in range(SSD_GROUPS)] for j in order}
    rm = {j: _dt_row_major(dtf_ref[_rows(j), :], bias_ref[...], alog_ref[...]) for j in order}
    hm = {j: _dt_head_major(dtTf_ref[:, _rows(j)], biasT_ref[...], alogT_ref[...]) for j in order}

    _ssd_bwd_sweep(xsb_ref, bcb_ref, dtb_ref, bias_ref, alog_ref, eb_ref, yob_ref, stb_ref)

    fac = {j: _state_factors(*rm[j], ef_ref) for j in order}
    xs = {j: xs_bf[j].astype(F32) for j in order}
    bm, coef_exp = {}, {}
    for j in order:
        bm[j], cm = _split_bc(bc_bf[j].astype(F32))
        cbd = jnp.where((lane & (nh - 1)) < hpg, jnp.sum(cm[0] * bm[j][0], axis=1, keepdims=True),
                        jnp.sum(cm[1] * bm[j][1], axis=1, keepdims=True))
        coef_exp[j] = _expand(dskip_ref[...] + cbd * rm[j][0], eb_ref)
    y_off = _state_sweep(stf_ref, order, xs, bm, cm_bf, fac)

    causal = _tri(False)
    left = lane < SSD_HEAD_DIM
    zero_bf = jnp.zeros((L, LANES), BF16)
    acum2 = {j: rm[j][1] * _LOG2E for j in order}
    y_loc = {j: [] for j in order}
    for k in range(nh // 2):
        g = (2 * k) // hpg
        for j in order:
            lhs = []
            for h in (2 * k, 2 * k + 1):
                col_f = jnp.broadcast_to(acum2[j][:, h:h + 1], (L, L))
                col_b = jnp.broadcast_to(acum2[j][:, nh + h:nh + h + 1], (L, L))
                e = jnp.where(causal, col_f - hm[j][0][h:h + 1, :], col_b - hm[j][1][h:h + 1, :])
                lhs.append((cb[j][g] * jnp.exp2(e)).astype(BF16))
            xs_pair = xs_bf[j][:, k * LANES:(k + 1) * LANES]
            rhs = jnp.concatenate([jnp.where(left, xs_pair, zero_bf), jnp.where(left, zero_bf, xs_pair)], axis=0)
            y_loc[j].append(_dot(jnp.concatenate(lhs, axis=1), rhs))
    for j in order:
        y = jnp.concatenate(y_loc[j], axis=1) + coef_exp[j] * xs[j] + y_off[j]
        y_ref[_rows(j), :] = y.astype(BF16)


def _ssd(proj, dt, dtT, p, e_mats, lay):
    t = proj.shape[0]
    ns = _SSD_TILE_CHUNKS
    assert lay.per_seq_first % ns == 0 and lay.per_seq_second % ns == 0 and lay.n_first % ns == 0
    nb = lay.n_blocks // ns
    rows = ns * BLK
    xs_blk, bc_blk = COL_XS // SSD_INNER, COL_BC // SSD_BC_WIDTH
    state = pltpu.VMEM((SSD_GROUPS, SSD_STATE, SSD_INNER // SSD_GROUPS), F32)

    def rev(c):
        return nb - 1 - c

    return pl.pallas_call(
        functools.partial(_ssd_body, lay=lay),
        grid=(nb,),
        in_specs=[
            pl.BlockSpec((rows, SSD_INNER), lambda c: (c, xs_blk)),
            pl.BlockSpec((rows, SSD_BC_WIDTH), lambda c: (c, bc_blk)),
            pl.BlockSpec((rows, LANES), lambda c: (c, 0)),
            pl.BlockSpec((DT_REP, rows), lambda c: (0, c)),
            pl.BlockSpec((rows, SSD_INNER), lambda c: (rev(c), xs_blk)),
            pl.BlockSpec((rows, SSD_BC_WIDTH), lambda c: (rev(c), bc_blk)),
            pl.BlockSpec((rows, LANES), lambda c: (rev(c), 0)),
            _const_spec((1, LANES)), _const_spec((1, LANES)),
            _const_spec((DT_REP, 1)), _const_spec((DT_REP, 1)),
            _const_spec((1, LANES)),
            _const_spec((LANES, SSD_INNER)), _const_spec((LANES, SSD_INNER)),
        ],
        out_specs=[
            pl.BlockSpec((rows, SSD_INNER), lambda c: (c, 0)),
            pl.BlockSpec((rows, SSD_INNER), lambda c: (rev(c), 0)),
        ],
        out_shape=[jax.ShapeDtypeStruct((t, SSD_INNER), BF16), jax.ShapeDtypeStruct((t, SSD_INNER), BF16)],
        scratch_shapes=[state, state],
        compiler_params=_params(("arbitrary",)),
        name="ssd",
    )(proj, proj, dt, dtT, proj, proj, dt, p["dtb"], p["alog"], p["dtbT"], p["alogT"], p["dskip"],
      e_mats[0], e_mats[1])


_ROW_TILE = 512
_MERGE_COLS = 256


def _attn_merge_body(q_ref, kvp_ref, kvm_ref, kvn_ref, bias_ref, sink_ref,
                     xa_ref, xb_ref, gu_ref, gv_ref, yc_ref, yob_ref, z_ref, g0_ref, g1_ref, g2_ref, ng_ref, ws_ref, bs_ref,
                     wa_ref, wb_ref, wc_ref, wo_ref, ha_ref, hb_ref, ya_ref, m_ref, *, lay, n_a, n_tiles):
    i = pl.program_id(0)
    slot = lax.rem(i, 2)

    @pl.when(i == 0)
    def _():
        ya_ref[...] = jnp.zeros(ya_ref.shape, BF16)

    ya = ya_ref[1 - slot]
    gw = GMLP_WIDTH // GMLP_GROUPS
    chunks = []
    for c in range(_ROW_TILE // BLK):
        r = slice(c * BLK, (c + 1) * BLK)
        chunks.append(jnp.concatenate(
            [gu_ref[r, g * gw:(g + 1) * gw].astype(F32)
             * (_dot(ws_ref[g], gv_ref[r, g * gw:(g + 1) * gw]) + bs_ref[:, g * gw:(g + 1) * gw])
             for g in range(GMLP_GROUPS)], axis=1))
    yb = jnp.concatenate(chunks, axis=0).astype(BF16)
    y = (yc_ref[...].astype(F32) + yob_ref[...].astype(F32)) * z_ref[...].astype(F32)
    sw = SSD_INNER // SSD_GROUPS
    yn = jnp.concatenate([_rms(y[:, g * sw:(g + 1) * sw], ng_ref[:, g * sw:(g + 1) * sw])
                          for g in range(SSD_GROUPS)], axis=1).astype(BF16)

    acc = {}

    def piece(c, k):
        cols = slice(c * _MERGE_COLS, (c + 1) * _MERGE_COLS)

        def run():
            if k == 0:
                acc[c] = g0_ref[:, cols].astype(F32) * _dot(ya, wa_ref[:, cols])
            elif k == 1:
                acc[c] = acc[c] + g1_ref[:, cols].astype(F32) * _dot(yb, wb_ref[:, cols])
            else:
                m_ref[:, cols] = (acc[c] + g2_ref[:, cols].astype(F32) * _dot(yn, wc_ref[:, cols])).astype(BF16)
        return run

    pieces = [piece(c, k) for c in range(D_MODEL // _MERGE_COLS) for k in range(3)]
    n_items = _ATTN_TILE_BLOCKS * ATTN_HEADS // 2
    done = [0]

    def after_item(idx):
        while done[0] < (idx + 1) * len(pieces) // n_items:
            pieces[done[0]]()
            done[0] += 1

    def put_block(jb, val):
        ya_ref[slot, jb * BLK:(jb + 1) * BLK, :] = val

    _attention_tile(jnp.minimum(i, n_tiles - 1), q_ref, kvp_ref, kvm_ref, kvn_ref, bias_ref, sink_ref,
                    put_block, after_item, lay)

    m = m_ref[...]
    in_first = jnp.maximum(i - 1, 0) < n_a

    @pl.when(in_first)
    def _():
        ha_ref[...] = xa_ref[...] + _dot(m, wo_ref[...])

    @pl.when(jnp.logical_not(in_first))
    def _():
        hb_ref[...] = xb_ref[...] + _dot(m, wo_ref[...])


def _attn_merge(xa, xb, yc, yob, proj, bias, p, lay):
    tm = _ROW_TILE
    nt = _ATTN_TILE_BLOCKS
    assert tm == nt * BLK
    assert lay.per_seq_first % nt == 0 and lay.per_seq_second % nt == 0 and lay.n_first % nt == 0
    n_a, n_b = xa.shape[0] // tm, xb.shape[0] // tm
    n_tiles = n_a + n_b
    g_blk = COL_GATE // D_MODEL
    kv_w = 2 * ATTN_KV_WIDTH
    kv_blk = COL_KV // kv_w
    last_blk = lay.n_blocks - 1

    def att(i):
        return jnp.minimum(i, n_tiles - 1)

    def mrg(i):
        return jnp.maximum(i - 1, 0)

    def row(width, col_blk=0):
        return pl.BlockSpec((tm, width), lambda i: (mrg(i), col_blk))

    stream = (pl.BlockSpec((tm, D_MODEL), lambda i: (jnp.minimum(mrg(i), n_a - 1), 0)),
              pl.BlockSpec((tm, D_MODEL), lambda i: (jnp.clip(mrg(i) - n_a, 0, n_b - 1), 0)))
    return pl.pallas_call(
        functools.partial(_attn_merge_body, lay=lay, n_a=n_a, n_tiles=n_tiles),
        grid=(n_tiles + 1,),
        in_specs=[
            pl.BlockSpec((tm, ATTN_Q_WIDTH), lambda i: (att(i), COL_Q // ATTN_Q_WIDTH)),
            pl.BlockSpec((BLK, kv_w), lambda i: (jnp.maximum(att(i) * nt - 1, 0), kv_blk)),
            pl.BlockSpec((tm, kv_w), lambda i: (att(i), kv_blk)),
            pl.BlockSpec((BLK, kv_w), lambda i: (jnp.minimum(att(i) * nt + nt, last_blk), kv_blk)),
            _const_spec((ATTN_HEADS // 2, 3 * BLK, 2 * BLK)),
            _const_spec((ATTN_HEADS // 2, 2 * BLK)),
            *stream, row(GMLP_WIDTH, COL_GU // GMLP_WIDTH), row(GMLP_WIDTH, COL_GV // GMLP_WIDTH),
            row(SSD_INNER), row(SSD_INNER), row(SSD_INNER, COL_Z // SSD_INNER),
            row(D_MODEL, g_blk), row(D_MODEL, g_blk + 1), row(D_MODEL, g_blk + 2),
            _const_spec((1, SSD_INNER)),
            _const_spec((GMLP_GROUPS, BLK, BLK)), _const_spec((BLK, GMLP_WIDTH)),
            _const_spec((ATTN_Q_WIDTH, D_MODEL)), _const_spec((GMLP_WIDTH, D_MODEL)),
            _const_spec((SSD_INNER, D_MODEL)), _const_spec((D_MODEL, D_MODEL)),
        ],
        out_specs=list(stream),
        out_shape=[jax.ShapeDtypeStruct(xa.shape, F32), jax.ShapeDtypeStruct(xb.shape, F32)],
        scratch_shapes=[pltpu.VMEM((2, tm, ATTN_Q_WIDTH), BF16), pltpu.VMEM((tm, D_MODEL), BF16)],
        compiler_params=_params(("arbitrary",)),
        name="attn_merge",
    )(proj, proj, proj, proj, bias, p["sink"],
      xa, xb, proj, proj, yc, yob, proj, proj, proj, proj, p["ssd_norm_g"], p["w_s"], p["b_s"],
      p["w_a"], p["w_b"], p["w_c"], p["w_o"])


_FF_CHUNK = 1024


def _ffn_body(ha_ref, hb_ref, g_ref, w1_ref, w2_ref, oa_ref, ob_ref, *, n_a):
    def run(h_ref, o_ref):
        h = h_ref[...]
        hn = _rms(h, g_ref[...]).astype(BF16)
        acc = h
        for c0 in range(0, FF_DIM, _FF_CHUNK):
            a = jnp.maximum(_dot(hn, w1_ref[:, c0:c0 + _FF_CHUNK]), 0.0)
            acc = acc + _dot((a * a).astype(BF16), w2_ref[c0:c0 + _FF_CHUNK, :])
        o_ref[...] = acc

    in_first = pl.program_id(0) < n_a
    pl.when(in_first)(lambda: run(ha_ref, oa_ref))
    pl.when(jnp.logical_not(in_first))(lambda: run(hb_ref, ob_ref))


def _ffn(ha, hb, g, w1, w2):
    tm = _ROW_TILE
    n_a, n_b = ha.shape[0] // tm, hb.shape[0] // tm
    stream = _two_stream_specs(tm, n_a, n_b)
    return pl.pallas_call(
        functools.partial(_ffn_body, n_a=n_a),
        grid=(n_a + n_b,),
        in_specs=[*stream, _const_spec((1, D_MODEL)), _const_spec((D_MODEL, FF_DIM)), _const_spec((FF_DIM, D_MODEL))],
        out_specs=list(stream),
        out_shape=[jax.ShapeDtypeStruct(ha.shape, F32), jax.ShapeDtypeStruct(hb.shape, F32)],
        compiler_params=_params(("arbitrary",)),
        name="ffn",
    )(ha, hb, g, w1, w2)


def _t5_bucket(rel):
    nb = NUM_BUCKETS // 2
    max_exact = nb // 2
    ret = jnp.where(rel > 0, nb, 0)
    n = jnp.abs(rel)
    n_safe = jnp.maximum(n, 1).astype(F32)
    large = max_exact + (jnp.log(n_safe / max_exact) / math.log(MAX_DISTANCE / max_exact)
                         * (nb - max_exact)).astype(jnp.int32)
    large = jnp.minimum(large, nb - 1)
    return ret + jnp.where(n < max_exact, n, large)


def _bias_table(rel_bias):
    qi = jnp.arange(BLK)[:, None]
    ki = jnp.arange(3 * BLK)[None, :]
    rel = ki - BLK - qi
    onehot = (_t5_bucket(rel)[..., None] == jnp.arange(NUM_BUCKETS)).astype(F32)
    bias = jnp.einsum("qkb,bh->hkq", onehot, rel_bias.astype(F32), precision=lax.Precision.HIGHEST) * _LOG2E
    bias = jnp.where((jnp.abs(rel) <= WINDOW).T[None], bias, NEG_INF)
    return jnp.concatenate([bias[0::2], bias[1::2]], axis=-1)


def _expand_matrix(d):
    k = np.arange(LANES)[:, None]
    col = np.arange(SSD_INNER)[None, :]
    hit = ((k % DT_REP) == d * SSD_HEADS + col // SSD_HEAD_DIM)
    return jnp.asarray(hit, dtype=BF16)


def _prep_layer_params(w_in, norm_mix_g, q_norm_g, k_norm_g, attn_sink, gmlp_ln_g, gmlp_ln_b, w_spatial,
                       b_spatial, conv_w, conv_b, dt_bias, a_log, d_skip, ssd_norm_g, w_up_attn, w_up_gmlp,
                       w_up_ssd, w_out, norm_ff_g, w_ff1, w_ff2):
    depth = w_in.shape[0]

    def cols(a, b):
        return w_in[:, :, a:b]

    w_main = jnp.concatenate([
        cols(_R_Z, _R_XS), cols(_R_XS, _R_B), cols(_R_GATE, _R_END), cols(_R_Q, _R_K),
        cols(_R_GU, _R_GV), cols(_R_GV, _R_Z), cols(_R_B, _R_DT), cols(_R_K, _R_GU)], axis=-1).astype(BF16)
    w_dt = cols(_R_DT, _R_GATE)
    rep = LANES // DT_REP
    gw = GMLP_WIDTH // GMLP_GROUPS

    def dt_lanes(v):
        return jnp.tile(v.reshape(depth, 1, DT_REP), (1, 1, rep))

    return dict(
        norm_mix_g=norm_mix_g[:, None, :],
        w_main=w_main,
        w_dt=jnp.tile(w_dt, (1, 1, rep)).astype(BF16),
        w_dtT=jnp.swapaxes(w_dt, 1, 2).astype(BF16),
        kscale=jnp.tile(q_norm_g * k_norm_g * (HEAD_DIM ** -0.5 * _LOG2E), (1, ATTN_KV_HEADS))[:, None, :],
        sink=jnp.repeat(attn_sink * _LOG2E, BLK, axis=-1).reshape(depth, ATTN_HEADS // 2, 2 * BLK),
        ln_g=gmlp_ln_g[:, None, :], ln_b=gmlp_ln_b[:, None, :],
        w_s=w_spatial.astype(BF16),
        b_s=jnp.repeat(jnp.swapaxes(b_spatial, 1, 2), gw, axis=-1),
        conv_w=conv_w, conv_b=conv_b[:, None, :],
        dtb=dt_lanes(dt_bias), alog=dt_lanes(a_log),
        dtbT=dt_bias.reshape(depth, DT_REP, 1), alogT=a_log.reshape(depth, DT_REP, 1),
        dskip=dt_lanes(jnp.concatenate([d_skip, d_skip], axis=-1)),
        ssd_norm_g=ssd_norm_g[:, None, :],
        w_a=w_up_attn.astype(BF16), w_b=w_up_gmlp.astype(BF16), w_c=w_up_ssd.astype(BF16),
        w_o=w_out.astype(BF16),
        norm_ff_g=norm_ff_g[:, None, :],
        w1=w_ff1.astype(BF16), w2=w_ff2.astype(BF16),
    )


def _layer(xa, xb, p, bias, e_mats, lay):
    proj, dt, dtT = _inproj(xa, xb, p, lay)
    yc, yob = _ssd(proj, dt, dtT, p, e_mats, lay)
    ha, hb = _attn_merge(xa, xb, yc, yob, proj, bias, p, lay)
    return _ffn(ha, hb, p["norm_ff_g"], p["w1"], p["w2"])


def _encoder(x_first, x_second, rel_bias, layer_params):
    b1, s1, _ = x_first.shape
    b2, s2, _ = x_second.shape
    t1, t2 = b1 * s1, b2 * s2
    lay = Layout(n_blocks=(t1 + t2) // BLK, n_first=t1 // BLK, per_seq_first=s1 // BLK,
                 per_seq_second=s2 // BLK)
    bias = _bias_table(rel_bias)
    e_mats = (_expand_matrix(0), _expand_matrix(1))
    params = _prep_layer_params(**layer_params)

    xa, xb = x_first.reshape(t1, D_MODEL), x_second.reshape(t2, D_MODEL)
    for layer in range(layer_params["w_in"].shape[0]):
        xa, xb = _layer(xa, xb, {k: v[layer] for k, v in params.items()}, bias, e_mats, lay)
    return xa.reshape(b1, s1, D_MODEL), xb.reshape(b2, s2, D_MODEL)


def kernel(x_prompt, x_sample, rel_bias, norm_mix_g, w_in, q_norm_g, k_norm_g, attn_sink, gmlp_ln_g, gmlp_ln_b,
           w_spatial, b_spatial, conv_w, conv_b, dt_bias, a_log, d_skip, ssd_norm_g, w_up_attn, w_up_gmlp,
           w_up_ssd, w_out, norm_ff_g, w_ff1, w_ff2):
    layer_params = dict(
        w_in=w_in, norm_mix_g=norm_mix_g, q_norm_g=q_norm_g, k_norm_g=k_norm_g, attn_sink=attn_sink,
        gmlp_ln_g=gmlp_ln_g, gmlp_ln_b=gmlp_ln_b, w_spatial=w_spatial, b_spatial=b_spatial, conv_w=conv_w,
        conv_b=conv_b, dt_bias=dt_bias, a_log=a_log, d_skip=d_skip, ssd_norm_g=ssd_norm_g,
        w_up_attn=w_up_attn, w_up_gmlp=w_up_gmlp, w_up_ssd=w_up_ssd, w_out=w_out, norm_ff_g=norm_ff_g,
        w_ff1=w_ff1, w_ff2=w_ff2)
    return _encoder(x_prompt, x_sample, rel_bias, layer_params)
```

```python
import functools
import math
from typing import NamedTuple

import numpy as np
import jax
import jax.numpy as jnp
from jax import lax
from jax.experimental import pallas as pl
from jax.experimental.pallas import tpu as pltpu

F32 = jnp.float32
BF16 = jnp.bfloat16

D_MODEL = 1024
ATTN_HEADS = 8
ATTN_KV_HEADS = 2
HEAD_DIM = 64
ATTN_Q_WIDTH = ATTN_HEADS * HEAD_DIM
ATTN_KV_WIDTH = ATTN_KV_HEADS * HEAD_DIM
WINDOW = 128
NUM_BUCKETS = 32
MAX_DISTANCE = 128
GMLP_WIDTH = 512
GMLP_GROUPS = 4
SSD_INNER = 1024
SSD_HEAD_DIM = 64
SSD_HEADS = SSD_INNER // SSD_HEAD_DIM
SSD_GROUPS = 2
SSD_STATE = 128
N_DIR = 2
FF_DIM = 4 * D_MODEL
EPS = 1e-6
NEG_INF = -1e30

BLK = 128
LANES = 128
BF16_SUBLANES = 16
VMEM_LIMIT = 56 * 1024 * 1024
_LOG2E = math.log2(math.e)

COL_Z = 0
COL_XS = 1024
COL_GATE = 2048
COL_Q = 5120
COL_GU = 5632
COL_GV = 6144
COL_BC = 6656
COL_KV = 7168
P_WIDTH = 7424
DT_REP = 32
SSD_BC_WIDTH = 2 * SSD_GROUPS * SSD_STATE

_R_Q, _R_K, _R_V, _R_GU, _R_GV, _R_Z, _R_XS, _R_B, _R_C, _R_DT, _R_GATE, _R_END = (
    0, 512, 640, 768, 1280, 1792, 2816, 3840, 4096, 4352, 4384, 7456)


class Layout(NamedTuple):
    n_blocks: int
    n_first: int
    per_seq_first: int
    per_seq_second: int


def _seq_pos(blk, lay):
    in_first = blk < lay.n_first
    pos = jnp.where(in_first, lax.rem(blk, lay.per_seq_first),
                    lax.rem(blk - lay.n_first, lay.per_seq_second))
    last = jnp.where(in_first, lay.per_seq_first - 1, lay.per_seq_second - 1)
    return pos == 0, pos == last


def _sigmoid(x):
    return 0.5 * jnp.tanh(0.5 * x) + 0.5


def _silu(x):
    return x * _sigmoid(x)


def _gelu(x):
    return x * (0.5 * (1.0 + jnp.tanh(math.sqrt(2.0 / math.pi) * (x + 0.044715 * (x * x * x)))))


def _softplus(x):
    return jnp.maximum(x, 0.0) + jnp.log1p(jnp.exp(-jnp.abs(x)))


def _rms(x, g):
    return x * lax.rsqrt(jnp.mean(x * x, axis=-1, keepdims=True) + EPS) * g


def _dot(a, b):
    return jnp.dot(a, b, preferred_element_type=F32)


def _dot_nt(a, b):
    return lax.dot_general(a, b, (((1,), (1,)), ((), ())), preferred_element_type=F32)


def _const_spec(shape):
    nd = len(shape)
    return pl.BlockSpec(shape, lambda *_: (0,) * nd)


def _params(semantics):
    return pltpu.CompilerParams(dimension_semantics=semantics, vmem_limit_bytes=VMEM_LIMIT)


_PROJ_TILE = 512
_PROJ_LOOKAHEAD = 1
_X_HALO = 8


def _split2(x):
    hi = x.astype(BF16)
    return hi, (x - hi.astype(F32)).astype(BF16)


def _inproj_body(xa_ref, xb_ref, xa_prev_ref, xb_prev_ref, xa_next_ref, xb_next_ref, g_ref, w_ref, wdt_ref, wdtT_ref,
                 cw_ref, cb_ref, lng_ref, lnb_ref, kscale_ref, qsum_ref, ksum_ref, o_ref, dt_ref, dtT_ref, *, lay):
    tm = _PROJ_TILE
    nblk = tm // BLK
    i = pl.program_id(0)
    keep_prev = jnp.where(_seq_pos(i * nblk, lay)[0], 0.0, 1.0)
    keep_next = jnp.where(_seq_pos(i * nblk + nblk - 1, lay)[1], 0.0, 1.0)
    in_first = i < lay.n_first // nblk

    row_limit = jnp.where(in_first, tm, 0)

    def pick(a_ref, b_ref):
        a = a_ref[...]
        return jnp.where(lax.broadcasted_iota(jnp.int32, a.shape, 0) < row_limit, a, b_ref[...])

    xn = _rms(pick(xa_ref, xb_ref), g_ref[...]).astype(BF16)
    xh = _rms(jnp.concatenate([pick(xa_prev_ref, xb_prev_ref), pick(xa_next_ref, xb_next_ref)], axis=0),
              g_ref[...]).astype(BF16)

    def put(c0, val):
        o_ref[:, c0:c0 + val.shape[1]] = val.astype(BF16)

    def conv_silu(k0):
        def post(c0, y):
            width = y.shape[1]
            yh = _dot(xh, w_ref[:, c0:c0 + width])
            halo_p = yh[_X_HALO - 1:_X_HALO] * keep_prev
            halo_n = yh[_X_HALO:_X_HALO + 1] * keep_next
            up = pltpu.roll(y, 1, axis=0)
            dn = pltpu.roll(y, tm - 1, axis=0)
            r8 = lax.broadcasted_iota(jnp.int32, (8, width), 0)
            y_m1 = jnp.concatenate([jnp.where(r8 == 0, halo_p, up[:8]), up[8:]], axis=0)
            y_p1 = jnp.concatenate([dn[:tm - 8], jnp.where(r8 == 7, halo_n, dn[tm - 8:])], axis=0)
            w = cw_ref[:, k0:k0 + width]
            put(c0, _silu(cb_ref[:, k0:k0 + width] + y_m1 * w[0:1] + y * w[1:2] + y_p1 * w[2:3]))
        return post

    def q_norm(c0, q):
        sq = jnp.concatenate(_split2(q * q), axis=1)

        def tail():
            qss = _dot(sq, qsum_ref[...])
            put(c0, q * lax.rsqrt(qss * (1.0 / HEAD_DIM) + EPS))
        return tail

    def gelu_ln(c0, y):
        vf = _gelu(y)
        vc = vf - jnp.mean(vf, axis=-1, keepdims=True)
        var = jnp.mean(vc * vc, axis=-1, keepdims=True)
        put(c0, vc * lax.rsqrt(var + EPS) * lng_ref[...] + lnb_ref[...])

    def kv_norm(c0, kv):
        kf = kv[:, :ATTN_KV_WIDTH]
        sq = jnp.concatenate(_split2(kf * kf), axis=1)
        put(c0 + ATTN_KV_WIDTH, kv[:, ATTN_KV_WIDTH:])

        def tail():
            kss = _dot(sq, ksum_ref[...])
            put(c0, kf * lax.rsqrt(kss * (1.0 / HEAD_DIM) + EPS) * kscale_ref[...])
        return tail

    def gate(k):
        return (COL_GATE + k * D_MODEL, D_MODEL, lambda c0, y: put(c0, _sigmoid(y)))

    items = [
        (COL_XS, SSD_INNER, conv_silu(0)),
        (COL_Z, SSD_INNER, lambda c0, y: put(c0, _silu(y))),
        (COL_BC, SSD_BC_WIDTH, conv_silu(SSD_INNER)),
        gate(0),
        (COL_GV, GMLP_WIDTH, gelu_ln),
        gate(1),
        (COL_GU, GMLP_WIDTH, lambda c0, y: put(c0, _gelu(y))),
        gate(2),
        (COL_Q, ATTN_Q_WIDTH, q_norm),
        (COL_KV, 2 * ATTN_KV_WIDTH, kv_norm),
    ]

    def proj(k):
        c0, width, _ = items[k]
        return _dot(xn, w_ref[:, c0:c0 + width])

    pending = [proj(k) for k in range(_PROJ_LOOKAHEAD)]
    tail = None
    for k, (c0, _, post) in enumerate(items):
        y = pending.pop(0)
        if k + _PROJ_LOOKAHEAD < len(items):
            pending.append(proj(k + _PROJ_LOOKAHEAD))
        if tail is not None:
            tail()
        tail = post(c0, y)
    dt_ref[...] = _dot(xn, wdt_ref[...])
    dtT_ref[...] = _dot_nt(wdtT_ref[...], xn)
    if tail is not None:
        tail()


def _head_sum_matrix(width):
    k = np.arange(2 * width)[:, None]
    n = np.arange(width)[None, :]
    return jnp.asarray((k % width) // HEAD_DIM == n // HEAD_DIM, dtype=BF16)


def _two_stream_specs(tm, n_a, n_b):
    return (pl.BlockSpec((tm, D_MODEL), lambda i: (jnp.minimum(i, n_a - 1), 0)),
            pl.BlockSpec((tm, D_MODEL), lambda i: (jnp.clip(i - n_a, 0, n_b - 1), 0)))


def _inproj(xa, xb, p, lay):
    tm = _PROJ_TILE
    n_a, n_b = xa.shape[0] // tm, xb.shape[0] // tm
    t = xa.shape[0] + xb.shape[0]
    assert lay.per_seq_first % (tm // BLK) == 0 and lay.per_seq_second % (tm // BLK) == 0
    hal = tm // _X_HALO
    conv_w = SSD_INNER + SSD_BC_WIDTH

    def halo(tile0, n_tiles, shift):
        last = n_tiles * hal - 1
        return pl.BlockSpec((_X_HALO, D_MODEL),
                            lambda i: (jnp.clip((i - tile0 + shift) * hal - 1 + shift, 0, last), 0))

    return pl.pallas_call(
        functools.partial(_inproj_body, lay=lay),
        grid=(n_a + n_b,),
        in_specs=[
            *_two_stream_specs(tm, n_a, n_b),
            halo(0, n_a, 0), halo(n_a, n_b, 0), halo(0, n_a, 1), halo(n_a, n_b, 1),
            _const_spec((1, D_MODEL)),
            _const_spec((D_MODEL, P_WIDTH)),
            _const_spec((D_MODEL, LANES)),
            _const_spec((DT_REP, D_MODEL)),
            _const_spec((3, conv_w)), _const_spec((1, conv_w)),
            _const_spec((1, GMLP_WIDTH)), _const_spec((1, GMLP_WIDTH)),
            _const_spec((1, ATTN_KV_WIDTH)),
            _const_spec((2 * ATTN_Q_WIDTH, ATTN_Q_WIDTH)),
            _const_spec((2 * ATTN_KV_WIDTH, ATTN_KV_WIDTH)),
        ],
        out_specs=[
            pl.BlockSpec((tm, P_WIDTH), lambda i: (i, 0)),
            pl.BlockSpec((tm, LANES), lambda i: (i, 0)),
            pl.BlockSpec((DT_REP, tm), lambda i: (0, i)),
        ],
        out_shape=[
            jax.ShapeDtypeStruct((t, P_WIDTH), BF16),
            jax.ShapeDtypeStruct((t, LANES), F32),
            jax.ShapeDtypeStruct((DT_REP, t), F32),
        ],
        compiler_params=_params(("parallel",)),
        name="inproj",
    )(xa, xb, xa, xb, xa, xb, p["norm_mix_g"], p["w_main"], p["w_dt"], p["w_dtT"], p["conv_w"], p["conv_b"], p["ln_g"], p["ln_b"],
      p["kscale"], _head_sum_matrix(ATTN_Q_WIDTH), _head_sum_matrix(ATTN_KV_WIDTH))


_ATTN_TILE_BLOCKS = 4
_PV_DELAY = 1
_ATTN_LOOKAHEAD = 4


def _attention_tile(tile, q_ref, kvp_ref, kvm_ref, kvn_ref, bias_ref, sink_ref, put_block, after_item, lay):
    nt = _ATTN_TILE_BLOCKS
    grp = ATTN_HEADS // ATTN_KV_HEADS
    kv = jnp.concatenate([kvp_ref[...], kvm_ref[...], kvn_ref[...]], axis=0)
    kn = [kv[:, g * HEAD_DIM:(g + 1) * HEAD_DIM] for g in range(ATTN_KV_HEADS)]
    vT = jnp.transpose(kv[:, ATTN_KV_WIDTH:].astype(F32)).astype(BF16)
    neg_prev = jnp.where(_seq_pos(tile * nt, lay)[0], NEG_INF, 0.0)
    neg_next = jnp.where(_seq_pos(tile * nt + nt - 1, lay)[1], NEG_INF, 0.0)

    qbs = [q_ref[jb * BLK:(jb + 1) * BLK, :] for jb in range(nt)]

    def scores(jb, hp):
        slab = qbs[jb][:, hp * LANES:(hp + 1) * LANES]
        q2 = jnp.concatenate([slab[:, :HEAD_DIM], slab[:, HEAD_DIM:]], axis=0)
        return _dot_nt(kn[2 * hp // grp][jb * BLK:(jb + 3) * BLK], q2) + bias_ref[hp]

    n_pairs = ATTN_HEADS // 2
    items = [(jb, hp) for jb in range(nt) for hp in range(n_pairs)]
    pending = [scores(*it) for it in items[:_ATTN_LOOKAHEAD]]
    outs = []
    waiting = []

    def value_matmul():
        jb, hp, p, inv = waiting.pop(0)
        g = 2 * hp // grp
        o = _dot(vT[g * HEAD_DIM:(g + 1) * HEAD_DIM, jb * BLK:(jb + 3) * BLK], p) * inv
        outs.extend([o[:, :BLK], o[:, BLK:]])
        if hp == n_pairs - 1:
            put_block(jb, jnp.transpose(jnp.concatenate(outs, axis=0)).astype(BF16))
            outs.clear()

    for idx, (jb, hp) in enumerate(items):
        s = pending.pop(0)
        if idx + _ATTN_LOOKAHEAD < len(items):
            pending.append(scores(*items[idx + _ATTN_LOOKAHEAD]))
        if len(waiting) > _PV_DELAY:
            value_matmul()
        if jb == 0:
            s = jnp.concatenate([s[:BLK] + neg_prev, s[BLK:]], axis=0)
        if jb == nt - 1:
            s = jnp.concatenate([s[:2 * BLK], s[2 * BLK:] + neg_next], axis=0)
        snk = sink_ref[hp:hp + 1, :]
        m = jnp.maximum(jnp.max(s, axis=0, keepdims=True), snk)
        p = jnp.exp2(s - m)
        denom = jnp.sum(p, axis=0, keepdims=True) + jnp.exp2(snk - m)
        waiting.append((jb, hp, p.astype(BF16), 1.0 / denom))
        after_item(idx)
    while waiting:
        value_matmul()


_SPLIT_TERMS = 4


def _lane_split(x, terms=_SPLIT_TERMS):
    rep = lax.broadcasted_iota(jnp.int32, x.shape, 1) // DT_REP
    out = jnp.zeros(x.shape, BF16)
    rem = x
    for r in range(terms):
        term = rem.astype(BF16)
        out = jnp.where(rep == r, term, out)
        rem = rem - term.astype(F32)
    return out


def _sublane_split(x):
    terms, rem = [], x
    for _ in range(_SPLIT_TERMS):
        term = rem.astype(BF16)
        terms.append(term)
        rem = rem - term.astype(F32)
    return jnp.concatenate(terms, axis=0)


_EXPAND_TERMS = 2


def _expand(x, e_ref):
    return _dot(_lane_split(x, _EXPAND_TERMS), e_ref[...])


def _tri(upper):
    r = lax.broadcasted_iota(jnp.int32, (BLK, BLK), 0)
    c = lax.broadcasted_iota(jnp.int32, (BLK, BLK), 1)
    return (c >= r) if upper else (c <= r)


def _ones_where(mask):
    return jnp.where(mask, 1.0, 0.0).astype(BF16)


_SSD_TILE_CHUNKS = 4


def _rows(j):
    return slice(j * BLK, (j + 1) * BLK)


def _dt_row_major(dt_raw, dtb, alog):
    L = BLK
    dt = _softplus(dt_raw + dtb)
    sp = _lane_split(dt * (-jnp.exp(alog)))
    lane = lax.broadcasted_iota(jnp.int32, (L, LANES), 1)
    is_fwd = (lane & (DT_REP - 1)) < SSD_HEADS
    part = jnp.where(is_fwd, _dot(_ones_where(_tri(False)), sp), _dot(_ones_where(_tri(True)), sp))
    acum = part
    for r in range(1, _SPLIT_TERMS):
        acum = acum + pltpu.roll(part, r * DT_REP, axis=1)
    a_end = jnp.where(is_fwd[0:1], acum[L - 1:L], acum[0:1])
    return dt, acum, a_end


def _dt_head_major(dtT_raw, dtbT, alogT):
    nh = SSD_HEADS
    dtT = _softplus(dtT_raw + dtbT)
    spT = _sublane_split(dtT * (-jnp.exp(alogT)))
    pf = _dot(spT, _ones_where(_tri(True)))
    pb = _dot(spT, _ones_where(_tri(False)))
    acumT_f = pf[0:nh]
    acumT_b = pb[nh:2 * nh]
    for r in range(1, _SPLIT_TERMS):
        acumT_f = acumT_f + pf[r * DT_REP:r * DT_REP + nh]
        acumT_b = acumT_b + pb[r * DT_REP + nh:(r + 1) * DT_REP]
    log2_dtT = jnp.log(dtT) * _LOG2E
    return acumT_f * _LOG2E - log2_dtT[0:nh], acumT_b * _LOG2E - log2_dtT[nh:2 * nh]


def _state_factors(dt, acum, a_end, e_ref):
    ea_exp = _expand(jnp.exp(acum), e_ref)
    w_exp = _expand(dt * jnp.exp(a_end - acum), e_ref)
    cd_exp = _expand(jnp.broadcast_to(jnp.exp(a_end), (8, LANES)), e_ref)[0:1]
    return ea_exp, w_exp, cd_exp


def _state_sweep(st_ref, order, xs, bm, cm_bf, fac):
    gw = SSD_INNER // SSD_GROUPS
    upd = {}
    for j in order:
        wx = (xs[j] * fac[j][1]).astype(BF16)
        upd[j] = [_dot(jnp.transpose(bm[j][g]).astype(BF16), wx[:, g * gw:(g + 1) * gw])
                  for g in range(SSD_GROUPS)]
    sg = [st_ref[g] for g in range(SSD_GROUPS)]
    out = {}
    for j in order:
        y_off = [_dot(cm_bf[j][g], sg[g].astype(BF16)) for g in range(SSD_GROUPS)]
        neg_cd = -fac[j][2]
        sg = [upd[j][g] - sg[g] * neg_cd[:, g * gw:(g + 1) * gw] for g in range(SSD_GROUPS)]
        out[j] = jnp.concatenate(y_off, axis=1) * fac[j][0]
    for g in range(SSD_GROUPS):
        st_ref[g] = sg[g]
    return out


def _split_bc(bc):
    n = SSD_STATE
    return ([bc[:, g * n:(g + 1) * n] for g in range(SSD_GROUPS)],
            [bc[:, (SSD_GROUPS + g) * n:(SSD_GROUPS + g + 1) * n] for g in range(SSD_GROUPS)])


def _ssd_bwd_sweep(xs_ref, bc_ref, dt_ref, dtb_ref, alog_ref, eb_ref, yo_ref, st_ref):
    order = list(reversed(range(_SSD_TILE_CHUNKS)))
    rm = {j: _dt_row_major(dt_ref[_rows(j), :], dtb_ref[...], alog_ref[...]) for j in order}
    fac = {j: _state_factors(*rm[j], eb_ref) for j in order}
    xs = {j: xs_ref[_rows(j), :].astype(F32) for j in order}
    bm = {j: _split_bc(bc_ref[_rows(j), :].astype(F32))[0] for j in order}
    cm_bf = {j: _split_bc(bc_ref[_rows(j), :])[1] for j in order}
    y_off = _state_sweep(st_ref, order, xs, bm, cm_bf, fac)
    for j in order:
        yo_ref[_rows(j), :] = y_off[j].astype(BF16)


def _ssd_body(xsf_ref, bcf_ref, dtf_ref, dtTf_ref, xsb_ref, bcb_ref, dtb_ref, bias_ref, alog_ref, biasT_ref, alogT_ref,
              dskip_ref, ef_ref, eb_ref, y_ref, yob_ref, stf_ref, stb_ref, *, lay):
    ns = _SSD_TILE_CHUNKS
    i = pl.program_id(0)
    n_steps = lay.n_blocks // ns

    @pl.when(_seq_pos(i * ns, lay)[0])
    def _():
        stf_ref[...] = jnp.zeros(stf_ref.shape, F32)

    @pl.when(_seq_pos((n_steps - 1 - i) * ns + ns - 1, lay)[1])
    def _():
        stb_ref[...] = jnp.zeros(stb_ref.shape, F32)

    L, nh = BLK, SSD_HEADS
    hpg = nh // SSD_GROUPS
    order = list(range(ns))
    lane = lax.broadcasted_iota(jnp.int32, (L, LANES), 1)
    xs_bf = {j: xsf_ref[_rows(j), :] for j in order}
    bc_bf = {j: bcf_ref[_rows(j), :] for j in order}
    bm_bf = {j: _split_bc(bc_bf[j])[0] for j in order}
    cm_bf = {j: _split_bc(bc_bf[j])[1] for j in order}
    cb = {j: [_dot_nt(cm_bf[j][g], bm_bf[j][g]) for g in range(SSD_GROUPS)] for j in order}
    rm = {j: _dt_row_major(dtf_ref[_rows(j), :], bias_ref[...], alog_ref[...]) for j in order}
    hm = {j: _dt_head_major(dtTf_ref[:, _rows(j)], biasT_ref[...], alogT_ref[...]) for j in order}

    _ssd_bwd_sweep(xsb_ref, bcb_ref, dtb_ref, bias_ref, alog_ref, eb_ref, yob_ref, stb_ref)

    fac = {j: _state_factors(*rm[j], ef_ref) for j in order}
    xs = {j: xs_bf[j].astype(F32) for j in order}
    bm, coef_exp = {}, {}
    for j in order:
        bm[j], cm = _split_bc(bc_bf[j].astype(F32))
        cbd = jnp.where((lane & (nh - 1)) < hpg, jnp.sum(cm[0] * bm[j][0], axis=1, keepdims=True),
                        jnp.sum(cm[1] * bm[j][1], axis=1, keepdims=True))
        coef_exp[j] = _expand(dskip_ref[...] + cbd * rm[j][0], eb_ref)
    y_off = _state_sweep(stf_ref, order, xs, bm, cm_bf, fac)

    causal = _tri(False)
    left = lane < SSD_HEAD_DIM
    zero_bf = jnp.zeros((L, LANES), BF16)
    acum2 = {j: rm[j][1] * _LOG2E for j in order}
    y_loc = {j: [] for j in order}
    for k in range(nh // 2):
        g = (2 * k) // hpg
        for j in order:
            lhs = []
            for h in (2 * k, 2 * k + 1):
                col_f = jnp.broadcast_to(acum2[j][:, h:h + 1], (L, L))
                col_b = jnp.broadcast_to(acum2[j][:, nh + h:nh + h + 1], (L, L))
                e = jnp.where(causal, col_f - hm[j][0][h:h + 1, :], col_b - hm[j][1][h:h + 1, :])
                lhs.append((cb[j][g] * jnp.exp2(e)).astype(BF16))
            xs_pair = xs_bf[j][:, k * LANES:(k + 1) * LANES]
            rhs = jnp.concatenate([jnp.where(left, xs_pair, zero_bf), jnp.where(left, zero_bf, xs_pair)], axis=0)
            y_loc[j].append(_dot(jnp.concatenate(lhs, axis=1), rhs))
    for j in order:
        y = jnp.concatenate(y_loc[j], axis=1) + coef_exp[j] * xs[j] + y_off[j]
        y_ref[_rows(j), :] = y.astype(BF16)


def _ssd(proj, dt, dtT, p, e_mats, lay):
    t = proj.shape[0]
    ns = _SSD_TILE_CHUNKS
    assert lay.per_seq_first % ns == 0 and lay.per_seq_second % ns == 0 and lay.n_first % ns == 0
    nb = lay.n_blocks // ns
    rows = ns * BLK
    xs_blk, bc_blk = COL_XS // SSD_INNER, COL_BC // SSD_BC_WIDTH
    state = pltpu.VMEM((SSD_GROUPS, SSD_STATE, SSD_INNER // SSD_GROUPS), F32)

    def rev(c):
        return nb - 1 - c

    return pl.pallas_call(
        functools.partial(_ssd_body, lay=lay),
        grid=(nb,),
        in_specs=[
            pl.BlockSpec((rows, SSD_INNER), lambda c: (c, xs_blk)),
            pl.BlockSpec((rows, SSD_BC_WIDTH), lambda c: (c, bc_blk)),
            pl.BlockSpec((rows, LANES), lambda c: (c, 0)),
            pl.BlockSpec((DT_REP, rows), lambda c: (0, c)),
            pl.BlockSpec((rows, SSD_INNER), lambda c: (rev(c), xs_blk)),
            pl.BlockSpec((rows, SSD_BC_WIDTH), lambda c: (rev(c), bc_blk)),
            pl.BlockSpec((rows, LANES), lambda c: (rev(c), 0)),
            _const_spec((1, LANES)), _const_spec((1, LANES)),
            _const_spec((DT_REP, 1)), _const_spec((DT_REP, 1)),
            _const_spec((1, LANES)),
            _const_spec((LANES, SSD_INNER)), _const_spec((LANES, SSD_INNER)),
        ],
        out_specs=[
            pl.BlockSpec((rows, SSD_INNER), lambda c: (c, 0)),
            pl.BlockSpec((rows, SSD_INNER), lambda c: (rev(c), 0)),
        ],
        out_shape=[jax.ShapeDtypeStruct((t, SSD_INNER), BF16), jax.ShapeDtypeStruct((t, SSD_INNER), BF16)],
        scratch_shapes=[state, state],
        compiler_params=_params(("arbitrary",)),
        name="ssd",
    )(proj, proj, dt, dtT, proj, proj, dt, p["dtb"], p["alog"], p["dtbT"], p["alogT"], p["dskip"],
      e_mats[0], e_mats[1])


_ROW_TILE = 512
_MERGE_COLS = 256


def _attn_merge_body(q_ref, kvp_ref, kvm_ref, kvn_ref, bias_ref, sink_ref,
                     xa_ref, xb_ref, gu_ref, gv_ref, yc_ref, yob_ref, z_ref, g0_ref, g1_ref, g2_ref, ng_ref, ws_ref, bs_ref,
                     wa_ref, wb_ref, wc_ref, wo_ref, ha_ref, hb_ref, ya_ref, m_ref, *, lay, n_a, n_tiles):
    i = pl.program_id(0)
    slot = lax.rem(i, 2)

    @pl.when(i == 0)
    def _():
        ya_ref[...] = jnp.zeros(ya_ref.shape, BF16)

    ya = ya_ref[1 - slot]
    gw = GMLP_WIDTH // GMLP_GROUPS
    chunks = []
    for c in range(_ROW_TILE // BLK):
        r = slice(c * BLK, (c + 1) * BLK)
        chunks.append(jnp.concatenate(
            [gu_ref[r, g * gw:(g + 1) * gw].astype(F32)
             * (_dot(ws_ref[g], gv_ref[r, g * gw:(g + 1) * gw]) + bs_ref[:, g * gw:(g + 1) * gw])
             for g in range(GMLP_GROUPS)], axis=1))
    yb = jnp.concatenate(chunks, axis=0).astype(BF16)
    y = (yc_ref[...].astype(F32) + yob_ref[...].astype(F32)) * z_ref[...].astype(F32)
    sw = SSD_INNER // SSD_GROUPS
    yn = jnp.concatenate([_rms(y[:, g * sw:(g + 1) * sw], ng_ref[:, g * sw:(g + 1) * sw])
                          for g in range(SSD_GROUPS)], axis=1).astype(BF16)

    acc = {}

    def piece(c, k):
        cols = slice(c * _MERGE_COLS, (c + 1) * _MERGE_COLS)

        def run():
            if k == 0:
                acc[c] = g0_ref[:, cols].astype(F32) * _dot(ya, wa_ref[:, cols])
            elif k == 1:
                acc[c] = acc[c] + g1_ref[:, cols].astype(F32) * _dot(yb, wb_ref[:, cols])
            else:
                m_ref[:, cols] = (acc[c] + g2_ref[:, cols].astype(F32) * _dot(yn, wc_ref[:, cols])).astype(BF16)
        return run

    pieces = [piece(c, k) for c in range(D_MODEL // _MERGE_COLS) for k in range(3)]
    n_items = _ATTN_TILE_BLOCKS * ATTN_HEADS // 2
    done = [0]

    def after_item(idx):
        while done[0] < (idx + 1) * len(pieces) // n_items:
            pieces[done[0]]()
            done[0] += 1

    def put_block(jb, val):
        ya_ref[slot, jb * BLK:(jb + 1) * BLK, :] = val

    _attention_tile(jnp.minimum(i, n_tiles - 1), q_ref, kvp_ref, kvm_ref, kvn_ref, bias_ref, sink_ref,
                    put_block, after_item, lay)

    m = m_ref[...]
    in_first = jnp.maximum(i - 1, 0) < n_a

    @pl.when(in_first)
    def _():
        ha_ref[...] = xa_ref[...] + _dot(m, wo_ref[...])

    @pl.when(jnp.logical_not(in_first))
    def _():
        hb_ref[...] = xb_ref[...] + _dot(m, wo_ref[...])


def _attn_merge(xa, xb, yc, yob, proj, bias, p, lay):
    tm = _ROW_TILE
    nt = _ATTN_TILE_BLOCKS
    assert tm == nt * BLK
    assert lay.per_seq_first % nt == 0 and lay.per_seq_second % nt == 0 and lay.n_first % nt == 0
    n_a, n_b = xa.shape[0] // tm, xb.shape[0] // tm
    n_tiles = n_a + n_b
    g_blk = COL_GATE // D_MODEL
    kv_w = 2 * ATTN_KV_WIDTH
    kv_blk = COL_KV // kv_w
    last_blk = lay.n_blocks - 1

    def att(i):
        return jnp.minimum(i, n_tiles - 1)

    def mrg(i):
        return jnp.maximum(i - 1, 0)

    def row(width, col_blk=0):
        return pl.BlockSpec((tm, width), lambda i: (mrg(i), col_blk))

    stream = (pl.BlockSpec((tm, D_MODEL), lambda i: (jnp.minimum(mrg(i), n_a - 1), 0)),
              pl.BlockSpec((tm, D_MODEL), lambda i: (jnp.clip(mrg(i) - n_a, 0, n_b - 1), 0)))
    return pl.pallas_call(
        functools.partial(_attn_merge_body, lay=lay, n_a=n_a, n_tiles=n_tiles),
        grid=(n_tiles + 1,),
        in_specs=[
            pl.BlockSpec((tm, ATTN_Q_WIDTH), lambda i: (att(i), COL_Q // ATTN_Q_WIDTH)),
            pl.BlockSpec((BLK, kv_w), lambda i: (jnp.maximum(att(i) * nt - 1, 0), kv_blk)),
            pl.BlockSpec((tm, kv_w), lambda i: (att(i), kv_blk)),
            pl.BlockSpec((BLK, kv_w), lambda i: (jnp.minimum(att(i) * nt + nt, last_blk), kv_blk)),
            _const_spec((ATTN_HEADS // 2, 3 * BLK, 2 * BLK)),
            _const_spec((ATTN_HEADS // 2, 2 * BLK)),
            *stream, row(GMLP_WIDTH, COL_GU // GMLP_WIDTH), row(GMLP_WIDTH, COL_GV // GMLP_WIDTH),
            row(SSD_INNER), row(SSD_INNER), row(SSD_INNER, COL_Z // SSD_INNER),
            row(D_MODEL, g_blk), row(D_MODEL, g_blk + 1), row(D_MODEL, g_blk + 2),
            _const_spec((1, SSD_INNER)),
            _const_spec((GMLP_GROUPS, BLK, BLK)), _const_spec((BLK, GMLP_WIDTH)),
            _const_spec((ATTN_Q_WIDTH, D_MODEL)), _const_spec((GMLP_WIDTH, D_MODEL)),
            _const_spec((SSD_INNER, D_MODEL)), _const_spec((D_MODEL, D_MODEL)),
        ],
        out_specs=list(stream),
        out_shape=[jax.ShapeDtypeStruct(xa.shape, F32), jax.ShapeDtypeStruct(xb.shape, F32)],
        scratch_shapes=[pltpu.VMEM((2, tm, ATTN_Q_WIDTH), BF16), pltpu.VMEM((tm, D_MODEL), BF16)],
        compiler_params=_params(("arbitrary",)),
        name="attn_merge",
    )(proj, proj, proj, proj, bias, p["sink"],
      xa, xb, proj, proj, yc, yob, proj, proj, proj, proj, p["ssd_norm_g"], p["w_s"], p["b_s"],
      p["w_a"], p["w_b"], p["w_c"], p["w_o"])


_FF_CHUNK = 1024


def _ffn_body(ha_ref, hb_ref, g_ref, w1_ref, w2_ref, oa_ref, ob_ref, *, n_a):
    def run(h_ref, o_ref):
        h = h_ref[...]
        hn = _rms(h, g_ref[...]).astype(BF16)
        acc = h
        for c0 in range(0, FF_DIM, _FF_CHUNK):
            a = jnp.maximum(_dot(hn, w1_ref[:, c0:c0 + _FF_CHUNK]), 0.0)
            acc = acc + _dot((a * a).astype(BF16), w2_ref[c0:c0 + _FF_CHUNK, :])
        o_ref[...] = acc

    in_first = pl.program_id(0) < n_a
    pl.when(in_first)(lambda: run(ha_ref, oa_ref))
    pl.when(jnp.logical_not(in_first))(lambda: run(hb_ref, ob_ref))


def _ffn(ha, hb, g, w1, w2):
    tm = _ROW_TILE
    n_a, n_b = ha.shape[0] // tm, hb.shape[0] // tm
    stream = _two_stream_specs(tm, n_a, n_b)
    return pl.pallas_call(
        functools.partial(_ffn_body, n_a=n_a),
        grid=(n_a + n_b,),
        in_specs=[*stream, _const_spec((1, D_MODEL)), _const_spec((D_MODEL, FF_DIM)), _const_spec((FF_DIM, D_MODEL))],
        out_specs=list(stream),
        out_shape=[jax.ShapeDtypeStruct(ha.shape, F32), jax.ShapeDtypeStruct(hb.shape, F32)],
        compiler_params=_params(("arbitrary",)),
        name="ffn",
    )(ha, hb, g, w1, w2)


def _t5_bucket(rel):
    nb = NUM_BUCKETS // 2
    max_exact = nb // 2
    ret = jnp.where(rel > 0, nb, 0)
    n = jnp.abs(rel)
    n_safe = jnp.maximum(n, 1).astype(F32)
    large = max_exact + (jnp.log(n_safe / max_exact) / math.log(MAX_DISTANCE / max_exact)
                         * (nb - max_exact)).astype(jnp.int32)
    large = jnp.minimum(large, nb - 1)
    return ret + jnp.where(n < max_exact, n, large)


def _bias_table(rel_bias):
    qi = jnp.arange(BLK)[:, None]
    ki = jnp.arange(3 * BLK)[None, :]
    rel = ki - BLK - qi
    onehot = (_t5_bucket(rel)[..., None] == jnp.arange(NUM_BUCKETS)).astype(F32)
    bias = jnp.einsum("qkb,bh->hkq", onehot, rel_bias.astype(F32), precision=lax.Precision.HIGHEST) * _LOG2E
    bias = jnp.where((jnp.abs(rel) <= WINDOW).T[None], bias, NEG_INF)
    return jnp.concatenate([bias[0::2], bias[1::2]], axis=-1)


def _expand_matrix(d):
    k = np.arange(LANES)[:, None]
    col = np.arange(SSD_INNER)[None, :]
    hit = ((k % DT_REP) == d * SSD_HEADS + col // SSD_HEAD_DIM)
    return jnp.asarray(hit, dtype=BF16)


def _prep_layer_params(w_in, norm_mix_g, q_norm_g, k_norm_g, attn_sink, gmlp_ln_g, gmlp_ln_b, w_spatial,
                       b_spatial, conv_w, conv_b, dt_bias, a_log, d_skip, ssd_norm_g, w_up_attn, w_up_gmlp,
                       w_up_ssd, w_out, norm_ff_g, w_ff1, w_ff2):
    depth = w_in.shape[0]

    w_bf = w_in.astype(BF16)

    def cols(a, b):
        return w_bf[:, :, a:b]

    w_main = jnp.concatenate([
        cols(_R_Z, _R_XS), cols(_R_XS, _R_B), cols(_R_GATE, _R_END), cols(_R_Q, _R_K),
        cols(_R_GU, _R_GV), cols(_R_GV, _R_Z), cols(_R_B, _R_DT), cols(_R_K, _R_GU)], axis=-1)
    w_dt = w_in[:, :, _R_DT:_R_GATE]
    rep = LANES // DT_REP
    gw = GMLP_WIDTH // GMLP_GROUPS

    def dt_lanes(v):
        return jnp.tile(v.reshape(depth, 1, DT_REP), (1, 1, rep))

    return dict(
        norm_mix_g=norm_mix_g[:, None, :],
        w_main=w_main,
        w_dt=jnp.tile(w_dt, (1, 1, rep)).astype(BF16),
        w_dtT=jnp.swapaxes(w_dt, 1, 2).astype(BF16),
        kscale=jnp.tile(q_norm_g * k_norm_g * (HEAD_DIM ** -0.5 * _LOG2E), (1, ATTN_KV_HEADS))[:, None, :],
        sink=jnp.repeat(attn_sink * _LOG2E, BLK, axis=-1).reshape(depth, ATTN_HEADS // 2, 2 * BLK),
        ln_g=gmlp_ln_g[:, None, :], ln_b=gmlp_ln_b[:, None, :],
        w_s=w_spatial.astype(BF16),
        b_s=jnp.repeat(jnp.swapaxes(b_spatial, 1, 2), gw, axis=-1),
        conv_w=conv_w, conv_b=conv_b[:, None, :],
        dtb=dt_lanes(dt_bias), alog=dt_lanes(a_log),
        dtbT=dt_bias.reshape(depth, DT_REP, 1), alogT=a_log.reshape(depth, DT_REP, 1),
        dskip=dt_lanes(jnp.concatenate([d_skip, d_skip], axis=-1)),
        ssd_norm_g=ssd_norm_g[:, None, :],
        w_a=w_up_attn.astype(BF16), w_b=w_up_gmlp.astype(BF16), w_c=w_up_ssd.astype(BF16),
        w_o=w_out.astype(BF16),
        norm_ff_g=norm_ff_g[:, None, :],
        w1=w_ff1.astype(BF16), w2=w_ff2.astype(BF16),
    )


def _layer(xa, xb, p, bias, e_mats, lay):
    proj, dt, dtT = _inproj(xa, xb, p, lay)
    yc, yob = _ssd(proj, dt, dtT, p, e_mats, lay)
    ha, hb = _attn_merge(xa, xb, yc, yob, proj, bias, p, lay)
    return _ffn(ha, hb, p["norm_ff_g"], p["w1"], p["w2"])


def _encoder(x_first, x_second, rel_bias, layer_params):
    b1, s1, _ = x_first.shape
    b2, s2, _ = x_second.shape
    t1, t2 = b1 * s1, b2 * s2
    lay = Layout(n_blocks=(t1 + t2) // BLK, n_first=t1 // BLK, per_seq_first=s1 // BLK,
                 per_seq_second=s2 // BLK)
    bias = _bias_table(rel_bias)
    e_mats = (_expand_matrix(0), _expand_matrix(1))
    params = _prep_layer_params(**layer_params)

    xa, xb = x_first.reshape(t1, D_MODEL), x_second.reshape(t2, D_MODEL)
    for layer in range(layer_params["w_in"].shape[0]):
        xa, xb = _layer(xa, xb, {k: v[layer] for k, v in params.items()}, bias, e_mats, lay)
    return xa.reshape(b1, s1, D_MODEL), xb.reshape(b2, s2, D_MODEL)


def kernel(x_prompt, x_sample, rel_bias, norm_mix_g, w_in, q_norm_g, k_norm_g, attn_sink, gmlp_ln_g, gmlp_ln_b,
           w_spatial, b_spatial, conv_w, conv_b, dt_bias, a_log, d_skip, ssd_norm_g, w_up_attn, w_up_gmlp,
           w_up_ssd, w_out, norm_ff_g, w_ff1, w_ff2):
    layer_params = dict(
        w_in=w_in, norm_mix_g=norm_mix_g, q_norm_g=q_norm_g, k_norm_g=k_norm_g, attn_sink=attn_sink,
        gmlp_ln_g=gmlp_ln_g, gmlp_ln_b=gmlp_ln_b, w_spatial=w_spatial, b_spatial=b_spatial, conv_w=conv_w,
        conv_b=conv_b, dt_bias=dt_bias, a_log=a_log, d_skip=d_skip, ssd_norm_g=ssd_norm_g,
        w_up_attn=w_up_attn, w_up_gmlp=w_up_gmlp, w_up_ssd=w_up_ssd, w_out=w_out, norm_ff_g=norm_ff_g,
        w_ff1=w_ff1, w_ff2=w_ff2)
    return _encoder(x_prompt, x_sample, rel_bias, layer_params)
```

```python
import functools
import math
from typing import NamedTuple

import numpy as np
import jax
import jax.numpy as jnp
from jax import lax
from jax.experimental import pallas as pl
from jax.experimental.pallas import tpu as pltpu

F32 = jnp.float32
BF16 = jnp.bfloat16

D_MODEL = 1024
ATTN_HEADS = 8
ATTN_KV_HEADS = 2
HEAD_DIM = 64
ATTN_Q_WIDTH = ATTN_HEADS * HEAD_DIM
ATTN_KV_WIDTH = ATTN_KV_HEADS * HEAD_DIM
WINDOW = 128
NUM_BUCKETS = 32
MAX_DISTANCE = 128
GMLP_WIDTH = 512
GMLP_GROUPS = 4
SSD_INNER = 1024
SSD_HEAD_DIM = 64
SSD_HEADS = SSD_INNER // SSD_HEAD_DIM
SSD_GROUPS = 2
SSD_STATE = 128
N_DIR = 2
FF_DIM = 4 * D_MODEL
EPS = 1e-6
NEG_INF = -1e30

BLK = 128
LANES = 128
BF16_SUBLANES = 16
VMEM_LIMIT = 56 * 1024 * 1024
_LOG2E = math.log2(math.e)

COL_Z = 0
COL_XS = 1024
COL_GATE = 2048
COL_Q = 5120
COL_GU = 5632
COL_GV = 6144
COL_BC = 6656
COL_KV = 7168
P_WIDTH = 7424
DT_REP = 32
SSD_BC_WIDTH = 2 * SSD_GROUPS * SSD_STATE

_R_Q, _R_K, _R_V, _R_GU, _R_GV, _R_Z, _R_XS, _R_B, _R_C, _R_DT, _R_GATE, _R_END = (
    0, 512, 640, 768, 1280, 1792, 2816, 3840, 4096, 4352, 4384, 7456)


class Layout(NamedTuple):
    n_blocks: int
    n_first: int
    per_seq_first: int
    per_seq_second: int


def _seq_pos(blk, lay):
    in_first = blk < lay.n_first
    pos = jnp.where(in_first, lax.rem(blk, lay.per_seq_first),
                    lax.rem(blk - lay.n_first, lay.per_seq_second))
    last = jnp.where(in_first, lay.per_seq_first - 1, lay.per_seq_second - 1)
    return pos == 0, pos == last


def _sigmoid(x):
    return 0.5 * jnp.tanh(0.5 * x) + 0.5


def _silu(x):
    return x * _sigmoid(x)


def _gelu(x):
    return x * (0.5 * (1.0 + jnp.tanh(math.sqrt(2.0 / math.pi) * (x + 0.044715 * (x * x * x)))))


def _softplus(x):
    return jnp.maximum(x, 0.0) + jnp.log1p(jnp.exp(-jnp.abs(x)))


def _rms(x, g):
    return x * lax.rsqrt(jnp.mean(x * x, axis=-1, keepdims=True) + EPS) * g


def _dot(a, b):
    return jnp.dot(a, b, preferred_element_type=F32)


def _dot_nt(a, b):
    return lax.dot_general(a, b, (((1,), (1,)), ((), ())), preferred_element_type=F32)


def _const_spec(shape):
    nd = len(shape)
    return pl.BlockSpec(shape, lambda *_: (0,) * nd)


def _params(semantics):
    return pltpu.CompilerParams(dimension_semantics=semantics, vmem_limit_bytes=VMEM_LIMIT)


_PROJ_TILE = 512
_PROJ_LOOKAHEAD = 1
_X_HALO = 8


def _split2(x):
    hi = x.astype(BF16)
    return hi, (x - hi.astype(F32)).astype(BF16)


def _inproj_body(xa_ref, xb_ref, xa_prev_ref, xb_prev_ref, xa_next_ref, xb_next_ref, g_ref, w_ref, wdt_ref, wdtT_ref,
                 cw_ref, cb_ref, lng_ref, lnb_ref, kscale_ref, qsum_ref, ksum_ref, o_ref, dt_ref, dtT_ref, *, lay):
    tm = _PROJ_TILE
    nblk = tm // BLK
    i = pl.program_id(0)
    keep_prev = jnp.where(_seq_pos(i * nblk, lay)[0], 0.0, 1.0)
    keep_next = jnp.where(_seq_pos(i * nblk + nblk - 1, lay)[1], 0.0, 1.0)
    in_first = i < lay.n_first // nblk

    row_limit = jnp.where(in_first, tm, 0)

    def pick(a_ref, b_ref):
        a = a_ref[...]
        return jnp.where(lax.broadcasted_iota(jnp.int32, a.shape, 0) < row_limit, a, b_ref[...])

    xn = _rms(pick(xa_ref, xb_ref), g_ref[...]).astype(BF16)
    xh = _rms(jnp.concatenate([pick(xa_prev_ref, xb_prev_ref), pick(xa_next_ref, xb_next_ref)], axis=0),
              g_ref[...]).astype(BF16)

    def put(c0, val):
        o_ref[:, c0:c0 + val.shape[1]] = val.astype(BF16)

    def conv_silu(k0):
        def post(c0, y):
            width = y.shape[1]
            yh = _dot(xh, w_ref[:, c0:c0 + width])
            halo_p = yh[_X_HALO - 1:_X_HALO] * keep_prev
            halo_n = yh[_X_HALO:_X_HALO + 1] * keep_next
            up = pltpu.roll(y, 1, axis=0)
            dn = pltpu.roll(y, tm - 1, axis=0)
            r8 = lax.broadcasted_iota(jnp.int32, (8, width), 0)
            y_m1 = jnp.concatenate([jnp.where(r8 == 0, halo_p, up[:8]), up[8:]], axis=0)
            y_p1 = jnp.concatenate([dn[:tm - 8], jnp.where(r8 == 7, halo_n, dn[tm - 8:])], axis=0)
            w = cw_ref[:, k0:k0 + width]
            put(c0, _silu(cb_ref[:, k0:k0 + width] + y_m1 * w[0:1] + y * w[1:2] + y_p1 * w[2:3]))
        return post

    def q_norm(c0, q):
        sq = jnp.concatenate(_split2(q * q), axis=1)

        def tail():
            qss = _dot(sq, qsum_ref[...])
            put(c0, q * lax.rsqrt(qss * (1.0 / HEAD_DIM) + EPS))
        return tail

    def gelu_ln(c0, y):
        vf = _gelu(y)
        vc = vf - jnp.mean(vf, axis=-1, keepdims=True)
        var = jnp.mean(vc * vc, axis=-1, keepdims=True)
        put(c0, vc * lax.rsqrt(var + EPS) * lng_ref[...] + lnb_ref[...])

    def kv_norm(c0, kv):
        kf = kv[:, :ATTN_KV_WIDTH]
        sq = jnp.concatenate(_split2(kf * kf), axis=1)
        put(c0 + ATTN_KV_WIDTH, kv[:, ATTN_KV_WIDTH:])

        def tail():
            kss = _dot(sq, ksum_ref[...])
            put(c0, kf * lax.rsqrt(kss * (1.0 / HEAD_DIM) + EPS) * kscale_ref[...])
        return tail

    def gate(k):
        return (COL_GATE + k * D_MODEL, D_MODEL, lambda c0, y: put(c0, _sigmoid(y)))

    items = [
        (COL_XS, SSD_INNER, conv_silu(0)),
        (COL_Z, SSD_INNER, lambda c0, y: put(c0, _silu(y))),
        (COL_BC, SSD_BC_WIDTH, conv_silu(SSD_INNER)),
        gate(0),
        (COL_GV, GMLP_WIDTH, gelu_ln),
        gate(1),
        (COL_GU, GMLP_WIDTH, lambda c0, y: put(c0, _gelu(y))),
        gate(2),
        (COL_Q, ATTN_Q_WIDTH, q_norm),
        (COL_KV, 2 * ATTN_KV_WIDTH, kv_norm),
    ]

    def proj(k):
        c0, width, _ = items[k]
        return _dot(xn, w_ref[:, c0:c0 + width])

    pending = [proj(k) for k in range(_PROJ_LOOKAHEAD)]
    tail = None
    for k, (c0, _, post) in enumerate(items):
        y = pending.pop(0)
        if k + _PROJ_LOOKAHEAD < len(items):
            pending.append(proj(k + _PROJ_LOOKAHEAD))
        if tail is not None:
            tail()
        tail = post(c0, y)
    dt_ref[...] = _dot(xn, wdt_ref[...])
    dtT_ref[...] = _dot_nt(wdtT_ref[...], xn)
    if tail is not None:
        tail()


def _head_sum_matrix(width):
    k = np.arange(2 * width)[:, None]
    n = np.arange(width)[None, :]
    return jnp.asarray((k % width) // HEAD_DIM == n // HEAD_DIM, dtype=BF16)


def _two_stream_specs(tm, n_a, n_b):
    return (pl.BlockSpec((tm, D_MODEL), lambda i: (jnp.minimum(i, n_a - 1), 0)),
            pl.BlockSpec((tm, D_MODEL), lambda i: (jnp.clip(i - n_a, 0, n_b - 1), 0)))


def _inproj(xa, xb, p, lay):
    tm = _PROJ_TILE
    n_a, n_b = xa.shape[0] // tm, xb.shape[0] // tm
    t = xa.shape[0] + xb.shape[0]
    assert lay.per_seq_first % (tm // BLK) == 0 and lay.per_seq_second % (tm // BLK) == 0
    hal = tm // _X_HALO
    conv_w = SSD_INNER + SSD_BC_WIDTH

    def halo(tile0, n_tiles, shift):
        last = n_tiles * hal - 1
        return pl.BlockSpec((_X_HALO, D_MODEL),
                            lambda i: (jnp.clip((i - tile0 + shift) * hal - 1 + shift, 0, last), 0))

    return pl.pallas_call(
        functools.partial(_inproj_body, lay=lay),
        grid=(n_a + n_b,),
        in_specs=[
            *_two_stream_specs(tm, n_a, n_b),
            halo(0, n_a, 0), halo(n_a, n_b, 0), halo(0, n_a, 1), halo(n_a, n_b, 1),
            _const_spec((1, D_MODEL)),
            _const_spec((D_MODEL, P_WIDTH)),
            _const_spec((D_MODEL, LANES)),
            _const_spec((DT_REP, D_MODEL)),
            _const_spec((3, conv_w)), _const_spec((1, conv_w)),
            _const_spec((1, GMLP_WIDTH)), _const_spec((1, GMLP_WIDTH)),
            _const_spec((1, ATTN_KV_WIDTH)),
            _const_spec((2 * ATTN_Q_WIDTH, ATTN_Q_WIDTH)),
            _const_spec((2 * ATTN_KV_WIDTH, ATTN_KV_WIDTH)),
        ],
        out_specs=[
            pl.BlockSpec((tm, P_WIDTH), lambda i: (i, 0)),
            pl.BlockSpec((tm, LANES), lambda i: (i, 0)),
            pl.BlockSpec((DT_REP, tm), lambda i: (0, i)),
        ],
        out_shape=[
            jax.ShapeDtypeStruct((t, P_WIDTH), BF16),
            jax.ShapeDtypeStruct((t, LANES), F32),
            jax.ShapeDtypeStruct((DT_REP, t), F32),
        ],
        compiler_params=_params(("parallel",)),
        name="inproj",
    )(xa, xb, xa, xb, xa, xb, p["norm_mix_g"], p["w_main"], p["w_dt"], p["w_dtT"], p["conv_w"], p["conv_b"], p["ln_g"], p["ln_b"],
      p["kscale"], _head_sum_matrix(ATTN_Q_WIDTH), _head_sum_matrix(ATTN_KV_WIDTH))


_ATTN_TILE_BLOCKS = 4
_PV_DELAY = 1
_ATTN_LOOKAHEAD = 4


def _attention_tile(tile, q_ref, kvp_ref, kvm_ref, kvn_ref, bias_ref, sink_ref, put_block, after_item, lay):
    nt = _ATTN_TILE_BLOCKS
    grp = ATTN_HEADS // ATTN_KV_HEADS
    kv = jnp.concatenate([kvp_ref[...], kvm_ref[...], kvn_ref[...]], axis=0)
    kn = [kv[:, g * HEAD_DIM:(g + 1) * HEAD_DIM] for g in range(ATTN_KV_HEADS)]
    vT = jnp.transpose(kv[:, ATTN_KV_WIDTH:].astype(F32)).astype(BF16)
    neg_prev = jnp.where(_seq_pos(tile * nt, lay)[0], NEG_INF, 0.0)
    neg_next = jnp.where(_seq_pos(tile * nt + nt - 1, lay)[1], NEG_INF, 0.0)

    qbs = [q_ref[jb * BLK:(jb + 1) * BLK, :] for jb in range(nt)]

    def scores(jb, hp):
        slab = qbs[jb][:, hp * LANES:(hp + 1) * LANES]
        q2 = jnp.concatenate([slab[:, :HEAD_DIM], slab[:, HEAD_DIM:]], axis=0)
        return _dot_nt(kn[2 * hp // grp][jb * BLK:(jb + 3) * BLK], q2) + bias_ref[hp]

    n_pairs = ATTN_HEADS // 2
    items = [(jb, hp) for jb in range(nt) for hp in range(n_pairs)]
    pending = [scores(*it) for it in items[:_ATTN_LOOKAHEAD]]
    outs = []
    waiting = []

    def value_matmul():
        jb, hp, p, inv = waiting.pop(0)
        g = 2 * hp // grp
        o = _dot(vT[g * HEAD_DIM:(g + 1) * HEAD_DIM, jb * BLK:(jb + 3) * BLK], p) * inv
        outs.extend([o[:, :BLK], o[:, BLK:]])
        if hp == n_pairs - 1:
            put_block(jb, jnp.transpose(jnp.concatenate(outs, axis=0)).astype(BF16))
            outs.clear()

    for idx, (jb, hp) in enumerate(items):
        s = pending.pop(0)
        if idx + _ATTN_LOOKAHEAD < len(items):
            pending.append(scores(*items[idx + _ATTN_LOOKAHEAD]))
        if len(waiting) > _PV_DELAY:
            value_matmul()
        if jb == 0:
            s = jnp.concatenate([s[:BLK] + neg_prev, s[BLK:]], axis=0)
        if jb == nt - 1:
            s = jnp.concatenate([s[:2 * BLK], s[2 * BLK:] + neg_next], axis=0)
        snk = sink_ref[hp:hp + 1, :]
        m = jnp.maximum(jnp.max(s, axis=0, keepdims=True), snk)
        p = jnp.exp2(s - m)
        denom = jnp.sum(p, axis=0, keepdims=True) + jnp.exp2(snk - m)
        waiting.append((jb, hp, p.astype(BF16), 1.0 / denom))
        after_item(idx)
    while waiting:
        value_matmul()


_SPLIT_TERMS = 4


def _lane_split(x, terms=_SPLIT_TERMS):
    rep = lax.broadcasted_iota(jnp.int32, x.shape, 1) // DT_REP
    out = jnp.zeros(x.shape, BF16)
    rem = x
    for r in range(terms):
        term = rem.astype(BF16)
        out = jnp.where(rep == r, term, out)
        rem = rem - term.astype(F32)
    return out


def _sublane_split(x):
    terms, rem = [], x
    for _ in range(_SPLIT_TERMS):
        term = rem.astype(BF16)
        terms.append(term)
        rem = rem - term.astype(F32)
    return jnp.concatenate(terms, axis=0)


_EXPAND_TERMS = 2


def _expand(x, e_ref):
    return _dot(_lane_split(x, _EXPAND_TERMS), e_ref[...])


def _tri(upper):
    r = lax.broadcasted_iota(jnp.int32, (BLK, BLK), 0)
    c = lax.broadcasted_iota(jnp.int32, (BLK, BLK), 1)
    return (c >= r) if upper else (c <= r)


def _ones_where(mask):
    return jnp.where(mask, 1.0, 0.0).astype(BF16)


_SSD_TILE_CHUNKS = 4


def _rows(j):
    return slice(j * BLK, (j + 1) * BLK)


def _dt_row_major(dt_raw, dtb, alog):
    L = BLK
    dt = _softplus(dt_raw + dtb)
    sp = _lane_split(dt * (-jnp.exp(alog)))
    lane = lax.broadcasted_iota(jnp.int32, (L, LANES), 1)
    is_fwd = (lane & (DT_REP - 1)) < SSD_HEADS
    part = jnp.where(is_fwd, _dot(_ones_where(_tri(False)), sp), _dot(_ones_where(_tri(True)), sp))
    acum = part
    for r in range(1, _SPLIT_TERMS):
        acum = acum + pltpu.roll(part, r * DT_REP, axis=1)
    a_end = jnp.where(is_fwd[0:1], acum[L - 1:L], acum[0:1])
    return dt, acum, a_end


def _dt_head_major(dtT_raw, dtbT, alogT):
    nh = SSD_HEADS
    dtT = _softplus(dtT_raw + dtbT)
    spT = _sublane_split(dtT * (-jnp.exp(alogT)))
    pf = _dot(spT, _ones_where(_tri(True)))
    pb = _dot(spT, _ones_where(_tri(False)))
    acumT_f = pf[0:nh]
    acumT_b = pb[nh:2 * nh]
    for r in range(1, _SPLIT_TERMS):
        acumT_f = acumT_f + pf[r * DT_REP:r * DT_REP + nh]
        acumT_b = acumT_b + pb[r * DT_REP + nh:(r + 1) * DT_REP]
    log2_dtT = jnp.log(dtT) * _LOG2E
    return acumT_f * _LOG2E - log2_dtT[0:nh], acumT_b * _LOG2E - log2_dtT[nh:2 * nh]


def _state_factors(dt, acum, a_end, e_ref):
    ea_exp = _expand(jnp.exp(acum), e_ref)
    w_exp = _expand(dt * jnp.exp(a_end - acum), e_ref)
    cd_exp = _expand(jnp.broadcast_to(jnp.exp(a_end), (8, LANES)), e_ref)[0:1]
    return ea_exp, w_exp, cd_exp


def _state_sweep(st_ref, order, xs, bm, cm_bf, fac):
    gw = SSD_INNER // SSD_GROUPS
    upd = {}
    for j in order:
        wx = (xs[j] * fac[j][1]).astype(BF16)
        upd[j] = [_dot(jnp.transpose(bm[j][g]).astype(BF16), wx[:, g * gw:(g + 1) * gw])
                  for g in range(SSD_GROUPS)]
    sg = [st_ref[g] for g in range(SSD_GROUPS)]
    out = {}
    for j in order:
        y_off = [_dot(cm_bf[j][g], sg[g].astype(BF16)) for g in range(SSD_GROUPS)]
        neg_cd = -fac[j][2]
        sg = [upd[j][g] - sg[g] * neg_cd[:, g * gw:(g + 1) * gw] for g in range(SSD_GROUPS)]
        out[j] = jnp.concatenate(y_off, axis=1) * fac[j][0]
    for g in range(SSD_GROUPS):
        st_ref[g] = sg[g]
    return out


def _split_bc(bc):
    n = SSD_STATE
    return ([bc[:, g * n:(g + 1) * n] for g in range(SSD_GROUPS)],
            [bc[:, (SSD_GROUPS + g) * n:(SSD_GROUPS + g + 1) * n] for g in range(SSD_GROUPS)])


def _ssd_bwd_sweep(xs_ref, bc_ref, dt_ref, dtb_ref, alog_ref, eb_ref, yo_ref, st_ref):
    order = list(reversed(range(_SSD_TILE_CHUNKS)))
    rm = {j: _dt_row_major(dt_ref[_rows(j), :], dtb_ref[...], alog_ref[...]) for j in order}
    fac = {j: _state_factors(*rm[j], eb_ref) for j in order}
    xs = {j: xs_ref[_rows(j), :].astype(F32) for j in order}
    bm = {j: _split_bc(bc_ref[_rows(j), :].astype(F32))[0] for j in order}
    cm_bf = {j: _split_bc(bc_ref[_rows(j), :])[1] for j in order}
    y_off = _state_sweep(st_ref, order, xs, bm, cm_bf, fac)
    for j in order:
        yo_ref[_rows(j), :] = y_off[j].astype(BF16)


def _ssd_body(xsf_ref, bcf_ref, dtf_ref, dtTf_ref, xsb_ref, bcb_ref, dtb_ref, bias_ref, alog_ref, biasT_ref, alogT_ref,
              dskip_ref, ef_ref, eb_ref, y_ref, yob_ref, stf_ref, stb_ref, *, lay):
    ns = _SSD_TILE_CHUNKS
    i = pl.program_id(0)
    n_steps = lay.n_blocks // ns

    @pl.when(_seq_pos(i * ns, lay)[0])
    def _():
        stf_ref[...] = jnp.zeros(stf_ref.shape, F32)

    @pl.when(_seq_pos((n_steps - 1 - i) * ns + ns - 1, lay)[1])
    def _():
        stb_ref[...] = jnp.zeros(stb_ref.shape, F32)

    L, nh = BLK, SSD_HEADS
    hpg = nh // SSD_GROUPS
    order = list(range(ns))
    lane = lax.broadcasted_iota(jnp.int32, (L, LANES), 1)
    xs_bf = {j: xsf_ref[_rows(j), :] for j in order}
    bc_bf = {j: bcf_ref[_rows(j), :] for j in order}
    bm_bf = {j: _split_bc(bc_bf[j])[0] for j in order}
    cm_bf = {j: _split_bc(bc_bf[j])[1] for j in order}
    cb = {j: [_dot_nt(cm_bf[j][g], bm_bf[j][g]) for g in range(SSD_GROUPS)] for j in order}
    rm = {j: _dt_row_major(dtf_ref[_rows(j), :], bias_ref[...], alog_ref[...]) for j in order}
    hm = {j: _dt_head_major(dtTf_ref[:, _rows(j)], biasT_ref[...], alogT_ref[...]) for j in order}

    _ssd_bwd_sweep(xsb_ref, bcb_ref, dtb_ref, bias_ref, alog_ref, eb_ref, yob_ref, stb_ref)

    fac = {j: _state_factors(*rm[j], ef_ref) for j in order}
    xs = {j: xs_bf[j].astype(F32) for j in order}
    bm, coef_exp = {}, {}
    for j in order:
        bm[j], cm = _split_bc(bc_bf[j].astype(F32))
        cbd = jnp.where((lane & (nh - 1)) < hpg, jnp.sum(cm[0] * bm[j][0], axis=1, keepdims=True),
                        jnp.sum(cm[1] * bm[j][1], axis=1, keepdims=True))
        coef_exp[j] = _expand(dskip_ref[...] + cbd * rm[j][0], eb_ref)
    y_off = _state_sweep(stf_ref, order, xs, bm, cm_bf, fac)

    causal = _tri(False)
    left = lane < SSD_HEAD_DIM
    zero_bf = jnp.zeros((L, LANES), BF16)
    acum2 = {j: rm[j][1] * _LOG2E for j in order}
    y_loc = {j: [] for j in order}
    for k in range(nh // 2):
        g = (2 * k) // hpg
        for j in order:
            lhs = []
            for h in (2 * k, 2 * k + 1):
                col_f = jnp.broadcast_to(acum2[j][:, h:h + 1], (L, L))
                col_b = jnp.broadcast_to(acum2[j][:, nh + h:nh + h + 1], (L, L))
                e = jnp.where(causal, col_f - hm[j][0][h:h + 1, :], col_b - hm[j][1][h:h + 1, :])
                lhs.append((cb[j][g] * jnp.exp2(e)).astype(BF16))
            xs_pair = xs_bf[j][:, k * LANES:(k + 1) * LANES]
            rhs = jnp.concatenate([jnp.where(left, xs_pair, zero_bf), jnp.where(left, zero_bf, xs_pair)], axis=0)
            y_loc[j].append(_dot(jnp.concatenate(lhs, axis=1), rhs))
    for j in order:
        y = jnp.concatenate(y_loc[j], axis=1) + coef_exp[j] * xs[j] + y_off[j]
        y_ref[_rows(j), :] = y.astype(BF16)


def _ssd(proj, dt, dtT, p, e_mats, lay):
    t = proj.shape[0]
    ns = _SSD_TILE_CHUNKS
    assert lay.per_seq_first % ns == 0 and lay.per_seq_second % ns == 0 and lay.n_first % ns == 0
    nb = lay.n_blocks // ns
    rows = ns * BLK
    xs_blk, bc_blk = COL_XS // SSD_INNER, COL_BC // SSD_BC_WIDTH
    state = pltpu.VMEM((SSD_GROUPS, SSD_STATE, SSD_INNER // SSD_GROUPS), F32)

    def rev(c):
        return nb - 1 - c

    return pl.pallas_call(
        functools.partial(_ssd_body, lay=lay),
        grid=(nb,),
        in_specs=[
            pl.BlockSpec((rows, SSD_INNER), lambda c: (c, xs_blk)),
            pl.BlockSpec((rows, SSD_BC_WIDTH), lambda c: (c, bc_blk)),
            pl.BlockSpec((rows, LANES), lambda c: (c, 0)),
            pl.BlockSpec((DT_REP, rows), lambda c: (0, c)),
            pl.BlockSpec((rows, SSD_INNER), lambda c: (rev(c), xs_blk)),
            pl.BlockSpec((rows, SSD_BC_WIDTH), lambda c: (rev(c), bc_blk)),
            pl.BlockSpec((rows, LANES), lambda c: (rev(c), 0)),
            _const_spec((1, LANES)), _const_spec((1, LANES)),
            _const_spec((DT_REP, 1)), _const_spec((DT_REP, 1)),
            _const_spec((1, LANES)),
            _const_spec((LANES, SSD_INNER)), _const_spec((LANES, SSD_INNER)),
        ],
        out_specs=[
            pl.BlockSpec((rows, SSD_INNER), lambda c: (c, 0)),
            pl.BlockSpec((rows, SSD_INNER), lambda c: (rev(c), 0)),
        ],
        out_shape=[jax.ShapeDtypeStruct((t, SSD_INNER), BF16), jax.ShapeDtypeStruct((t, SSD_INNER), BF16)],
        scratch_shapes=[state, state],
        compiler_params=_params(("arbitrary",)),
        name="ssd",
    )(proj, proj, dt, dtT, proj, proj, dt, p["dtb"], p["alog"], p["dtbT"], p["alogT"], p["dskip"],
      e_mats[0], e_mats[1])


_ROW_TILE = 512
_MERGE_COLS = 256


def _attn_merge_body(q_ref, kvp_ref, kvm_ref, kvn_ref, bias_ref, sink_ref,
                     xa_ref, xb_ref, gu_ref, gv_ref, yc_ref, yob_ref, z_ref, g0_ref, g1_ref, g2_ref, ng_ref, ws_ref, bs_ref,
                     wa_ref, wb_ref, wc_ref, wo_ref, ha_ref, hb_ref, ya_ref, m_ref, *, lay, n_a, n_tiles):
    i = pl.program_id(0)
    slot = lax.rem(i, 2)

    @pl.when(i == 0)
    def _():
        ya_ref[...] = jnp.zeros(ya_ref.shape, BF16)

    ya = ya_ref[1 - slot]
    gw = GMLP_WIDTH // GMLP_GROUPS
    chunks = []
    for c in range(_ROW_TILE // BLK):
        r = slice(c * BLK, (c + 1) * BLK)
        chunks.append(jnp.concatenate(
            [gu_ref[r, g * gw:(g + 1) * gw].astype(F32)
             * (_dot(ws_ref[g], gv_ref[r, g * gw:(g + 1) * gw]) + bs_ref[:, g * gw:(g + 1) * gw])
             for g in range(GMLP_GROUPS)], axis=1))
    yb = jnp.concatenate(chunks, axis=0).astype(BF16)
    y = (yc_ref[...].astype(F32) + yob_ref[...].astype(F32)) * z_ref[...].astype(F32)
    sw = SSD_INNER // SSD_GROUPS
    yn = jnp.concatenate([_rms(y[:, g * sw:(g + 1) * sw], ng_ref[:, g * sw:(g + 1) * sw])
                          for g in range(SSD_GROUPS)], axis=1).astype(BF16)

    acc = {}

    def piece(c, k):
        cols = slice(c * _MERGE_COLS, (c + 1) * _MERGE_COLS)

        def run():
            if k == 0:
                acc[c] = g0_ref[:, cols].astype(F32) * _dot(ya, wa_ref[:, cols])
            elif k == 1:
                acc[c] = acc[c] + g1_ref[:, cols].astype(F32) * _dot(yb, wb_ref[:, cols])
            else:
                m_ref[:, cols] = (acc[c] + g2_ref[:, cols].astype(F32) * _dot(yn, wc_ref[:, cols])).astype(BF16)
        return run

    pieces = [piece(c, k) for c in range(D_MODEL // _MERGE_COLS) for k in range(3)]
    n_items = _ATTN_TILE_BLOCKS * ATTN_HEADS // 2
    done = [0]

    def after_item(idx):
        while done[0] < (idx + 1) * len(pieces) // n_items:
            pieces[done[0]]()
            done[0] += 1

    def put_block(jb, val):
        ya_ref[slot, jb * BLK:(jb + 1) * BLK, :] = val

    _attention_tile(jnp.minimum(i, n_tiles - 1), q_ref, kvp_ref, kvm_ref, kvn_ref, bias_ref, sink_ref,
                    put_block, after_item, lay)

    m = m_ref[...]
    in_first = jnp.maximum(i - 1, 0) < n_a

    @pl.when(in_first)
    def _():
        ha_ref[...] = xa_ref[...] + _dot(m, wo_ref[...])

    @pl.when(jnp.logical_not(in_first))
    def _():
        hb_ref[...] = xb_ref[...] + _dot(m, wo_ref[...])


def _attn_merge(xa, xb, yc, yob, proj, bias, p, lay):
    tm = _ROW_TILE
    nt = _ATTN_TILE_BLOCKS
    assert tm == nt * BLK
    assert lay.per_seq_first % nt == 0 and lay.per_seq_second % nt == 0 and lay.n_first % nt == 0
    n_a, n_b = xa.shape[0] // tm, xb.shape[0] // tm
    n_tiles = n_a + n_b
    g_blk = COL_GATE // D_MODEL
    kv_w = 2 * ATTN_KV_WIDTH
    kv_blk = COL_KV // kv_w
    last_blk = lay.n_blocks - 1

    def att(i):
        return jnp.minimum(i, n_tiles - 1)

    def mrg(i):
        return jnp.maximum(i - 1, 0)

    def row(width, col_blk=0):
        return pl.BlockSpec((tm, width), lambda i: (mrg(i), col_blk))

    stream = (pl.BlockSpec((tm, D_MODEL), lambda i: (jnp.minimum(mrg(i), n_a - 1), 0)),
              pl.BlockSpec((tm, D_MODEL), lambda i: (jnp.clip(mrg(i) - n_a, 0, n_b - 1), 0)))
    return pl.pallas_call(
        functools.partial(_attn_merge_body, lay=lay, n_a=n_a, n_tiles=n_tiles),
        grid=(n_tiles + 1,),
        in_specs=[
            pl.BlockSpec((tm, ATTN_Q_WIDTH), lambda i: (att(i), COL_Q // ATTN_Q_WIDTH)),
            pl.BlockSpec((BLK, kv_w), lambda i: (jnp.maximum(att(i) * nt - 1, 0), kv_blk)),
            pl.BlockSpec((tm, kv_w), lambda i: (att(i), kv_blk)),
            pl.BlockSpec((BLK, kv_w), lambda i: (jnp.minimum(att(i) * nt + nt, last_blk), kv_blk)),
            _const_spec((ATTN_HEADS // 2, 3 * BLK, 2 * BLK)),
            _const_spec((ATTN_HEADS // 2, 2 * BLK)),
            *stream, row(GMLP_WIDTH, COL_GU // GMLP_WIDTH), row(GMLP_WIDTH, COL_GV // GMLP_WIDTH),
            row(SSD_INNER), row(SSD_INNER), row(SSD_INNER, COL_Z // SSD_INNER),
            row(D_MODEL, g_blk), row(D_MODEL, g_blk + 1), row(D_MODEL, g_blk + 2),
            _const_spec((1, SSD_INNER)),
            _const_spec((GMLP_GROUPS, BLK, BLK)), _const_spec((BLK, GMLP_WIDTH)),
            _const_spec((ATTN_Q_WIDTH, D_MODEL)), _const_spec((GMLP_WIDTH, D_MODEL)),
            _const_spec((SSD_INNER, D_MODEL)), _const_spec((D_MODEL, D_MODEL)),
        ],
        out_specs=list(stream),
        out_shape=[jax.ShapeDtypeStruct(xa.shape, F32), jax.ShapeDtypeStruct(xb.shape, F32)],
        scratch_shapes=[pltpu.VMEM((2, tm, ATTN_Q_WIDTH), BF16), pltpu.VMEM((tm, D_MODEL), BF16)],
        compiler_params=_params(("arbitrary",)),
        name="attn_merge",
    )(proj, proj, proj, proj, bias, p["sink"],
      xa, xb, proj, proj, yc, yob, proj, proj, proj, proj, p["ssd_norm_g"], p["w_s"], p["b_s"],
      p["w_a"], p["w_b"], p["w_c"], p["w_o"])


_FF_CHUNK = 1024


def _ffn_body(ha_ref, hb_ref, g_ref, w1_ref, w2_ref, oa_ref, ob_ref, *, n_a):
    def run(h_ref, o_ref):
        h = h_ref[...]
        hn = _rms(h, g_ref[...]).astype(BF16)
        acc = h
        for c0 in range(0, FF_DIM, _FF_CHUNK):
            a = jnp.maximum(_dot(hn, w1_ref[:, c0:c0 + _FF_CHUNK]), 0.0)
            acc = acc + _dot((a * a).astype(BF16), w2_ref[c0:c0 + _FF_CHUNK, :])
        o_ref[...] = acc

    in_first = pl.program_id(0) < n_a
    pl.when(in_first)(lambda: run(ha_ref, oa_ref))
    pl.when(jnp.logical_not(in_first))(lambda: run(hb_ref, ob_ref))


def _ffn(ha, hb, g, w1, w2):
    tm = _ROW_TILE
    n_a, n_b = ha.shape[0] // tm, hb.shape[0] // tm
    stream = _two_stream_specs(tm, n_a, n_b)
    return pl.pallas_call(
        functools.partial(_ffn_body, n_a=n_a),
        grid=(n_a + n_b,),
        in_specs=[*stream, _const_spec((1, D_MODEL)), _const_spec((D_MODEL, FF_DIM)), _const_spec((FF_DIM, D_MODEL))],
        out_specs=list(stream),
        out_shape=[jax.ShapeDtypeStruct(ha.shape, F32), jax.ShapeDtypeStruct(hb.shape, F32)],
        compiler_params=_params(("arbitrary",)),
        name="ffn",
    )(ha, hb, g, w1, w2)


def _t5_bucket(rel):
    nb = NUM_BUCKETS // 2
    max_exact = nb // 2
    ret = jnp.where(rel > 0, nb, 0)
    n = jnp.abs(rel)
    n_safe = jnp.maximum(n, 1).astype(F32)
    large = max_exact + (jnp.log(n_safe / max_exact) / math.log(MAX_DISTANCE / max_exact)
                         * (nb - max_exact)).astype(jnp.int32)
    large = jnp.minimum(large, nb - 1)
    return ret + jnp.where(n < max_exact, n, large)


def _bias_table(rel_bias):
    qi = jnp.arange(BLK)[:, None]
    ki = jnp.arange(3 * BLK)[None, :]
    rel = ki - BLK - qi
    onehot = (_t5_bucket(rel)[..., None] == jnp.arange(NUM_BUCKETS)).astype(F32)
    bias = jnp.einsum("qkb,bh->hkq", onehot, rel_bias.astype(F32), precision=lax.Precision.HIGHEST) * _LOG2E
    bias = jnp.where((jnp.abs(rel) <= WINDOW).T[None], bias, NEG_INF)
    return jnp.concatenate([bias[0::2], bias[1::2]], axis=-1)


def _expand_matrix(d):
    k = np.arange(LANES)[:, None]
    col = np.arange(SSD_INNER)[None, :]
    hit = ((k % DT_REP) == d * SSD_HEADS + col // SSD_HEAD_DIM)
    return jnp.asarray(hit, dtype=BF16)


_SLAB_SOURCES = ((_R_Z, _R_XS), (_R_XS, _R_B), (_R_GATE, _R_END), (_R_Q, _R_K), (_R_GU, _R_GV), (_R_GV, _R_Z),
                 (_R_B, _R_DT), (_R_K, _R_GU))
_W_ROWS = 256


def _permute_body(w_ref, o_ref):
    dst = 0
    for a, b in _SLAB_SOURCES:
        o_ref[0, :, dst:dst + b - a] = w_ref[0, :, a:b].astype(BF16)
        dst += b - a


def _permute_w_in(w_in):
    depth = w_in.shape[0]
    assert sum(b - a for a, b in _SLAB_SOURCES) == P_WIDTH
    return pl.pallas_call(
        _permute_body,
        grid=(depth, D_MODEL // _W_ROWS),
        in_specs=[pl.BlockSpec((1, _W_ROWS, _R_END), lambda l, r: (l, r, 0))],
        out_specs=pl.BlockSpec((1, _W_ROWS, P_WIDTH), lambda l, r: (l, r, 0)),
        out_shape=jax.ShapeDtypeStruct((depth, D_MODEL, P_WIDTH), BF16),
        compiler_params=_params(("parallel", "parallel")),
        name="permute_w_in",
    )(w_in)


def _prep_layer_params(w_in, norm_mix_g, q_norm_g, k_norm_g, attn_sink, gmlp_ln_g, gmlp_ln_b, w_spatial,
                       b_spatial, conv_w, conv_b, dt_bias, a_log, d_skip, ssd_norm_g, w_up_attn, w_up_gmlp,
                       w_up_ssd, w_out, norm_ff_g, w_ff1, w_ff2):
    depth = w_in.shape[0]

    w_main = _permute_w_in(w_in)
    w_dt = w_in[:, :, _R_DT:_R_GATE]
    rep = LANES // DT_REP
    gw = GMLP_WIDTH // GMLP_GROUPS

    def dt_lanes(v):
        return jnp.tile(v.reshape(depth, 1, DT_REP), (1, 1, rep))

    return dict(
        norm_mix_g=norm_mix_g[:, None, :],
        w_main=w_main,
        w_dt=jnp.tile(w_dt, (1, 1, rep)).astype(BF16),
        w_dtT=jnp.swapaxes(w_dt, 1, 2).astype(BF16),
        kscale=jnp.tile(q_norm_g * k_norm_g * (HEAD_DIM ** -0.5 * _LOG2E), (1, ATTN_KV_HEADS))[:, None, :],
        sink=jnp.repeat(attn_sink * _LOG2E, BLK, axis=-1).reshape(depth, ATTN_HEADS // 2, 2 * BLK),
        ln_g=gmlp_ln_g[:, None, :], ln_b=gmlp_ln_b[:, None, :],
        w_s=w_spatial.astype(BF16),
        b_s=jnp.repeat(jnp.swapaxes(b_spatial, 1, 2), gw, axis=-1),
        conv_w=conv_w, conv_b=conv_b[:, None, :],
        dtb=dt_lanes(dt_bias), alog=dt_lanes(a_log),
        dtbT=dt_bias.reshape(depth, DT_REP, 1), alogT=a_log.reshape(depth, DT_REP, 1),
        dskip=dt_lanes(jnp.concatenate([d_skip, d_skip], axis=-1)),
        ssd_norm_g=ssd_norm_g[:, None, :],
        w_a=w_up_attn.astype(BF16), w_b=w_up_gmlp.astype(BF16), w_c=w_up_ssd.astype(BF16),
        w_o=w_out.astype(BF16),
        norm_ff_g=norm_ff_g[:, None, :],
        w1=w_ff1.astype(BF16), w2=w_ff2.astype(BF16),
    )


def _layer(xa, xb, p, bias, e_mats, lay):
    proj, dt, dtT = _inproj(xa, xb, p, lay)
    yc, yob = _ssd(proj, dt, dtT, p, e_mats, lay)
    ha, hb = _attn_merge(xa, xb, yc, yob, proj, bias, p, lay)
    return _ffn(ha, hb, p["norm_ff_g"], p["w1"], p["w2"])


def _encoder(x_first, x_second, rel_bias, layer_params):
    b1, s1, _ = x_first.shape
    b2, s2, _ = x_second.shape
    t1, t2 = b1 * s1, b2 * s2
    lay = Layout(n_blocks=(t1 + t2) // BLK, n_first=t1 // BLK, per_seq_first=s1 // BLK,
                 per_seq_second=s2 // BLK)
    bias = _bias_table(rel_bias)
    e_mats = (_expand_matrix(0), _expand_matrix(1))
    params = _prep_layer_params(**layer_params)

    xa, xb = x_first.reshape(t1, D_MODEL), x_second.reshape(t2, D_MODEL)
    for layer in range(layer_params["w_in"].shape[0]):
        xa, xb = _layer(xa, xb, {k: v[layer] for k, v in params.items()}, bias, e_mats, lay)
    return xa.reshape(b1, s1, D_MODEL), xb.reshape(b2, s2, D_MODEL)


def kernel(x_prompt, x_sample, rel_bias, norm_mix_g, w_in, q_norm_g, k_norm_g, attn_sink, gmlp_ln_g, gmlp_ln_b,
           w_spatial, b_spatial, conv_w, conv_b, dt_bias, a_log, d_skip, ssd_norm_g, w_up_attn, w_up_gmlp,
           w_up_ssd, w_out, norm_ff_g, w_ff1, w_ff2):
    layer_params = dict(
        w_in=w_in, norm_mix_g=norm_mix_g, q_norm_g=q_norm_g, k_norm_g=k_norm_g, attn_sink=attn_sink,
        gmlp_ln_g=gmlp_ln_g, gmlp_ln_b=gmlp_ln_b, w_spatial=w_spatial, b_spatial=b_spatial, conv_w=conv_w,
        conv_b=conv_b, dt_bias=dt_bias, a_log=a_log, d_skip=d_skip, ssd_norm_g=ssd_norm_g,
        w_up_attn=w_up_attn, w_up_gmlp=w_up_gmlp, w_up_ssd=w_up_ssd, w_out=w_out, norm_ff_g=norm_ff_g,
        w_ff1=w_ff1, w_ff2=w_ff2)
    return _encoder(x_prompt, x_sample, rel_bias, layer_params)
```

```python
import functools
import math
from typing import NamedTuple

import numpy as np
import jax
import jax.numpy as jnp
from jax import lax
from jax.experimental import pallas as pl
from jax.experimental.pallas import tpu as pltpu

F32 = jnp.float32
BF16 = jnp.bfloat16

D_MODEL = 1024
ATTN_HEADS = 8
ATTN_KV_HEADS = 2
HEAD_DIM = 64
ATTN_Q_WIDTH = ATTN_HEADS * HEAD_DIM
ATTN_KV_WIDTH = ATTN_KV_HEADS * HEAD_DIM
WINDOW = 128
NUM_BUCKETS = 32
MAX_DISTANCE = 128
GMLP_WIDTH = 512
GMLP_GROUPS = 4
SSD_INNER = 1024
SSD_HEAD_DIM = 64
SSD_HEADS = SSD_INNER // SSD_HEAD_DIM
SSD_GROUPS = 2
SSD_STATE = 128
N_DIR = 2
FF_DIM = 4 * D_MODEL
EPS = 1e-6
NEG_INF = -1e30

BLK = 128
LANES = 128
BF16_SUBLANES = 16
VMEM_LIMIT = 56 * 1024 * 1024
_LOG2E = math.log2(math.e)

COL_Z = 0
COL_XS = 1024
COL_GATE = 2048
COL_Q = 5120
COL_GU = 5632
COL_GV = 6144
COL_BC = 6656
COL_KV = 7168
P_WIDTH = 7424
DT_REP = 32
SSD_BC_WIDTH = 2 * SSD_GROUPS * SSD_STATE

_R_Q, _R_K, _R_V, _R_GU, _R_GV, _R_Z, _R_XS, _R_B, _R_C, _R_DT, _R_GATE, _R_END = (
    0, 512, 640, 768, 1280, 1792, 2816, 3840, 4096, 4352, 4384, 7456)


class Layout(NamedTuple):
    n_blocks: int
    n_first: int
    per_seq_first: int
    per_seq_second: int


def _seq_pos(blk, lay):
    in_first = blk < lay.n_first
    pos = jnp.where(in_first, lax.rem(blk, lay.per_seq_first),
                    lax.rem(blk - lay.n_first, lay.per_seq_second))
    last = jnp.where(in_first, lay.per_seq_first - 1, lay.per_seq_second - 1)
    return pos == 0, pos == last


def _sigmoid(x):
    return 0.5 * jnp.tanh(0.5 * x) + 0.5


def _silu(x):
    return x * _sigmoid(x)


def _gelu(x):
    return x * (0.5 * (1.0 + jnp.tanh(math.sqrt(2.0 / math.pi) * (x + 0.044715 * (x * x * x)))))


def _softplus(x):
    return jnp.maximum(x, 0.0) + jnp.log1p(jnp.exp(-jnp.abs(x)))


def _rms(x, g):
    return x * lax.rsqrt(jnp.mean(x * x, axis=-1, keepdims=True) + EPS) * g


def _dot(a, b):
    return jnp.dot(a, b, preferred_element_type=F32)


def _dot_nt(a, b):
    return lax.dot_general(a, b, (((1,), (1,)), ((), ())), preferred_element_type=F32)


def _const_spec(shape):
    nd = len(shape)
    return pl.BlockSpec(shape, lambda *_: (0,) * nd)


def _params(semantics):
    return pltpu.CompilerParams(dimension_semantics=semantics, vmem_limit_bytes=VMEM_LIMIT)


_PROJ_TILE = 512
_PROJ_LOOKAHEAD = 1
_X_HALO = 8


def _split2(x):
    hi = x.astype(BF16)
    return hi, (x - hi.astype(F32)).astype(BF16)


def _inproj_body(xa_ref, xb_ref, xa_prev_ref, xb_prev_ref, xa_next_ref, xb_next_ref, g_ref, w_ref, wdt_ref, wdtT_ref,
                 cw_ref, cb_ref, lng_ref, lnb_ref, kscale_ref, qsum_ref, ksum_ref, o_ref, dt_ref, dtT_ref, *, lay):
    tm = _PROJ_TILE
    nblk = tm // BLK
    i = pl.program_id(0)
    keep_prev = jnp.where(_seq_pos(i * nblk, lay)[0], 0.0, 1.0)
    keep_next = jnp.where(_seq_pos(i * nblk + nblk - 1, lay)[1], 0.0, 1.0)
    in_first = i < lay.n_first // nblk

    row_limit = jnp.where(in_first, tm, 0)

    def pick(a_ref, b_ref):
        a = a_ref[...]
        return jnp.where(lax.broadcasted_iota(jnp.int32, a.shape, 0) < row_limit, a, b_ref[...])

    xn = _rms(pick(xa_ref, xb_ref), g_ref[...]).astype(BF16)
    xh = _rms(jnp.concatenate([pick(xa_prev_ref, xb_prev_ref), pick(xa_next_ref, xb_next_ref)], axis=0),
              g_ref[...]).astype(BF16)

    def put(c0, val):
        o_ref[:, c0:c0 + val.shape[1]] = val.astype(BF16)

    def conv_silu(k0):
        def post(c0, y):
            width = y.shape[1]
            yh = _dot(xh, w_ref[:, c0:c0 + width])
            halo_p = yh[_X_HALO - 1:_X_HALO] * keep_prev
            halo_n = yh[_X_HALO:_X_HALO + 1] * keep_next
            up = pltpu.roll(y, 1, axis=0)
            dn = pltpu.roll(y, tm - 1, axis=0)
            r8 = lax.broadcasted_iota(jnp.int32, (8, width), 0)
            y_m1 = jnp.concatenate([jnp.where(r8 == 0, halo_p, up[:8]), up[8:]], axis=0)
            y_p1 = jnp.concatenate([dn[:tm - 8], jnp.where(r8 == 7, halo_n, dn[tm - 8:])], axis=0)
            w = cw_ref[:, k0:k0 + width]
            put(c0, _silu(cb_ref[:, k0:k0 + width] + y_m1 * w[0:1] + y * w[1:2] + y_p1 * w[2:3]))
        return post

    def q_norm(c0, q):
        sq = jnp.concatenate(_split2(q * q), axis=1)

        def tail():
            qss = _dot(sq, qsum_ref[...])
            put(c0, q * lax.rsqrt(qss * (1.0 / HEAD_DIM) + EPS))
        return tail

    def gelu_ln(c0, y):
        vf = _gelu(y)
        vc = vf - jnp.mean(vf, axis=-1, keepdims=True)
        var = jnp.mean(vc * vc, axis=-1, keepdims=True)
        put(c0, vc * lax.rsqrt(var + EPS) * lng_ref[...] + lnb_ref[...])

    def kv_norm(c0, kv):
        kf = kv[:, :ATTN_KV_WIDTH]
        sq = jnp.concatenate(_split2(kf * kf), axis=1)
        put(c0 + ATTN_KV_WIDTH, kv[:, ATTN_KV_WIDTH:])

        def tail():
            kss = _dot(sq, ksum_ref[...])
            put(c0, kf * lax.rsqrt(kss * (1.0 / HEAD_DIM) + EPS) * kscale_ref[...])
        return tail

    def gate(k):
        return (COL_GATE + k * D_MODEL, D_MODEL, lambda c0, y: put(c0, _sigmoid(y)))

    items = [
        (COL_XS, SSD_INNER, conv_silu(0)),
        (COL_Z, SSD_INNER, lambda c0, y: put(c0, _silu(y))),
        (COL_BC, SSD_BC_WIDTH, conv_silu(SSD_INNER)),
        gate(0),
        (COL_GV, GMLP_WIDTH, gelu_ln),
        gate(1),
        (COL_GU, GMLP_WIDTH, lambda c0, y: put(c0, _gelu(y))),
        gate(2),
        (COL_Q, ATTN_Q_WIDTH, q_norm),
        (COL_KV, 2 * ATTN_KV_WIDTH, kv_norm),
    ]

    def proj(k):
        c0, width, _ = items[k]
        return _dot(xn, w_ref[:, c0:c0 + width])

    pending = [proj(k) for k in range(_PROJ_LOOKAHEAD)]
    tail = None
    for k, (c0, _, post) in enumerate(items):
        y = pending.pop(0)
        if k + _PROJ_LOOKAHEAD < len(items):
            pending.append(proj(k + _PROJ_LOOKAHEAD))
        if tail is not None:
            tail()
        tail = post(c0, y)
    dt_ref[...] = _dot(xn, wdt_ref[...])
    dtT_ref[...] = _dot_nt(wdtT_ref[...], xn)
    if tail is not None:
        tail()


def _head_sum_matrix(width):
    k = np.arange(2 * width)[:, None]
    n = np.arange(width)[None, :]
    return jnp.asarray((k % width) // HEAD_DIM == n // HEAD_DIM, dtype=BF16)


def _two_stream_specs(tm, n_a, n_b):
    return (pl.BlockSpec((tm, D_MODEL), lambda i: (jnp.minimum(i, n_a - 1), 0)),
            pl.BlockSpec((tm, D_MODEL), lambda i: (jnp.clip(i - n_a, 0, n_b - 1), 0)))


def _inproj(xa, xb, p, lay):
    tm = _PROJ_TILE
    n_a, n_b = xa.shape[0] // tm, xb.shape[0] // tm
    t = xa.shape[0] + xb.shape[0]
    assert lay.per_seq_first % (tm // BLK) == 0 and lay.per_seq_second % (tm // BLK) == 0
    hal = tm // _X_HALO
    conv_w = SSD_INNER + SSD_BC_WIDTH

    def halo(tile0, n_tiles, shift):
        last = n_tiles * hal - 1
        return pl.BlockSpec((_X_HALO, D_MODEL),
                            lambda i: (jnp.clip((i - tile0 + shift) * hal - 1 + shift, 0, last), 0))

    return pl.pallas_call(
        functools.partial(_inproj_body, lay=lay),
        grid=(n_a + n_b,),
        in_specs=[
            *_two_stream_specs(tm, n_a, n_b),
            halo(0, n_a, 0), halo(n_a, n_b, 0), halo(0, n_a, 1), halo(n_a, n_b, 1),
            _const_spec((1, D_MODEL)),
            _const_spec((D_MODEL, P_WIDTH)),
            _const_spec((D_MODEL, LANES)),
            _const_spec((DT_REP, D_MODEL)),
            _const_spec((3, conv_w)), _const_spec((1, conv_w)),
            _const_spec((1, GMLP_WIDTH)), _const_spec((1, GMLP_WIDTH)),
            _const_spec((1, ATTN_KV_WIDTH)),
            _const_spec((2 * ATTN_Q_WIDTH, ATTN_Q_WIDTH)),
            _const_spec((2 * ATTN_KV_WIDTH, ATTN_KV_WIDTH)),
        ],
        out_specs=[
            pl.BlockSpec((tm, P_WIDTH), lambda i: (i, 0)),
            pl.BlockSpec((tm, LANES), lambda i: (i, 0)),
            pl.BlockSpec((DT_REP, tm), lambda i: (0, i)),
        ],
        out_shape=[
            jax.ShapeDtypeStruct((t, P_WIDTH), BF16),
            jax.ShapeDtypeStruct((t, LANES), F32),
            jax.ShapeDtypeStruct((DT_REP, t), F32),
        ],
        compiler_params=_params(("parallel",)),
        name="inproj",
    )(xa, xb, xa, xb, xa, xb, p["norm_mix_g"], p["w_main"], p["w_dt"], p["w_dtT"], p["conv_w"], p["conv_b"], p["ln_g"], p["ln_b"],
      p["kscale"], _head_sum_matrix(ATTN_Q_WIDTH), _head_sum_matrix(ATTN_KV_WIDTH))


_ATTN_TILE_BLOCKS = 4
_PV_DELAY = 1
_ATTN_LOOKAHEAD = 4


def _attention_tile(tile, q_ref, kvp_ref, kvm_ref, kvn_ref, bias_ref, sink_ref, put_block, after_item, lay):
    nt = _ATTN_TILE_BLOCKS
    grp = ATTN_HEADS // ATTN_KV_HEADS
    kv = jnp.concatenate([kvp_ref[...], kvm_ref[...], kvn_ref[...]], axis=0)
    kn = [kv[:, g * HEAD_DIM:(g + 1) * HEAD_DIM] for g in range(ATTN_KV_HEADS)]
    vT = jnp.transpose(kv[:, ATTN_KV_WIDTH:].astype(F32)).astype(BF16)
    neg_prev = jnp.where(_seq_pos(tile * nt, lay)[0], NEG_INF, 0.0)
    neg_next = jnp.where(_seq_pos(tile * nt + nt - 1, lay)[1], NEG_INF, 0.0)

    qbs = [q_ref[jb * BLK:(jb + 1) * BLK, :] for jb in range(nt)]

    def scores(jb, hp):
        slab = qbs[jb][:, hp * LANES:(hp + 1) * LANES]
        q2 = jnp.concatenate([slab[:, :HEAD_DIM], slab[:, HEAD_DIM:]], axis=0)
        return _dot_nt(kn[2 * hp // grp][jb * BLK:(jb + 3) * BLK], q2) + bias_ref[hp]

    n_pairs = ATTN_HEADS // 2
    items = [(jb, hp) for jb in range(nt) for hp in range(n_pairs)]
    pending = [scores(*it) for it in items[:_ATTN_LOOKAHEAD]]
    outs = []
    waiting = []

    def value_matmul():
        jb, hp, p, inv = waiting.pop(0)
        g = 2 * hp // grp
        o = _dot(vT[g * HEAD_DIM:(g + 1) * HEAD_DIM, jb * BLK:(jb + 3) * BLK], p) * inv
        outs.extend([o[:, :BLK], o[:, BLK:]])
        if hp == n_pairs - 1:
            put_block(jb, jnp.transpose(jnp.concatenate(outs, axis=0)).astype(BF16))
            outs.clear()

    for idx, (jb, hp) in enumerate(items):
        s = pending.pop(0)
        if idx + _ATTN_LOOKAHEAD < len(items):
            pending.append(scores(*items[idx + _ATTN_LOOKAHEAD]))
        if len(waiting) > _PV_DELAY:
            value_matmul()
        if jb == 0:
            s = jnp.concatenate([s[:BLK] + neg_prev, s[BLK:]], axis=0)
        if jb == nt - 1:
            s = jnp.concatenate([s[:2 * BLK], s[2 * BLK:] + neg_next], axis=0)
        snk = sink_ref[hp:hp + 1, :]
        m = jnp.maximum(jnp.max(s, axis=0, keepdims=True), snk)
        p = jnp.exp2(s - m)
        denom = jnp.sum(p, axis=0, keepdims=True) + jnp.exp2(snk - m)
        waiting.append((jb, hp, p.astype(BF16), 1.0 / denom))
        after_item(idx)
    while waiting:
        value_matmul()


_SPLIT_TERMS = 4


def _lane_split(x, terms=_SPLIT_TERMS):
    rep = lax.broadcasted_iota(jnp.int32, x.shape, 1) // DT_REP
    out = jnp.zeros(x.shape, BF16)
    rem = x
    for r in range(terms):
        term = rem.astype(BF16)
        out = jnp.where(rep == r, term, out)
        rem = rem - term.astype(F32)
    return out


def _sublane_split(x):
    terms, rem = [], x
    for _ in range(_SPLIT_TERMS):
        term = rem.astype(BF16)
        terms.append(term)
        rem = rem - term.astype(F32)
    return jnp.concatenate(terms, axis=0)


_EXPAND_TERMS = 2


def _expand(x, e_ref):
    return _dot(_lane_split(x, _EXPAND_TERMS), e_ref[...])


def _tri(upper):
    r = lax.broadcasted_iota(jnp.int32, (BLK, BLK), 0)
    c = lax.broadcasted_iota(jnp.int32, (BLK, BLK), 1)
    return (c >= r) if upper else (c <= r)


def _ones_where(mask):
    return jnp.where(mask, 1.0, 0.0).astype(BF16)


_SSD_TILE_CHUNKS = 8


def _rows(j):
    return slice(j * BLK, (j + 1) * BLK)


def _dt_row_major(dt_raw, dtb, alog):
    L = BLK
    dt = _softplus(dt_raw + dtb)
    sp = _lane_split(dt * (-jnp.exp(alog)))
    lane = lax.broadcasted_iota(jnp.int32, (L, LANES), 1)
    is_fwd = (lane & (DT_REP - 1)) < SSD_HEADS
    part = jnp.where(is_fwd, _dot(_ones_where(_tri(False)), sp), _dot(_ones_where(_tri(True)), sp))
    acum = part
    for r in range(1, _SPLIT_TERMS):
        acum = acum + pltpu.roll(part, r * DT_REP, axis=1)
    a_end = jnp.where(is_fwd[0:1], acum[L - 1:L], acum[0:1])
    return dt, acum, a_end


def _dt_head_major(dtT_raw, dtbT, alogT):
    nh = SSD_HEADS
    dtT = _softplus(dtT_raw + dtbT)
    spT = _sublane_split(dtT * (-jnp.exp(alogT)))
    pf = _dot(spT, _ones_where(_tri(True)))
    pb = _dot(spT, _ones_where(_tri(False)))
    acumT_f = pf[0:nh]
    acumT_b = pb[nh:2 * nh]
    for r in range(1, _SPLIT_TERMS):
        acumT_f = acumT_f + pf[r * DT_REP:r * DT_REP + nh]
        acumT_b = acumT_b + pb[r * DT_REP + nh:(r + 1) * DT_REP]
    log2_dtT = jnp.log(dtT) * _LOG2E
    return acumT_f * _LOG2E - log2_dtT[0:nh], acumT_b * _LOG2E - log2_dtT[nh:2 * nh]


def _state_factors(dt, acum, a_end, e_ref):
    ea_exp = _expand(jnp.exp(acum), e_ref)
    w_exp = _expand(dt * jnp.exp(a_end - acum), e_ref)
    cd_exp = _expand(jnp.broadcast_to(jnp.exp(a_end), (8, LANES)), e_ref)[0:1]
    return ea_exp, w_exp, cd_exp


def _state_sweep(st_ref, order, xs, bm, cm_bf, fac):
    gw = SSD_INNER // SSD_GROUPS
    upd = {}
    for j in order:
        wx = (xs[j] * fac[j][1]).astype(BF16)
        upd[j] = [_dot(jnp.transpose(bm[j][g]).astype(BF16), wx[:, g * gw:(g + 1) * gw])
                  for g in range(SSD_GROUPS)]
    sg = [st_ref[g] for g in range(SSD_GROUPS)]
    out = {}
    for j in order:
        y_off = [_dot(cm_bf[j][g], sg[g].astype(BF16)) for g in range(SSD_GROUPS)]
        neg_cd = -fac[j][2]
        sg = [upd[j][g] - sg[g] * neg_cd[:, g * gw:(g + 1) * gw] for g in range(SSD_GROUPS)]
        out[j] = jnp.concatenate(y_off, axis=1) * fac[j][0]
    for g in range(SSD_GROUPS):
        st_ref[g] = sg[g]
    return out


def _split_bc(bc):
    n = SSD_STATE
    return ([bc[:, g * n:(g + 1) * n] for g in range(SSD_GROUPS)],
            [bc[:, (SSD_GROUPS + g) * n:(SSD_GROUPS + g + 1) * n] for g in range(SSD_GROUPS)])


def _ssd_bwd_sweep(xs_ref, bc_ref, dt_ref, dtb_ref, alog_ref, eb_ref, yo_ref, st_ref):
    order = list(reversed(range(_SSD_TILE_CHUNKS)))
    rm = {j: _dt_row_major(dt_ref[_rows(j), :], dtb_ref[...], alog_ref[...]) for j in order}
    fac = {j: _state_factors(*rm[j], eb_ref) for j in order}
    xs = {j: xs_ref[_rows(j), :].astype(F32) for j in order}
    bm = {j: _split_bc(bc_ref[_rows(j), :].astype(F32))[0] for j in order}
    cm_bf = {j: _split_bc(bc_ref[_rows(j), :])[1] for j in order}
    y_off = _state_sweep(st_ref, order, xs, bm, cm_bf, fac)
    for j in order:
        yo_ref[_rows(j), :] = y_off[j].astype(BF16)


def _ssd_body(xsf_ref, bcf_ref, dtf_ref, dtTf_ref, xsb_ref, bcb_ref, dtb_ref, bias_ref, alog_ref, biasT_ref, alogT_ref,
              dskip_ref, ef_ref, eb_ref, y_ref, yob_ref, stf_ref, stb_ref, *, lay):
    ns = _SSD_TILE_CHUNKS
    i = pl.program_id(0)
    n_steps = lay.n_blocks // ns

    @pl.when(_seq_pos(i * ns, lay)[0])
    def _():
        stf_ref[...] = jnp.zeros(stf_ref.shape, F32)

    @pl.when(_seq_pos((n_steps - 1 - i) * ns + ns - 1, lay)[1])
    def _():
        stb_ref[...] = jnp.zeros(stb_ref.shape, F32)

    L, nh = BLK, SSD_HEADS
    hpg = nh // SSD_GROUPS
    order = list(range(ns))
    lane = lax.broadcasted_iota(jnp.int32, (L, LANES), 1)
    xs_bf = {j: xsf_ref[_rows(j), :] for j in order}
    bc_bf = {j: bcf_ref[_rows(j), :] for j in order}
    bm_bf = {j: _split_bc(bc_bf[j])[0] for j in order}
    cm_bf = {j: _split_bc(bc_bf[j])[1] for j in order}
    cb = {j: [_dot_nt(cm_bf[j][g], bm_bf[j][g]) for g in range(SSD_GROUPS)] for j in order}
    rm = {j: _dt_row_major(dtf_ref[_rows(j), :], bias_ref[...], alog_ref[...]) for j in order}
    hm = {j: _dt_head_major(dtTf_ref[:, _rows(j)], biasT_ref[...], alogT_ref[...]) for j in order}

    _ssd_bwd_sweep(xsb_ref, bcb_ref, dtb_ref, bias_ref, alog_ref, eb_ref, yob_ref, stb_ref)

    fac = {j: _state_factors(*rm[j], ef_ref) for j in order}
    xs = {j: xs_bf[j].astype(F32) for j in order}
    bm, coef_exp = {}, {}
    for j in order:
        bm[j], cm = _split_bc(bc_bf[j].astype(F32))
        cbd = jnp.where((lane & (nh - 1)) < hpg, jnp.sum(cm[0] * bm[j][0], axis=1, keepdims=True),
                        jnp.sum(cm[1] * bm[j][1], axis=1, keepdims=True))
        coef_exp[j] = _expand(dskip_ref[...] + cbd * rm[j][0], eb_ref)
    y_off = _state_sweep(stf_ref, order, xs, bm, cm_bf, fac)

    causal = _tri(False)
    left = lane < SSD_HEAD_DIM
    zero_bf = jnp.zeros((L, LANES), BF16)
    acum2 = {j: rm[j][1] * _LOG2E for j in order}
    y_loc = {j: [] for j in order}
    for k in range(nh // 2):
        g = (2 * k) // hpg
        for j in order:
            lhs = []
            for h in (2 * k, 2 * k + 1):
                col_f = jnp.broadcast_to(acum2[j][:, h:h + 1], (L, L))
                col_b = jnp.broadcast_to(acum2[j][:, nh + h:nh + h + 1], (L, L))
                e = jnp.where(causal, col_f - hm[j][0][h:h + 1, :], col_b - hm[j][1][h:h + 1, :])
                lhs.append((cb[j][g] * jnp.exp2(e)).astype(BF16))
            xs_pair = xs_bf[j][:, k * LANES:(k + 1) * LANES]
            rhs = jnp.concatenate([jnp.where(left, xs_pair, zero_bf), jnp.where(left, zero_bf, xs_pair)], axis=0)
            y_loc[j].append(_dot(jnp.concatenate(lhs, axis=1), rhs))
    for j in order:
        y = jnp.concatenate(y_loc[j], axis=1) + coef_exp[j] * xs[j] + y_off[j]
        y_ref[_rows(j), :] = y.astype(BF16)


def _ssd(proj, dt, dtT, p, e_mats, lay):
    t = proj.shape[0]
    ns = _SSD_TILE_CHUNKS
    assert lay.per_seq_first % ns == 0 and lay.per_seq_second % ns == 0 and lay.n_first % ns == 0
    nb = lay.n_blocks // ns
    rows = ns * BLK
    xs_blk, bc_blk = COL_XS // SSD_INNER, COL_BC // SSD_BC_WIDTH
    state = pltpu.VMEM((SSD_GROUPS, SSD_STATE, SSD_INNER // SSD_GROUPS), F32)

    def rev(c):
        return nb - 1 - c

    return pl.pallas_call(
        functools.partial(_ssd_body, lay=lay),
        grid=(nb,),
        in_specs=[
            pl.BlockSpec((rows, SSD_INNER), lambda c: (c, xs_blk)),
            pl.BlockSpec((rows, SSD_BC_WIDTH), lambda c: (c, bc_blk)),
            pl.BlockSpec((rows, LANES), lambda c: (c, 0)),
            pl.BlockSpec((DT_REP, rows), lambda c: (0, c)),
            pl.BlockSpec((rows, SSD_INNER), lambda c: (rev(c), xs_blk)),
            pl.BlockSpec((rows, SSD_BC_WIDTH), lambda c: (rev(c), bc_blk)),
            pl.BlockSpec((rows, LANES), lambda c: (rev(c), 0)),
            _const_spec((1, LANES)), _const_spec((1, LANES)),
            _const_spec((DT_REP, 1)), _const_spec((DT_REP, 1)),
            _const_spec((1, LANES)),
            _const_spec((LANES, SSD_INNER)), _const_spec((LANES, SSD_INNER)),
        ],
        out_specs=[
            pl.BlockSpec((rows, SSD_INNER), lambda c: (c, 0)),
            pl.BlockSpec((rows, SSD_INNER), lambda c: (rev(c), 0)),
        ],
        out_shape=[jax.ShapeDtypeStruct((t, SSD_INNER), BF16), jax.ShapeDtypeStruct((t, SSD_INNER), BF16)],
        scratch_shapes=[state, state],
        compiler_params=_params(("arbitrary",)),
        name="ssd",
    )(proj, proj, dt, dtT, proj, proj, dt, p["dtb"], p["alog"], p["dtbT"], p["alogT"], p["dskip"],
      e_mats[0], e_mats[1])


_ROW_TILE = 512
_MERGE_COLS = 256


def _attn_merge_body(q_ref, kvp_ref, kvm_ref, kvn_ref, bias_ref, sink_ref,
                     xa_ref, xb_ref, gu_ref, gv_ref, yc_ref, yob_ref, z_ref, g0_ref, g1_ref, g2_ref, ng_ref, ws_ref, bs_ref,
                     wa_ref, wb_ref, wc_ref, wo_ref, ha_ref, hb_ref, ya_ref, m_ref, *, lay, n_a, n_tiles):
    i = pl.program_id(0)
    slot = lax.rem(i, 2)

    @pl.when(i == 0)
    def _():
        ya_ref[...] = jnp.zeros(ya_ref.shape, BF16)

    ya = ya_ref[1 - slot]
    gw = GMLP_WIDTH // GMLP_GROUPS
    chunks = []
    for c in range(_ROW_TILE // BLK):
        r = slice(c * BLK, (c + 1) * BLK)
        chunks.append(jnp.concatenate(
            [gu_ref[r, g * gw:(g + 1) * gw].astype(F32)
             * (_dot(ws_ref[g], gv_ref[r, g * gw:(g + 1) * gw]) + bs_ref[:, g * gw:(g + 1) * gw])
             for g in range(GMLP_GROUPS)], axis=1))
    yb = jnp.concatenate(chunks, axis=0).astype(BF16)
    y = (yc_ref[...].astype(F32) + yob_ref[...].astype(F32)) * z_ref[...].astype(F32)
    sw = SSD_INNER // SSD_GROUPS
    yn = jnp.concatenate([_rms(y[:, g * sw:(g + 1) * sw], ng_ref[:, g * sw:(g + 1) * sw])
                          for g in range(SSD_GROUPS)], axis=1).astype(BF16)

    acc = {}

    def piece(c, k):
        cols = slice(c * _MERGE_COLS, (c + 1) * _MERGE_COLS)

        def run():
            if k == 0:
                acc[c] = g0_ref[:, cols].astype(F32) * _dot(ya, wa_ref[:, cols])
            elif k == 1:
                acc[c] = acc[c] + g1_ref[:, cols].astype(F32) * _dot(yb, wb_ref[:, cols])
            else:
                m_ref[:, cols] = (acc[c] + g2_ref[:, cols].astype(F32) * _dot(yn, wc_ref[:, cols])).astype(BF16)
        return run

    pieces = [piece(c, k) for c in range(D_MODEL // _MERGE_COLS) for k in range(3)]
    n_items = _ATTN_TILE_BLOCKS * ATTN_HEADS // 2
    done = [0]

    def after_item(idx):
        while done[0] < (idx + 1) * len(pieces) // n_items:
            pieces[done[0]]()
            done[0] += 1

    def put_block(jb, val):
        ya_ref[slot, jb * BLK:(jb + 1) * BLK, :] = val

    _attention_tile(jnp.minimum(i, n_tiles - 1), q_ref, kvp_ref, kvm_ref, kvn_ref, bias_ref, sink_ref,
                    put_block, after_item, lay)

    m = m_ref[...]
    in_first = jnp.maximum(i - 1, 0) < n_a

    @pl.when(in_first)
    def _():
        ha_ref[...] = xa_ref[...] + _dot(m, wo_ref[...])

    @pl.when(jnp.logical_not(in_first))
    def _():
        hb_ref[...] = xb_ref[...] + _dot(m, wo_ref[...])


def _attn_merge(xa, xb, yc, yob, proj, bias, p, lay):
    tm = _ROW_TILE
    nt = _ATTN_TILE_BLOCKS
    assert tm == nt * BLK
    assert lay.per_seq_first % nt == 0 and lay.per_seq_second % nt == 0 and lay.n_first % nt == 0
    n_a, n_b = xa.shape[0] // tm, xb.shape[0] // tm
    n_tiles = n_a + n_b
    g_blk = COL_GATE // D_MODEL
    kv_w = 2 * ATTN_KV_WIDTH
    kv_blk = COL_KV // kv_w
    last_blk = lay.n_blocks - 1

    def att(i):
        return jnp.minimum(i, n_tiles - 1)

    def mrg(i):
        return jnp.maximum(i - 1, 0)

    def row(width, col_blk=0):
        return pl.BlockSpec((tm, width), lambda i: (mrg(i), col_blk))

    stream = (pl.BlockSpec((tm, D_MODEL), lambda i: (jnp.minimum(mrg(i), n_a - 1), 0)),
              pl.BlockSpec((tm, D_MODEL), lambda i: (jnp.clip(mrg(i) - n_a, 0, n_b - 1), 0)))
    return pl.pallas_call(
        functools.partial(_attn_merge_body, lay=lay, n_a=n_a, n_tiles=n_tiles),
        grid=(n_tiles + 1,),
        in_specs=[
            pl.BlockSpec((tm, ATTN_Q_WIDTH), lambda i: (att(i), COL_Q // ATTN_Q_WIDTH)),
            pl.BlockSpec((BLK, kv_w), lambda i: (jnp.maximum(att(i) * nt - 1, 0), kv_blk)),
            pl.BlockSpec((tm, kv_w), lambda i: (att(i), kv_blk)),
            pl.BlockSpec((BLK, kv_w), lambda i: (jnp.minimum(att(i) * nt + nt, last_blk), kv_blk)),
            _const_spec((ATTN_HEADS // 2, 3 * BLK, 2 * BLK)),
            _const_spec((ATTN_HEADS // 2, 2 * BLK)),
            *stream, row(GMLP_WIDTH, COL_GU // GMLP_WIDTH), row(GMLP_WIDTH, COL_GV // GMLP_WIDTH),
            row(SSD_INNER), row(SSD_INNER), row(SSD_INNER, COL_Z // SSD_INNER),
            row(D_MODEL, g_blk), row(D_MODEL, g_blk + 1), row(D_MODEL, g_blk + 2),
            _const_spec((1, SSD_INNER)),
            _const_spec((GMLP_GROUPS, BLK, BLK)), _const_spec((BLK, GMLP_WIDTH)),
            _const_spec((ATTN_Q_WIDTH, D_MODEL)), _const_spec((GMLP_WIDTH, D_MODEL)),
            _const_spec((SSD_INNER, D_MODEL)), _const_spec((D_MODEL, D_MODEL)),
        ],
        out_specs=list(stream),
        out_shape=[jax.ShapeDtypeStruct(xa.shape, F32), jax.ShapeDtypeStruct(xb.shape, F32)],
        scratch_shapes=[pltpu.VMEM((2, tm, ATTN_Q_WIDTH), BF16), pltpu.VMEM((tm, D_MODEL), BF16)],
        compiler_params=_params(("arbitrary",)),
        name="attn_merge",
    )(proj, proj, proj, proj, bias, p["sink"],
      xa, xb, proj, proj, yc, yob, proj, proj, proj, proj, p["ssd_norm_g"], p["w_s"], p["b_s"],
      p["w_a"], p["w_b"], p["w_c"], p["w_o"])


_FF_CHUNK = 1024


def _ffn_body(ha_ref, hb_ref, g_ref, w1_ref, w2_ref, oa_ref, ob_ref, *, n_a):
    def run(h_ref, o_ref):
        h = h_ref[...]
        hn = _rms(h, g_ref[...]).astype(BF16)
        acc = h
        for c0 in range(0, FF_DIM, _FF_CHUNK):
            a = jnp.maximum(_dot(hn, w1_ref[:, c0:c0 + _FF_CHUNK]), 0.0)
            acc = acc + _dot((a * a).astype(BF16), w2_ref[c0:c0 + _FF_CHUNK, :])
        o_ref[...] = acc

    in_first = pl.program_id(0) < n_a
    pl.when(in_first)(lambda: run(ha_ref, oa_ref))
    pl.when(jnp.logical_not(in_first))(lambda: run(hb_ref, ob_ref))


def _ffn(ha, hb, g, w1, w2):
    tm = _ROW_TILE
    n_a, n_b = ha.shape[0] // tm, hb.shape[0] // tm
    stream = _two_stream_specs(tm, n_a, n_b)
    return pl.pallas_call(
        functools.partial(_ffn_body, n_a=n_a),
        grid=(n_a + n_b,),
        in_specs=[*stream, _const_spec((1, D_MODEL)), _const_spec((D_MODEL, FF_DIM)), _const_spec((FF_DIM, D_MODEL))],
        out_specs=list(stream),
        out_shape=[jax.ShapeDtypeStruct(ha.shape, F32), jax.ShapeDtypeStruct(hb.shape, F32)],
        compiler_params=_params(("arbitrary",)),
        name="ffn",
    )(ha, hb, g, w1, w2)


def _t5_bucket(rel):
    nb = NUM_BUCKETS // 2
    max_exact = nb // 2
    ret = jnp.where(rel > 0, nb, 0)
    n = jnp.abs(rel)
    n_safe = jnp.maximum(n, 1).astype(F32)
    large = max_exact + (jnp.log(n_safe / max_exact) / math.log(MAX_DISTANCE / max_exact)
                         * (nb - max_exact)).astype(jnp.int32)
    large = jnp.minimum(large, nb - 1)
    return ret + jnp.where(n < max_exact, n, large)


def _bias_table(rel_bias):
    qi = jnp.arange(BLK)[:, None]
    ki = jnp.arange(3 * BLK)[None, :]
    rel = ki - BLK - qi
    onehot = (_t5_bucket(rel)[..., None] == jnp.arange(NUM_BUCKETS)).astype(F32)
    bias = jnp.einsum("qkb,bh->hkq", onehot, rel_bias.astype(F32), precision=lax.Precision.HIGHEST) * _LOG2E
    bias = jnp.where((jnp.abs(rel) <= WINDOW).T[None], bias, NEG_INF)
    return jnp.concatenate([bias[0::2], bias[1::2]], axis=-1)


def _expand_matrix(d):
    k = np.arange(LANES)[:, None]
    col = np.arange(SSD_INNER)[None, :]
    hit = ((k % DT_REP) == d * SSD_HEADS + col // SSD_HEAD_DIM)
    return jnp.asarray(hit, dtype=BF16)


def _prep_layer_params(w_in, norm_mix_g, q_norm_g, k_norm_g, attn_sink, gmlp_ln_g, gmlp_ln_b, w_spatial,
                       b_spatial, conv_w, conv_b, dt_bias, a_log, d_skip, ssd_norm_g, w_up_attn, w_up_gmlp,
                       w_up_ssd, w_out, norm_ff_g, w_ff1, w_ff2):
    depth = w_in.shape[0]

    def cols(a, b):
        return w_in[:, :, a:b]

    w_main = jnp.concatenate([
        cols(_R_Z, _R_XS), cols(_R_XS, _R_B), cols(_R_GATE, _R_END), cols(_R_Q, _R_K),
        cols(_R_GU, _R_GV), cols(_R_GV, _R_Z), cols(_R_B, _R_DT), cols(_R_K, _R_GU)], axis=-1).astype(BF16)
    w_dt = cols(_R_DT, _R_GATE)
    rep = LANES // DT_REP
    gw = GMLP_WIDTH // GMLP_GROUPS

    def dt_lanes(v):
        return jnp.tile(v.reshape(depth, 1, DT_REP), (1, 1, rep))

    return dict(
        norm_mix_g=norm_mix_g[:, None, :],
        w_main=w_main,
        w_dt=jnp.tile(w_dt, (1, 1, rep)).astype(BF16),
        w_dtT=jnp.swapaxes(w_dt, 1, 2).astype(BF16),
        kscale=jnp.tile(q_norm_g * k_norm_g * (HEAD_DIM ** -0.5 * _LOG2E), (1, ATTN_KV_HEADS))[:, None, :],
        sink=jnp.repeat(attn_sink * _LOG2E, BLK, axis=-1).reshape(depth, ATTN_HEADS // 2, 2 * BLK),
        ln_g=gmlp_ln_g[:, None, :], ln_b=gmlp_ln_b[:, None, :],
        w_s=w_spatial.astype(BF16),
        b_s=jnp.repeat(jnp.swapaxes(b_spatial, 1, 2), gw, axis=-1),
        conv_w=conv_w, conv_b=conv_b[:, None, :],
        dtb=dt_lanes(dt_bias), alog=dt_lanes(a_log),
        dtbT=dt_bias.reshape(depth, DT_REP, 1), alogT=a_log.reshape(depth, DT_REP, 1),
        dskip=dt_lanes(jnp.concatenate([d_skip, d_skip], axis=-1)),
        ssd_norm_g=ssd_norm_g[:, None, :],
        w_a=w_up_attn.astype(BF16), w_b=w_up_gmlp.astype(BF16), w_c=w_up_ssd.astype(BF16),
        w_o=w_out.astype(BF16),
        norm_ff_g=norm_ff_g[:, None, :],
        w1=w_ff1.astype(BF16), w2=w_ff2.astype(BF16),
    )


def _layer(xa, xb, p, bias, e_mats, lay):
    proj, dt, dtT = _inproj(xa, xb, p, lay)
    yc, yob = _ssd(proj, dt, dtT, p, e_mats, lay)
    ha, hb = _attn_merge(xa, xb, yc, yob, proj, bias, p, lay)
    return _ffn(ha, hb, p["norm_ff_g"], p["w1"], p["w2"])


def _encoder(x_first, x_second, rel_bias, layer_params):
    b1, s1, _ = x_first.shape
    b2, s2, _ = x_second.shape
    t1, t2 = b1 * s1, b2 * s2
    lay = Layout(n_blocks=(t1 + t2) // BLK, n_first=t1 // BLK, per_seq_first=s1 // BLK,
                 per_seq_second=s2 // BLK)
    bias = _bias_table(rel_bias)
    e_mats = (_expand_matrix(0), _expand_matrix(1))
    params = _prep_layer_params(**layer_params)

    xa, xb = x_first.reshape(t1, D_MODEL), x_second.reshape(t2, D_MODEL)
    for layer in range(layer_params["w_in"].shape[0]):
        xa, xb = _layer(xa, xb, {k: v[layer] for k, v in params.items()}, bias, e_mats, lay)
    return xa.reshape(b1, s1, D_MODEL), xb.reshape(b2, s2, D_MODEL)


def kernel(x_prompt, x_sample, rel_bias, norm_mix_g, w_in, q_norm_g, k_norm_g, attn_sink, gmlp_ln_g, gmlp_ln_b,
           w_spatial, b_spatial, conv_w, conv_b, dt_bias, a_log, d_skip, ssd_norm_g, w_up_attn, w_up_gmlp,
           w_up_ssd, w_out, norm_ff_g, w_ff1, w_ff2):
    layer_params = dict(
        w_in=w_in, norm_mix_g=norm_mix_g, q_norm_g=q_norm_g, k_norm_g=k_norm_g, attn_sink=attn_sink,
        gmlp_ln_g=gmlp_ln_g, gmlp_ln_b=gmlp_ln_b, w_spatial=w_spatial, b_spatial=b_spatial, conv_w=conv_w,
        conv_b=conv_b, dt_bias=dt_bias, a_log=a_log, d_skip=d_skip, ssd_norm_g=ssd_norm_g,
        w_up_attn=w_up_attn, w_up_gmlp=w_up_gmlp, w_up_ssd=w_up_ssd, w_out=w_out, norm_ff_g=norm_ff_g,
        w_ff1=w_ff1, w_ff2=w_ff2)
    return _encoder(x_prompt, x_sample, rel_bias, layer_params)
```

```python
import functools
import math
from typing import NamedTuple

import numpy as np
import jax
import jax.numpy as jnp
from jax import lax
from jax.experimental import pallas as pl
from jax.experimental.pallas import tpu as pltpu

F32 = jnp.float32
BF16 = jnp.bfloat16

D_MODEL = 1024
ATTN_HEADS = 8
ATTN_KV_HEADS = 2
HEAD_DIM = 64
ATTN_Q_WIDTH = ATTN_HEADS * HEAD_DIM
ATTN_KV_WIDTH = ATTN_KV_HEADS * HEAD_DIM
WINDOW = 128
NUM_BUCKETS = 32
MAX_DISTANCE = 128
GMLP_WIDTH = 512
GMLP_GROUPS = 4
SSD_INNER = 1024
SSD_HEAD_DIM = 64
SSD_HEADS = SSD_INNER // SSD_HEAD_DIM
SSD_GROUPS = 2
SSD_STATE = 128
N_DIR = 2
FF_DIM = 4 * D_MODEL
EPS = 1e-6
NEG_INF = -1e30

BLK = 128
LANES = 128
BF16_SUBLANES = 16
VMEM_LIMIT = 56 * 1024 * 1024
_LOG2E = math.log2(math.e)

COL_Z = 0
COL_XS = 1024
COL_GATE = 2048
COL_Q = 5120
COL_GU = 5632
COL_GV = 6144
COL_BC = 6656
COL_KV = 7168
P_WIDTH = 7424
DT_REP = 32
SSD_BC_WIDTH = 2 * SSD_GROUPS * SSD_STATE

_R_Q, _R_K, _R_V, _R_GU, _R_GV, _R_Z, _R_XS, _R_B, _R_C, _R_DT, _R_GATE, _R_END = (
    0, 512, 640, 768, 1280, 1792, 2816, 3840, 4096, 4352, 4384, 7456)


class Layout(NamedTuple):
    n_blocks: int
    n_first: int
    per_seq_first: int
    per_seq_second: int


def _seq_pos(blk, lay):
    in_first = blk < lay.n_first
    pos = jnp.where(in_first, lax.rem(blk, lay.per_seq_first),
                    lax.rem(blk - lay.n_first, lay.per_seq_second))
    last = jnp.where(in_first, lay.per_seq_first - 1, lay.per_seq_second - 1)
    return pos == 0, pos == last


def _sigmoid(x):
    return 0.5 * jnp.tanh(0.5 * x) + 0.5


def _silu(x):
    return x * _sigmoid(x)


def _gelu(x):
    return x * (0.5 * (1.0 + jnp.tanh(math.sqrt(2.0 / math.pi) * (x + 0.044715 * (x * x * x)))))


def _softplus(x):
    return jnp.maximum(x, 0.0) + jnp.log1p(jnp.exp(-jnp.abs(x)))


def _rms(x, g):
    return x * lax.rsqrt(jnp.mean(x * x, axis=-1, keepdims=True) + EPS) * g


def _dot(a, b):
    return jnp.dot(a, b, preferred_element_type=F32)


def _dot_nt(a, b):
    return lax.dot_general(a, b, (((1,), (1,)), ((), ())), preferred_element_type=F32)


def _const_spec(shape):
    nd = len(shape)
    return pl.BlockSpec(shape, lambda *_: (0,) * nd)


def _params(semantics):
    return pltpu.CompilerParams(dimension_semantics=semantics, vmem_limit_bytes=VMEM_LIMIT)


_PROJ_TILE = 512
_PROJ_LOOKAHEAD = 1
_X_HALO = 8


def _split2(x):
    hi = x.astype(BF16)
    return hi, (x - hi.astype(F32)).astype(BF16)


def _inproj_body(xa_ref, xb_ref, xa_prev_ref, xb_prev_ref, xa_next_ref, xb_next_ref, g_ref, w_ref, wdt_ref, wdtT_ref,
                 cw_ref, cb_ref, lng_ref, lnb_ref, kscale_ref, qsum_ref, ksum_ref, o_ref, dt_ref, dtT_ref, *, lay):
    tm = _PROJ_TILE
    nblk = tm // BLK
    i = pl.program_id(0)
    keep_prev = jnp.where(_seq_pos(i * nblk, lay)[0], 0.0, 1.0)
    keep_next = jnp.where(_seq_pos(i * nblk + nblk - 1, lay)[1], 0.0, 1.0)
    in_first = i < lay.n_first // nblk

    row_limit = jnp.where(in_first, tm, 0)

    def pick(a_ref, b_ref):
        a = a_ref[...]
        return jnp.where(lax.broadcasted_iota(jnp.int32, a.shape, 0) < row_limit, a, b_ref[...])

    xn = _rms(pick(xa_ref, xb_ref), g_ref[...]).astype(BF16)
    xh = _rms(jnp.concatenate([pick(xa_prev_ref, xb_prev_ref), pick(xa_next_ref, xb_next_ref)], axis=0),
              g_ref[...]).astype(BF16)

    def put(c0, val):
        o_ref[:, c0:c0 + val.shape[1]] = val.astype(BF16)

    def conv_silu(k0):
        def post(c0, y):
            width = y.shape[1]
            yh = _dot(xh, w_ref[:, c0:c0 + width])
            halo_p = yh[_X_HALO - 1:_X_HALO] * keep_prev
            halo_n = yh[_X_HALO:_X_HALO + 1] * keep_next
            up = pltpu.roll(y, 1, axis=0)
            dn = pltpu.roll(y, tm - 1, axis=0)
            r8 = lax.broadcasted_iota(jnp.int32, (8, width), 0)
            y_m1 = jnp.concatenate([jnp.where(r8 == 0, halo_p, up[:8]), up[8:]], axis=0)
            y_p1 = jnp.concatenate([dn[:tm - 8], jnp.where(r8 == 7, halo_n, dn[tm - 8:])], axis=0)
            w = cw_ref[:, k0:k0 + width]
            put(c0, _silu(cb_ref[:, k0:k0 + width] + y_m1 * w[0:1] + y * w[1:2] + y_p1 * w[2:3]))
        return post

    def q_norm(c0, q):
        sq = jnp.concatenate(_split2(q * q), axis=1)

        def tail():
            qss = _dot(sq, qsum_ref[...])
            put(c0, q * lax.rsqrt(qss * (1.0 / HEAD_DIM) + EPS))
        return tail

    def gelu_ln(c0, y):
        vf = _gelu(y)
        vc = vf - jnp.mean(vf, axis=-1, keepdims=True)
        var = jnp.mean(vc * vc, axis=-1, keepdims=True)
        put(c0, vc * lax.rsqrt(var + EPS) * lng_ref[...] + lnb_ref[...])

    def kv_norm(c0, kv):
        kf = kv[:, :ATTN_KV_WIDTH]
        sq = jnp.concatenate(_split2(kf * kf), axis=1)
        put(c0 + ATTN_KV_WIDTH, kv[:, ATTN_KV_WIDTH:])

        def tail():
            kss = _dot(sq, ksum_ref[...])
            put(c0, kf * lax.rsqrt(kss * (1.0 / HEAD_DIM) + EPS) * kscale_ref[...])
        return tail

    def gate(k):
        return (COL_GATE + k * D_MODEL, D_MODEL, lambda c0, y: put(c0, _sigmoid(y)))

    items = [
        (COL_XS, SSD_INNER, conv_silu(0)),
        (COL_Z, SSD_INNER, lambda c0, y: put(c0, _silu(y))),
        (COL_BC, SSD_BC_WIDTH, conv_silu(SSD_INNER)),
        gate(0),
        (COL_GV, GMLP_WIDTH, gelu_ln),
        gate(1),
        (COL_GU, GMLP_WIDTH, lambda c0, y: put(c0, _gelu(y))),
        gate(2),
        (COL_Q, ATTN_Q_WIDTH, q_norm),
        (COL_KV, 2 * ATTN_KV_WIDTH, kv_norm),
    ]

    def proj(k):
        c0, width, _ = items[k]
        return _dot(xn, w_ref[:, c0:c0 + width])

    pending = [proj(k) for k in range(_PROJ_LOOKAHEAD)]
    tail = None
    for k, (c0, _, post) in enumerate(items):
        y = pending.pop(0)
        if k + _PROJ_LOOKAHEAD < len(items):
            pending.append(proj(k + _PROJ_LOOKAHEAD))
        if tail is not None:
            tail()
        tail = post(c0, y)
    dt_ref[...] = _dot(xn, wdt_ref[...])
    dtT_ref[...] = _dot_nt(wdtT_ref[...], xn)
    if tail is not None:
        tail()


def _head_sum_matrix(width):
    k = np.arange(2 * width)[:, None]
    n = np.arange(width)[None, :]
    return jnp.asarray((k % width) // HEAD_DIM == n // HEAD_DIM, dtype=BF16)


def _two_stream_specs(tm, n_a, n_b):
    return (pl.BlockSpec((tm, D_MODEL), lambda i: (jnp.minimum(i, n_a - 1), 0)),
            pl.BlockSpec((tm, D_MODEL), lambda i: (jnp.clip(i - n_a, 0, n_b - 1), 0)))


def _inproj(xa, xb, p, lay):
    tm = _PROJ_TILE
    n_a, n_b = xa.shape[0] // tm, xb.shape[0] // tm
    t = xa.shape[0] + xb.shape[0]
    assert lay.per_seq_first % (tm // BLK) == 0 and lay.per_seq_second % (tm // BLK) == 0
    hal = tm // _X_HALO
    conv_w = SSD_INNER + SSD_BC_WIDTH

    def halo(tile0, n_tiles, shift):
        last = n_tiles * hal - 1
        return pl.BlockSpec((_X_HALO, D_MODEL),
                            lambda i: (jnp.clip((i - tile0 + shift) * hal - 1 + shift, 0, last), 0))

    return pl.pallas_call(
        functools.partial(_inproj_body, lay=lay),
        grid=(n_a + n_b,),
        in_specs=[
            *_two_stream_specs(tm, n_a, n_b),
            halo(0, n_a, 0), halo(n_a, n_b, 0), halo(0, n_a, 1), halo(n_a, n_b, 1),
            _const_spec((1, D_MODEL)),
            _const_spec((D_MODEL, P_WIDTH)),
            _const_spec((D_MODEL, LANES)),
            _const_spec((DT_REP, D_MODEL)),
            _const_spec((3, conv_w)), _const_spec((1, conv_w)),
            _const_spec((1, GMLP_WIDTH)), _const_spec((1, GMLP_WIDTH)),
            _const_spec((1, ATTN_KV_WIDTH)),
            _const_spec((2 * ATTN_Q_WIDTH, ATTN_Q_WIDTH)),
            _const_spec((2 * ATTN_KV_WIDTH, ATTN_KV_WIDTH)),
        ],
        out_specs=[
            pl.BlockSpec((tm, P_WIDTH), lambda i: (i, 0)),
            pl.BlockSpec((tm, LANES), lambda i: (i, 0)),
            pl.BlockSpec((DT_REP, tm), lambda i: (0, i)),
        ],
        out_shape=[
            jax.ShapeDtypeStruct((t, P_WIDTH), BF16),
            jax.ShapeDtypeStruct((t, LANES), F32),
            jax.ShapeDtypeStruct((DT_REP, t), F32),
        ],
        compiler_params=_params(("parallel",)),
        name="inproj",
    )(xa, xb, xa, xb, xa, xb, p["norm_mix_g"], p["w_main"], p["w_dt"], p["w_dtT"], p["conv_w"], p["conv_b"], p["ln_g"], p["ln_b"],
      p["kscale"], _head_sum_matrix(ATTN_Q_WIDTH), _head_sum_matrix(ATTN_KV_WIDTH))


_ATTN_TILE_BLOCKS = 4
_PV_DELAY = 1
_ATTN_LOOKAHEAD = 4


def _attention_tile(tile, q_ref, kvp_ref, kvm_ref, kvn_ref, bias_ref, sink_ref, put_block, after_item, lay):
    nt = _ATTN_TILE_BLOCKS
    grp = ATTN_HEADS // ATTN_KV_HEADS
    kv = jnp.concatenate([kvp_ref[...], kvm_ref[...], kvn_ref[...]], axis=0)
    kn = [kv[:, g * HEAD_DIM:(g + 1) * HEAD_DIM] for g in range(ATTN_KV_HEADS)]
    vT = jnp.transpose(kv[:, ATTN_KV_WIDTH:].astype(F32)).astype(BF16)
    neg_prev = jnp.where(_seq_pos(tile * nt, lay)[0], NEG_INF, 0.0)
    neg_next = jnp.where(_seq_pos(tile * nt + nt - 1, lay)[1], NEG_INF, 0.0)

    qbs = [q_ref[jb * BLK:(jb + 1) * BLK, :] for jb in range(nt)]

    def scores(jb, hp):
        slab = qbs[jb][:, hp * LANES:(hp + 1) * LANES]
        q2 = jnp.concatenate([slab[:, :HEAD_DIM], slab[:, HEAD_DIM:]], axis=0)
        return _dot_nt(kn[2 * hp // grp][jb * BLK:(jb + 3) * BLK], q2) + bias_ref[hp]

    n_pairs = ATTN_HEADS // 2
    items = [(jb, hp) for jb in range(nt) for hp in range(n_pairs)]
    pending = [scores(*it) for it in items[:_ATTN_LOOKAHEAD]]
    outs = []
    waiting = []

    def value_matmul():
        jb, hp, p, inv = waiting.pop(0)
        g = 2 * hp // grp
        o = _dot(vT[g * HEAD_DIM:(g + 1) * HEAD_DIM, jb * BLK:(jb + 3) * BLK], p) * inv
        outs.extend([o[:, :BLK], o[:, BLK:]])
        if hp == n_pairs - 1:
            put_block(jb, jnp.transpose(jnp.concatenate(outs, axis=0)).astype(BF16))
            outs.clear()

    for idx, (jb, hp) in enumerate(items):
        s = pending.pop(0)
        if idx + _ATTN_LOOKAHEAD < len(items):
            pending.append(scores(*items[idx + _ATTN_LOOKAHEAD]))
        if len(waiting) > _PV_DELAY:
            value_matmul()
        if jb == 0:
            s = jnp.concatenate([s[:BLK] + neg_prev, s[BLK:]], axis=0)
        if jb == nt - 1:
            s = jnp.concatenate([s[:2 * BLK], s[2 * BLK:] + neg_next], axis=0)
        snk = sink_ref[hp:hp + 1, :]
        m = jnp.maximum(jnp.max(s, axis=0, keepdims=True), snk)
        p = jnp.exp2(s - m)
        denom = jnp.sum(p, axis=0, keepdims=True) + jnp.exp2(snk - m)
        waiting.append((jb, hp, p.astype(BF16), 1.0 / denom))
        after_item(idx)
    while waiting:
        value_matmul()


_SPLIT_TERMS = 3
_REPLICAS = LANES // DT_REP


def _lane_split(x, terms=_SPLIT_TERMS):
    rep = lax.broadcasted_iota(jnp.int32, x.shape, 1) // DT_REP
    out = jnp.zeros(x.shape, BF16)
    rem = x
    for r in range(terms):
        term = rem.astype(BF16)
        out = jnp.where(rep == r, term, out)
        rem = rem - term.astype(F32)
    return out


def _sublane_split(x):
    terms, rem = [], x
    for _ in range(_SPLIT_TERMS):
        term = rem.astype(BF16)
        terms.append(term)
        rem = rem - term.astype(F32)
    return jnp.concatenate(terms, axis=0)


_EXPAND_TERMS = 2


def _expand(x, e_ref):
    return _dot(_lane_split(x, _EXPAND_TERMS), e_ref[...])


def _tri(upper):
    r = lax.broadcasted_iota(jnp.int32, (BLK, BLK), 0)
    c = lax.broadcasted_iota(jnp.int32, (BLK, BLK), 1)
    return (c >= r) if upper else (c <= r)


def _ones_where(mask):
    return jnp.where(mask, 1.0, 0.0).astype(BF16)


_SSD_TILE_CHUNKS = 4


def _rows(j):
    return slice(j * BLK, (j + 1) * BLK)


def _dt_row_major(dt_raw, dtb, alog):
    L = BLK
    dt = _softplus(dt_raw + dtb)
    sp = _lane_split(dt * (-jnp.exp(alog)))
    lane = lax.broadcasted_iota(jnp.int32, (L, LANES), 1)
    is_fwd = (lane & (DT_REP - 1)) < SSD_HEADS
    part = jnp.where(is_fwd, _dot(_ones_where(_tri(False)), sp), _dot(_ones_where(_tri(True)), sp))
    acum = part
    for r in range(1, _REPLICAS):
        acum = acum + pltpu.roll(part, r * DT_REP, axis=1)
    a_end = jnp.where(is_fwd[0:1], acum[L - 1:L], acum[0:1])
    return dt, acum, a_end


def _dt_head_major(dtT_raw, dtbT, alogT):
    nh = SSD_HEADS
    dtT = _softplus(dtT_raw + dtbT)
    spT = _sublane_split(dtT * (-jnp.exp(alogT)))
    pf = _dot(spT, _ones_where(_tri(True)))
    pb = _dot(spT, _ones_where(_tri(False)))
    acumT_f = pf[0:nh]
    acumT_b = pb[nh:2 * nh]
    for r in range(1, _SPLIT_TERMS):
        acumT_f = acumT_f + pf[r * DT_REP:r * DT_REP + nh]
        acumT_b = acumT_b + pb[r * DT_REP + nh:(r + 1) * DT_REP]
    log2_dtT = jnp.log(dtT) * _LOG2E
    return acumT_f * _LOG2E - log2_dtT[0:nh], acumT_b * _LOG2E - log2_dtT[nh:2 * nh]


def _state_factors(dt, acum, a_end, e_ref):
    ea_exp = _expand(jnp.exp(acum), e_ref)
    w_exp = _expand(dt * jnp.exp(a_end - acum), e_ref)
    cd_exp = _expand(jnp.broadcast_to(jnp.exp(a_end), (8, LANES)), e_ref)[0:1]
    return ea_exp, w_exp, cd_exp


def _state_sweep(st_ref, order, xs, bm, cm_bf, fac):
    gw = SSD_INNER // SSD_GROUPS
    upd = {}
    for j in order:
        wx = (xs[j] * fac[j][1]).astype(BF16)
        upd[j] = [_dot(jnp.transpose(bm[j][g]).astype(BF16), wx[:, g * gw:(g + 1) * gw])
                  for g in range(SSD_GROUPS)]
    sg = [st_ref[g] for g in range(SSD_GROUPS)]
    out = {}
    for j in order:
        y_off = [_dot(cm_bf[j][g], sg[g].astype(BF16)) for g in range(SSD_GROUPS)]
        neg_cd = -fac[j][2]
        sg = [upd[j][g] - sg[g] * neg_cd[:, g * gw:(g + 1) * gw] for g in range(SSD_GROUPS)]
        out[j] = jnp.concatenate(y_off, axis=1) * fac[j][0]
    for g in range(SSD_GROUPS):
        st_ref[g] = sg[g]
    return out


def _split_bc(bc):
    n = SSD_STATE
    return ([bc[:, g * n:(g + 1) * n] for g in range(SSD_GROUPS)],
            [bc[:, (SSD_GROUPS + g) * n:(SSD_GROUPS + g + 1) * n] for g in range(SSD_GROUPS)])


def _ssd_bwd_sweep(xs_ref, bc_ref, dt_ref, dtb_ref, alog_ref, eb_ref, yo_ref, st_ref):
    order = list(reversed(range(_SSD_TILE_CHUNKS)))
    rm = {j: _dt_row_major(dt_ref[_rows(j), :], dtb_ref[...], alog_ref[...]) for j in order}
    fac = {j: _state_factors(*rm[j], eb_ref) for j in order}
    xs = {j: xs_ref[_rows(j), :].astype(F32) for j in order}
    bm = {j: _split_bc(bc_ref[_rows(j), :].astype(F32))[0] for j in order}
    cm_bf = {j: _split_bc(bc_ref[_rows(j), :])[1] for j in order}
    y_off = _state_sweep(st_ref, order, xs, bm, cm_bf, fac)
    for j in order:
        yo_ref[_rows(j), :] = y_off[j].astype(BF16)


def _ssd_body(xsf_ref, bcf_ref, dtf_ref, dtTf_ref, xsb_ref, bcb_ref, dtb_ref, bias_ref, alog_ref, biasT_ref, alogT_ref,
              dskip_ref, ef_ref, eb_ref, y_ref, yob_ref, stf_ref, stb_ref, *, lay):
    ns = _SSD_TILE_CHUNKS
    i = pl.program_id(0)
    n_steps = lay.n_blocks // ns

    @pl.when(_seq_pos(i * ns, lay)[0])
    def _():
        stf_ref[...] = jnp.zeros(stf_ref.shape, F32)

    @pl.when(_seq_pos((n_steps - 1 - i) * ns + ns - 1, lay)[1])
    def _():
        stb_ref[...] = jnp.zeros(stb_ref.shape, F32)

    L, nh = BLK, SSD_HEADS
    hpg = nh // SSD_GROUPS
    order = list(range(ns))
    lane = lax.broadcasted_iota(jnp.int32, (L, LANES), 1)
    xs_bf = {j: xsf_ref[_rows(j), :] for j in order}
    bc_bf = {j: bcf_ref[_rows(j), :] for j in order}
    bm_bf = {j: _split_bc(bc_bf[j])[0] for j in order}
    cm_bf = {j: _split_bc(bc_bf[j])[1] for j in order}
    cb = {j: [_dot_nt(cm_bf[j][g], bm_bf[j][g]) for g in range(SSD_GROUPS)] for j in order}
    rm = {j: _dt_row_major(dtf_ref[_rows(j), :], bias_ref[...], alog_ref[...]) for j in order}
    hm = {j: _dt_head_major(dtTf_ref[:, _rows(j)], biasT_ref[...], alogT_ref[...]) for j in order}

    _ssd_bwd_sweep(xsb_ref, bcb_ref, dtb_ref, bias_ref, alog_ref, eb_ref, yob_ref, stb_ref)

    fac = {j: _state_factors(*rm[j], ef_ref) for j in order}
    xs = {j: xs_bf[j].astype(F32) for j in order}
    bm, coef_exp = {}, {}
    for j in order:
        bm[j], cm = _split_bc(bc_bf[j].astype(F32))
        cbd = jnp.where((lane & (nh - 1)) < hpg, jnp.sum(cm[0] * bm[j][0], axis=1, keepdims=True),
                        jnp.sum(cm[1] * bm[j][1], axis=1, keepdims=True))
        coef_exp[j] = _expand(dskip_ref[...] + cbd * rm[j][0], eb_ref)
    y_off = _state_sweep(stf_ref, order, xs, bm, cm_bf, fac)

    causal = _tri(False)
    left = lane < SSD_HEAD_DIM
    zero_bf = jnp.zeros((L, LANES), BF16)
    acum2 = {j: rm[j][1] * _LOG2E for j in order}
    y_loc = {j: [] for j in order}
    for j in order:
        for k in range(nh // 2):
            g = (2 * k) // hpg
            lhs = []
            for h in (2 * k, 2 * k + 1):
                col_f = jnp.broadcast_to(acum2[j][:, h:h + 1], (L, L))
                col_b = jnp.broadcast_to(acum2[j][:, nh + h:nh + h + 1], (L, L))
                e = jnp.where(causal, col_f - hm[j][0][h:h + 1, :], col_b - hm[j][1][h:h + 1, :])
                lhs.append((cb[j][g] * jnp.exp2(e)).astype(BF16))
            xs_pair = xs_bf[j][:, k * LANES:(k + 1) * LANES]
            rhs = jnp.concatenate([jnp.where(left, xs_pair, zero_bf), jnp.where(left, zero_bf, xs_pair)], axis=0)
            y_loc[j].append(_dot(jnp.concatenate(lhs, axis=1), rhs))
    for j in order:
        y = jnp.concatenate(y_loc[j], axis=1) + coef_exp[j] * xs[j] + y_off[j]
        y_ref[_rows(j), :] = y.astype(BF16)


def _ssd(proj, dt, dtT, p, e_mats, lay):
    t = proj.shape[0]
    ns = _SSD_TILE_CHUNKS
    assert lay.per_seq_first % ns == 0 and lay.per_seq_second % ns == 0 and lay.n_first % ns == 0
    nb = lay.n_blocks // ns
    rows = ns * BLK
    xs_blk, bc_blk = COL_XS // SSD_INNER, COL_BC // SSD_BC_WIDTH
    state = pltpu.VMEM((SSD_GROUPS, SSD_STATE, SSD_INNER // SSD_GROUPS), F32)

    def rev(c):
        return nb - 1 - c

    return pl.pallas_call(
        functools.partial(_ssd_body, lay=lay),
        grid=(nb,),
        in_specs=[
            pl.BlockSpec((rows, SSD_INNER), lambda c: (c, xs_blk)),
            pl.BlockSpec((rows, SSD_BC_WIDTH), lambda c: (c, bc_blk)),
            pl.BlockSpec((rows, LANES), lambda c: (c, 0)),
            pl.BlockSpec((DT_REP, rows), lambda c: (0, c)),
            pl.BlockSpec((rows, SSD_INNER), lambda c: (rev(c), xs_blk)),
            pl.BlockSpec((rows, SSD_BC_WIDTH), lambda c: (rev(c), bc_blk)),
            pl.BlockSpec((rows, LANES), lambda c: (rev(c), 0)),
            _const_spec((1, LANES)), _const_spec((1, LANES)),
            _const_spec((DT_REP, 1)), _const_spec((DT_REP, 1)),
            _const_spec((1, LANES)),
            _const_spec((LANES, SSD_INNER)), _const_spec((LANES, SSD_INNER)),
        ],
        out_specs=[
            pl.BlockSpec((rows, SSD_INNER), lambda c: (c, 0)),
            pl.BlockSpec((rows, SSD_INNER), lambda c: (rev(c), 0)),
        ],
        out_shape=[jax.ShapeDtypeStruct((t, SSD_INNER), BF16), jax.ShapeDtypeStruct((t, SSD_INNER), BF16)],
        scratch_shapes=[state, state],
        compiler_params=_params(("arbitrary",)),
        name="ssd",
    )(proj, proj, dt, dtT, proj, proj, dt, p["dtb"], p["alog"], p["dtbT"], p["alogT"], p["dskip"],
      e_mats[0], e_mats[1])


_ROW_TILE = 512
_MERGE_COLS = 256


def _attn_merge_body(q_ref, kvp_ref, kvm_ref, kvn_ref, bias_ref, sink_ref,
                     xa_ref, xb_ref, gu_ref, gv_ref, yc_ref, yob_ref, z_ref, g0_ref, g1_ref, g2_ref, ng_ref, ws_ref, bs_ref,
                     wa_ref, wb_ref, wc_ref, wo_ref, ha_ref, hb_ref, ya_ref, m_ref, *, lay, n_a, n_tiles):
    i = pl.program_id(0)
    slot = lax.rem(i, 2)

    @pl.when(i == 0)
    def _():
        ya_ref[...] = jnp.zeros(ya_ref.shape, BF16)

    ya = ya_ref[1 - slot]
    gw = GMLP_WIDTH // GMLP_GROUPS
    chunks = []
    for c in range(_ROW_TILE // BLK):
        r = slice(c * BLK, (c + 1) * BLK)
        chunks.append(jnp.concatenate(
            [gu_ref[r, g * gw:(g + 1) * gw].astype(F32)
             * (_dot(ws_ref[g], gv_ref[r, g * gw:(g + 1) * gw]) + bs_ref[:, g * gw:(g + 1) * gw])
             for g in range(GMLP_GROUPS)], axis=1))
    yb = jnp.concatenate(chunks, axis=0).astype(BF16)
    y = (yc_ref[...].astype(F32) + yob_ref[...].astype(F32)) * z_ref[...].astype(F32)
    sw = SSD_INNER // SSD_GROUPS
    yn = jnp.concatenate([_rms(y[:, g * sw:(g + 1) * sw], ng_ref[:, g * sw:(g + 1) * sw])
                          for g in range(SSD_GROUPS)], axis=1).astype(BF16)

    acc = {}

    def piece(c, k):
        cols = slice(c * _MERGE_COLS, (c + 1) * _MERGE_COLS)

        def run():
            if k == 0:
                acc[c] = g0_ref[:, cols].astype(F32) * _dot(ya, wa_ref[:, cols])
            elif k == 1:
                acc[c] = acc[c] + g1_ref[:, cols].astype(F32) * _dot(yb, wb_ref[:, cols])
            else:
                m_ref[:, cols] = (acc[c] + g2_ref[:, cols].astype(F32) * _dot(yn, wc_ref[:, cols])).astype(BF16)
        return run

    pieces = [piece(c, k) for c in range(D_MODEL // _MERGE_COLS) for k in range(3)]
    n_items = _ATTN_TILE_BLOCKS * ATTN_HEADS // 2
    done = [0]

    def after_item(idx):
        while done[0] < (idx + 1) * len(pieces) // n_items:
            pieces[done[0]]()
            done[0] += 1

    def put_block(jb, val):
        ya_ref[slot, jb * BLK:(jb + 1) * BLK, :] = val

    _attention_tile(jnp.minimum(i, n_tiles - 1), q_ref, kvp_ref, kvm_ref, kvn_ref, bias_ref, sink_ref,
                    put_block, after_item, lay)

    m = m_ref[...]
    in_first = jnp.maximum(i - 1, 0) < n_a

    @pl.when(in_first)
    def _():
        ha_ref[...] = xa_ref[...] + _dot(m, wo_ref[...])

    @pl.when(jnp.logical_not(in_first))
    def _():
        hb_ref[...] = xb_ref[...] + _dot(m, wo_ref[...])


def _attn_merge(xa, xb, yc, yob, proj, bias, p, lay):
    tm = _ROW_TILE
    nt = _ATTN_TILE_BLOCKS
    assert tm == nt * BLK
    assert lay.per_seq_first % nt == 0 and lay.per_seq_second % nt == 0 and lay.n_first % nt == 0
    n_a, n_b = xa.shape[0] // tm, xb.shape[0] // tm
    n_tiles = n_a + n_b
    g_blk = COL_GATE // D_MODEL
    kv_w = 2 * ATTN_KV_WIDTH
    kv_blk = COL_KV // kv_w
    last_blk = lay.n_blocks - 1

    def att(i):
        return jnp.minimum(i, n_tiles - 1)

    def mrg(i):
        return jnp.maximum(i - 1, 0)

    def row(width, col_blk=0):
        return pl.BlockSpec((tm, width), lambda i: (mrg(i), col_blk))

    stream = (pl.BlockSpec((tm, D_MODEL), lambda i: (jnp.minimum(mrg(i), n_a - 1), 0)),
              pl.BlockSpec((tm, D_MODEL), lambda i: (jnp.clip(mrg(i) - n_a, 0, n_b - 1), 0)))
    return pl.pallas_call(
        functools.partial(_attn_merge_body, lay=lay, n_a=n_a, n_tiles=n_tiles),
        grid=(n_tiles + 1,),
        in_specs=[
            pl.BlockSpec((tm, ATTN_Q_WIDTH), lambda i: (att(i), COL_Q // ATTN_Q_WIDTH)),
            pl.BlockSpec((BLK, kv_w), lambda i: (jnp.maximum(att(i) * nt - 1, 0), kv_blk)),
            pl.BlockSpec((tm, kv_w), lambda i: (att(i), kv_blk)),
            pl.BlockSpec((BLK, kv_w), lambda i: (jnp.minimum(att(i) * nt + nt, last_blk), kv_blk)),
            _const_spec((ATTN_HEADS // 2, 3 * BLK, 2 * BLK)),
            _const_spec((ATTN_HEADS // 2, 2 * BLK)),
            *stream, row(GMLP_WIDTH, COL_GU // GMLP_WIDTH), row(GMLP_WIDTH, COL_GV // GMLP_WIDTH),
            row(SSD_INNER), row(SSD_INNER), row(SSD_INNER, COL_Z // SSD_INNER),
            row(D_MODEL, g_blk), row(D_MODEL, g_blk + 1), row(D_MODEL, g_blk + 2),
            _const_spec((1, SSD_INNER)),
            _const_spec((GMLP_GROUPS, BLK, BLK)), _const_spec((BLK, GMLP_WIDTH)),
            _const_spec((ATTN_Q_WIDTH, D_MODEL)), _const_spec((GMLP_WIDTH, D_MODEL)),
            _const_spec((SSD_INNER, D_MODEL)), _const_spec((D_MODEL, D_MODEL)),
        ],
        out_specs=list(stream),
        out_shape=[jax.ShapeDtypeStruct(xa.shape, F32), jax.ShapeDtypeStruct(xb.shape, F32)],
        scratch_shapes=[pltpu.VMEM((2, tm, ATTN_Q_WIDTH), BF16), pltpu.VMEM((tm, D_MODEL), BF16)],
        compiler_params=_params(("arbitrary",)),
        name="attn_merge",
    )(proj, proj, proj, proj, bias, p["sink"],
      xa, xb, proj, proj, yc, yob, proj, proj, proj, proj, p["ssd_norm_g"], p["w_s"], p["b_s"],
      p["w_a"], p["w_b"], p["w_c"], p["w_o"])


_FF_CHUNK = 1024


def _ffn_body(ha_ref, hb_ref, g_ref, w1_ref, w2_ref, oa_ref, ob_ref, *, n_a):
    def run(h_ref, o_ref):
        h = h_ref[...]
        hn = _rms(h, g_ref[...]).astype(BF16)
        acc = h
        for c0 in range(0, FF_DIM, _FF_CHUNK):
            a = jnp.maximum(_dot(hn, w1_ref[:, c0:c0 + _FF_CHUNK]), 0.0)
            acc = acc + _dot((a * a).astype(BF16), w2_ref[c0:c0 + _FF_CHUNK, :])
        o_ref[...] = acc

    in_first = pl.program_id(0) < n_a
    pl.when(in_first)(lambda: run(ha_ref, oa_ref))
    pl.when(jnp.logical_not(in_first))(lambda: run(hb_ref, ob_ref))


def _ffn(ha, hb, g, w1, w2):
    tm = _ROW_TILE
    n_a, n_b = ha.shape[0] // tm, hb.shape[0] // tm
    stream = _two_stream_specs(tm, n_a, n_b)
    return pl.pallas_call(
        functools.partial(_ffn_body, n_a=n_a),
        grid=(n_a + n_b,),
        in_specs=[*stream, _const_spec((1, D_MODEL)), _const_spec((D_MODEL, FF_DIM)), _const_spec((FF_DIM, D_MODEL))],
        out_specs=list(stream),
        out_shape=[jax.ShapeDtypeStruct(ha.shape, F32), jax.ShapeDtypeStruct(hb.shape, F32)],
        compiler_params=_params(("arbitrary",)),
        name="ffn",
    )(ha, hb, g, w1, w2)


def _t5_bucket(rel):
    nb = NUM_BUCKETS // 2
    max_exact = nb // 2
    ret = jnp.where(rel > 0, nb, 0)
    n = jnp.abs(rel)
    n_safe = jnp.maximum(n, 1).astype(F32)
    large = max_exact + (jnp.log(n_safe / max_exact) / math.log(MAX_DISTANCE / max_exact)
                         * (nb - max_exact)).astype(jnp.int32)
    large = jnp.minimum(large, nb - 1)
    return ret + jnp.where(n < max_exact, n, large)


def _bias_table(rel_bias):
    qi = jnp.arange(BLK)[:, None]
    ki = jnp.arange(3 * BLK)[None, :]
    rel = ki - BLK - qi
    onehot = (_t5_bucket(rel)[..., None] == jnp.arange(NUM_BUCKETS)).astype(F32)
    bias = jnp.einsum("qkb,bh->hkq", onehot, rel_bias.astype(F32), precision=lax.Precision.HIGHEST) * _LOG2E
    bias = jnp.where((jnp.abs(rel) <= WINDOW).T[None], bias, NEG_INF)
    return jnp.concatenate([bias[0::2], bias[1::2]], axis=-1)


def _expand_matrix(d):
    k = np.arange(LANES)[:, None]
    col = np.arange(SSD_INNER)[None, :]
    hit = ((k % DT_REP) == d * SSD_HEADS + col // SSD_HEAD_DIM)
    return jnp.asarray(hit, dtype=BF16)


def _prep_layer_params(w_in, norm_mix_g, q_norm_g, k_norm_g, attn_sink, gmlp_ln_g, gmlp_ln_b, w_spatial,
                       b_spatial, conv_w, conv_b, dt_bias, a_log, d_skip, ssd_norm_g, w_up_attn, w_up_gmlp,
                       w_up_ssd, w_out, norm_ff_g, w_ff1, w_ff2):
    depth = w_in.shape[0]

    def cols(a, b):
        return w_in[:, :, a:b]

    w_main = jnp.concatenate([
        cols(_R_Z, _R_XS), cols(_R_XS, _R_B), cols(_R_GATE, _R_END), cols(_R_Q, _R_K),
        cols(_R_GU, _R_GV), cols(_R_GV, _R_Z), cols(_R_B, _R_DT), cols(_R_K, _R_GU)], axis=-1).astype(BF16)
    w_dt = cols(_R_DT, _R_GATE)
    rep = LANES // DT_REP
    gw = GMLP_WIDTH // GMLP_GROUPS

    def dt_lanes(v):
        return jnp.tile(v.reshape(depth, 1, DT_REP), (1, 1, rep))

    return dict(
        norm_mix_g=norm_mix_g[:, None, :],
        w_main=w_main,
        w_dt=jnp.tile(w_dt, (1, 1, rep)).astype(BF16),
        w_dtT=jnp.swapaxes(w_dt, 1, 2).astype(BF16),
        kscale=jnp.tile(q_norm_g * k_norm_g * (HEAD_DIM ** -0.5 * _LOG2E), (1, ATTN_KV_HEADS))[:, None, :],
        sink=jnp.repeat(attn_sink * _LOG2E, BLK, axis=-1).reshape(depth, ATTN_HEADS // 2, 2 * BLK),
        ln_g=gmlp_ln_g[:, None, :], ln_b=gmlp_ln_b[:, None, :],
        w_s=w_spatial.astype(BF16),
        b_s=jnp.repeat(jnp.swapaxes(b_spatial, 1, 2), gw, axis=-1),
        conv_w=conv_w, conv_b=conv_b[:, None, :],
        dtb=dt_lanes(dt_bias), alog=dt_lanes(a_log),
        dtbT=dt_bias.reshape(depth, DT_REP, 1), alogT=a_log.reshape(depth, DT_REP, 1),
        dskip=dt_lanes(jnp.concatenate([d_skip, d_skip], axis=-1)),
        ssd_norm_g=ssd_norm_g[:, None, :],
        w_a=w_up_attn.astype(BF16), w_b=w_up_gmlp.astype(BF16), w_c=w_up_ssd.astype(BF16),
        w_o=w_out.astype(BF16),
        norm_ff_g=norm_ff_g[:, None, :],
        w1=w_ff1.astype(BF16), w2=w_ff2.astype(BF16),
    )


def _layer(xa, xb, p, bias, e_mats, lay):
    proj, dt, dtT = _inproj(xa, xb, p, lay)
    yc, yob = _ssd(proj, dt, dtT, p, e_mats, lay)
    ha, hb = _attn_merge(xa, xb, yc, yob, proj, bias, p, lay)
    return _ffn(ha, hb, p["norm_ff_g"], p["w1"], p["w2"])


def _encoder(x_first, x_second, rel_bias, layer_params):
    b1, s1, _ = x_first.shape
    b2, s2, _ = x_second.shape
    t1, t2 = b1 * s1, b2 * s2
    lay = Layout(n_blocks=(t1 + t2) // BLK, n_first=t1 // BLK, per_seq_first=s1 // BLK,
                 per_seq_second=s2 // BLK)
    bias = _bias_table(rel_bias)
    e_mats = (_expand_matrix(0), _expand_matrix(1))
    params = _prep_layer_params(**layer_params)

    xa, xb = x_first.reshape(t1, D_MODEL), x_second.reshape(t2, D_MODEL)
    for layer in range(layer_params["w_in"].shape[0]):
        xa, xb = _layer(xa, xb, {k: v[layer] for k, v in params.items()}, bias, e_mats, lay)
    return xa.reshape(b1, s1, D_MODEL), xb.reshape(b2, s2, D_MODEL)


def kernel(x_prompt, x_sample, rel_bias, norm_mix_g, w_in, q_norm_g, k_norm_g, attn_sink, gmlp_ln_g, gmlp_ln_b,
           w_spatial, b_spatial, conv_w, conv_b, dt_bias, a_log, d_skip, ssd_norm_g, w_up_attn, w_up_gmlp,
           w_up_ssd, w_out, norm_ff_g, w_ff1, w_ff2):
    layer_params = dict(
        w_in=w_in, norm_mix_g=norm_mix_g, q_norm_g=q_norm_g, k_norm_g=k_norm_g, attn_sink=attn_sink,
        gmlp_ln_g=gmlp_ln_g, gmlp_ln_b=gmlp_ln_b, w_spatial=w_spatial, b_spatial=b_spatial, conv_w=conv_w,
        conv_b=conv_b, dt_bias=dt_bias, a_log=a_log, d_skip=d_skip, ssd_norm_g=ssd_norm_g,
        w_up_attn=w_up_attn, w_up_gmlp=w_up_gmlp, w_up_ssd=w_up_ssd, w_out=w_out, norm_ff_g=norm_ff_g,
        w_ff1=w_ff1, w_ff2=w_ff2)
    return _encoder(x_prompt, x_sample, rel_bias, layer_params)
```

```python
import functools
import math
from typing import NamedTuple

import numpy as np
import jax
import jax.numpy as jnp
from jax import lax
from jax.experimental import pallas as pl
from jax.experimental.pallas import tpu as pltpu

F32 = jnp.float32
BF16 = jnp.bfloat16

D_MODEL = 1024
ATTN_HEADS = 8
ATTN_KV_HEADS = 2
HEAD_DIM = 64
ATTN_Q_WIDTH = ATTN_HEADS * HEAD_DIM
ATTN_KV_WIDTH = ATTN_KV_HEADS * HEAD_DIM
WINDOW = 128
NUM_BUCKETS = 32
MAX_DISTANCE = 128
GMLP_WIDTH = 512
GMLP_GROUPS = 4
SSD_INNER = 1024
SSD_HEAD_DIM = 64
SSD_HEADS = SSD_INNER // SSD_HEAD_DIM
SSD_GROUPS = 2
SSD_STATE = 128
N_DIR = 2
FF_DIM = 4 * D_MODEL
EPS = 1e-6
NEG_INF = -1e30

BLK = 128
LANES = 128
BF16_SUBLANES = 16
VMEM_LIMIT = 56 * 1024 * 1024
_LOG2E = math.log2(math.e)

COL_Z = 0
COL_XS = 1024
COL_GATE = 2048
COL_Q = 5120
COL_GU = 5632
COL_GV = 6144
COL_BC = 6656
COL_KV = 7168
P_WIDTH = 7424
DT_REP = 32
SSD_BC_WIDTH = 2 * SSD_GROUPS * SSD_STATE

_R_Q, _R_K, _R_V, _R_GU, _R_GV, _R_Z, _R_XS, _R_B, _R_C, _R_DT, _R_GATE, _R_END = (
    0, 512, 640, 768, 1280, 1792, 2816, 3840, 4096, 4352, 4384, 7456)


class Layout(NamedTuple):
    n_blocks: int
    n_first: int
    per_seq_first: int
    per_seq_second: int


def _seq_pos(blk, lay):
    in_first = blk < lay.n_first
    pos = jnp.where(in_first, lax.rem(blk, lay.per_seq_first),
                    lax.rem(blk - lay.n_first, lay.per_seq_second))
    last = jnp.where(in_first, lay.per_seq_first - 1, lay.per_seq_second - 1)
    return pos == 0, pos == last


def _sigmoid(x):
    return 0.5 * jnp.tanh(0.5 * x) + 0.5


def _silu(x):
    return x * _sigmoid(x)


def _gelu(x):
    return x * (0.5 * (1.0 + jnp.tanh(math.sqrt(2.0 / math.pi) * (x + 0.044715 * (x * x * x)))))


def _softplus(x):
    return jnp.maximum(x, 0.0) + jnp.log1p(jnp.exp(-jnp.abs(x)))


def _rms(x, g):
    return x * lax.rsqrt(jnp.mean(x * x, axis=-1, keepdims=True) + EPS) * g


def _dot(a, b):
    return jnp.dot(a, b, preferred_element_type=F32)


def _dot_nt(a, b):
    return lax.dot_general(a, b, (((1,), (1,)), ((), ())), preferred_element_type=F32)


def _const_spec(shape):
    nd = len(shape)
    return pl.BlockSpec(shape, lambda *_: (0,) * nd)


def _params(semantics):
    return pltpu.CompilerParams(dimension_semantics=semantics, vmem_limit_bytes=VMEM_LIMIT)


_PROJ_TILE = 512
_PROJ_LOOKAHEAD = 1
_X_HALO = 8


def _split2(x):
    hi = x.astype(BF16)
    return hi, (x - hi.astype(F32)).astype(BF16)


def _inproj_body(xa_ref, xb_ref, xa_prev_ref, xb_prev_ref, xa_next_ref, xb_next_ref, g_ref, w_ref, wdt_ref, wdtT_ref,
                 cw_ref, cb_ref, lng_ref, lnb_ref, kscale_ref, qsum_ref, ksum_ref, o_ref, dt_ref, dtT_ref, *, lay):
    tm = _PROJ_TILE
    nblk = tm // BLK
    i = pl.program_id(0)
    keep_prev = jnp.where(_seq_pos(i * nblk, lay)[0], 0.0, 1.0)
    keep_next = jnp.where(_seq_pos(i * nblk + nblk - 1, lay)[1], 0.0, 1.0)
    in_first = i < lay.n_first // nblk

    row_limit = jnp.where(in_first, tm, 0)

    def pick(a_ref, b_ref):
        a = a_ref[...]
        return jnp.where(lax.broadcasted_iota(jnp.int32, a.shape, 0) < row_limit, a, b_ref[...])

    xn = _rms(pick(xa_ref, xb_ref), g_ref[...]).astype(BF16)
    xh = _rms(jnp.concatenate([pick(xa_prev_ref, xb_prev_ref), pick(xa_next_ref, xb_next_ref)], axis=0),
              g_ref[...]).astype(BF16)

    def put(c0, val):
        o_ref[:, c0:c0 + val.shape[1]] = val.astype(BF16)

    def conv_silu(k0):
        def post(c0, y):
            width = y.shape[1]
            yh = _dot(xh, w_ref[:, c0:c0 + width])
            halo_p = yh[_X_HALO - 1:_X_HALO] * keep_prev
            halo_n = yh[_X_HALO:_X_HALO + 1] * keep_next
            up = pltpu.roll(y, 1, axis=0)
            dn = pltpu.roll(y, tm - 1, axis=0)
            r8 = lax.broadcasted_iota(jnp.int32, (8, width), 0)
            y_m1 = jnp.concatenate([jnp.where(r8 == 0, halo_p, up[:8]), up[8:]], axis=0)
            y_p1 = jnp.concatenate([dn[:tm - 8], jnp.where(r8 == 7, halo_n, dn[tm - 8:])], axis=0)
            w = cw_ref[:, k0:k0 + width]
            put(c0, _silu(cb_ref[:, k0:k0 + width] + y_m1 * w[0:1] + y * w[1:2] + y_p1 * w[2:3]))
        return post

    def q_norm(c0, q):
        sq = jnp.concatenate(_split2(q * q), axis=1)

        def tail():
            qss = _dot(sq, qsum_ref[...])
            put(c0, q * lax.rsqrt(qss * (1.0 / HEAD_DIM) + EPS))
        return tail

    def gelu_ln(c0, y):
        vf = _gelu(y)
        vc = vf - jnp.mean(vf, axis=-1, keepdims=True)
        var = jnp.mean(vc * vc, axis=-1, keepdims=True)
        put(c0, vc * lax.rsqrt(var + EPS) * lng_ref[...] + lnb_ref[...])

    def kv_norm(c0, kv):
        kf = kv[:, :ATTN_KV_WIDTH]
        sq = jnp.concatenate(_split2(kf * kf), axis=1)
        put(c0 + ATTN_KV_WIDTH, kv[:, ATTN_KV_WIDTH:])

        def tail():
            kss = _dot(sq, ksum_ref[...])
            put(c0, kf * lax.rsqrt(kss * (1.0 / HEAD_DIM) + EPS) * kscale_ref[...])
        return tail

    def gate(k):
        return (COL_GATE + k * D_MODEL, D_MODEL, lambda c0, y: put(c0, _sigmoid(y)))

    items = [
        (COL_XS, SSD_INNER, conv_silu(0)),
        (COL_Z, SSD_INNER, lambda c0, y: put(c0, _silu(y))),
        (COL_BC, SSD_BC_WIDTH, conv_silu(SSD_INNER)),
        gate(0),
        (COL_GV, GMLP_WIDTH, gelu_ln),
        gate(1),
        (COL_GU, GMLP_WIDTH, lambda c0, y: put(c0, _gelu(y))),
        gate(2),
        (COL_Q, ATTN_Q_WIDTH, q_norm),
        (COL_KV, 2 * ATTN_KV_WIDTH, kv_norm),
    ]

    def proj(k):
        c0, width, _ = items[k]
        return _dot(xn, w_ref[:, c0:c0 + width])

    pending = [proj(k) for k in range(_PROJ_LOOKAHEAD)]
    tail = None
    for k, (c0, _, post) in enumerate(items):
        y = pending.pop(0)
        if k + _PROJ_LOOKAHEAD < len(items):
            pending.append(proj(k + _PROJ_LOOKAHEAD))
        if tail is not None:
            tail()
        tail = post(c0, y)
    dt_ref[...] = _dot(xn, wdt_ref[...])
    dtT_ref[...] = _dot_nt(wdtT_ref[...], xn)
    if tail is not None:
        tail()


def _head_sum_matrix(width):
    k = np.arange(2 * width)[:, None]
    n = np.arange(width)[None, :]
    return jnp.asarray((k % width) // HEAD_DIM == n // HEAD_DIM, dtype=BF16)


def _two_stream_specs(tm, n_a, n_b):
    return (pl.BlockSpec((tm, D_MODEL), lambda i: (jnp.minimum(i, n_a - 1), 0)),
            pl.BlockSpec((tm, D_MODEL), lambda i: (jnp.clip(i - n_a, 0, n_b - 1), 0)))


def _inproj(xa, xb, p, lay):
    tm = _PROJ_TILE
    n_a, n_b = xa.shape[0] // tm, xb.shape[0] // tm
    t = xa.shape[0] + xb.shape[0]
    assert lay.per_seq_first % (tm // BLK) == 0 and lay.per_seq_second % (tm // BLK) == 0
    hal = tm // _X_HALO
    conv_w = SSD_INNER + SSD_BC_WIDTH

    def halo(tile0, n_tiles, shift):
        last = n_tiles * hal - 1
        return pl.BlockSpec((_X_HALO, D_MODEL),
                            lambda i: (jnp.clip((i - tile0 + shift) * hal - 1 + shift, 0, last), 0))

    return pl.pallas_call(
        functools.partial(_inproj_body, lay=lay),
        grid=(n_a + n_b,),
        in_specs=[
            *_two_stream_specs(tm, n_a, n_b),
            halo(0, n_a, 0), halo(n_a, n_b, 0), halo(0, n_a, 1), halo(n_a, n_b, 1),
            _const_spec((1, D_MODEL)),
            _const_spec((D_MODEL, P_WIDTH)),
            _const_spec((D_MODEL, LANES)),
            _const_spec((DT_REP, D_MODEL)),
            _const_spec((3, conv_w)), _const_spec((1, conv_w)),
            _const_spec((1, GMLP_WIDTH)), _const_spec((1, GMLP_WIDTH)),
            _const_spec((1, ATTN_KV_WIDTH)),
            _const_spec((2 * ATTN_Q_WIDTH, ATTN_Q_WIDTH)),
            _const_spec((2 * ATTN_KV_WIDTH, ATTN_KV_WIDTH)),
        ],
        out_specs=[
            pl.BlockSpec((tm, P_WIDTH), lambda i: (i, 0)),
            pl.BlockSpec((tm, LANES), lambda i: (i, 0)),
            pl.BlockSpec((DT_REP, tm), lambda i: (0, i)),
        ],
        out_shape=[
            jax.ShapeDtypeStruct((t, P_WIDTH), BF16),
            jax.ShapeDtypeStruct((t, LANES), F32),
            jax.ShapeDtypeStruct((DT_REP, t), F32),
        ],
        compiler_params=_params(("parallel",)),
        name="inproj",
    )(xa, xb, xa, xb, xa, xb, p["norm_mix_g"], p["w_main"], p["w_dt"], p["w_dtT"], p["conv_w"], p["conv_b"], p["ln_g"], p["ln_b"],
      p["kscale"], _head_sum_matrix(ATTN_Q_WIDTH), _head_sum_matrix(ATTN_KV_WIDTH))


_ATTN_TILE_BLOCKS = 4
_PV_DELAY = 1
_ATTN_LOOKAHEAD = 4


def _attention_tile(tile, q_ref, kvp_ref, kvm_ref, kvn_ref, bias_ref, sink_ref, put_block, after_item, lay):
    nt = _ATTN_TILE_BLOCKS
    grp = ATTN_HEADS // ATTN_KV_HEADS
    kv = jnp.concatenate([kvp_ref[...], kvm_ref[...], kvn_ref[...]], axis=0)
    kn = [kv[:, g * HEAD_DIM:(g + 1) * HEAD_DIM] for g in range(ATTN_KV_HEADS)]
    vT = jnp.transpose(kv[:, ATTN_KV_WIDTH:].astype(F32)).astype(BF16)
    neg_prev = jnp.where(_seq_pos(tile * nt, lay)[0], NEG_INF, 0.0)
    neg_next = jnp.where(_seq_pos(tile * nt + nt - 1, lay)[1], NEG_INF, 0.0)

    qbs = [q_ref[jb * BLK:(jb + 1) * BLK, :] for jb in range(nt)]

    def scores(jb, hp):
        slab = qbs[jb][:, hp * LANES:(hp + 1) * LANES]
        q2 = jnp.concatenate([slab[:, :HEAD_DIM], slab[:, HEAD_DIM:]], axis=0)
        return _dot_nt(kn[2 * hp // grp][jb * BLK:(jb + 3) * BLK], q2) + bias_ref[hp]

    n_pairs = ATTN_HEADS // 2
    items = [(jb, hp) for jb in range(nt) for hp in range(n_pairs)]
    pending = [scores(*it) for it in items[:_ATTN_LOOKAHEAD]]
    outs = []
    waiting = []

    def value_matmul():
        jb, hp, p, inv = waiting.pop(0)
        g = 2 * hp // grp
        o = _dot(vT[g * HEAD_DIM:(g + 1) * HEAD_DIM, jb * BLK:(jb + 3) * BLK], p) * inv
        outs.extend([o[:, :BLK], o[:, BLK:]])
        if hp == n_pairs - 1:
            put_block(jb, jnp.transpose(jnp.concatenate(outs, axis=0)).astype(BF16))
            outs.clear()

    for idx, (jb, hp) in enumerate(items):
        s = pending.pop(0)
        if idx + _ATTN_LOOKAHEAD < len(items):
            pending.append(scores(*items[idx + _ATTN_LOOKAHEAD]))
        if len(waiting) > _PV_DELAY:
            value_matmul()
        if jb == 0:
            s = jnp.concatenate([s[:BLK] + neg_prev, s[BLK:]], axis=0)
        if jb == nt - 1:
            s = jnp.concatenate([s[:2 * BLK], s[2 * BLK:] + neg_next], axis=0)
        snk = sink_ref[hp:hp + 1, :]
        m = jnp.maximum(jnp.max(s, axis=0, keepdims=True), snk)
        p = jnp.exp2(s - m)
        denom = jnp.sum(p, axis=0, keepdims=True) + jnp.exp2(snk - m)
        waiting.append((jb, hp, p.astype(BF16), 1.0 / denom))
        after_item(idx)
    while waiting:
        value_matmul()


_SPLIT_TERMS = 3
_REPLICAS = LANES // DT_REP


def _lane_split(x, terms=_SPLIT_TERMS):
    rep = lax.broadcasted_iota(jnp.int32, x.shape, 1) // DT_REP
    out = jnp.zeros(x.shape, BF16)
    rem = x
    for r in range(terms):
        term = rem.astype(BF16)
        out = jnp.where(rep == r, term, out)
        rem = rem - term.astype(F32)
    return out


def _sublane_split(x):
    terms, rem = [], x
    for _ in range(_SPLIT_TERMS):
        term = rem.astype(BF16)
        terms.append(term)
        rem = rem - term.astype(F32)
    return jnp.concatenate(terms, axis=0)


_EXPAND_TERMS = 2


def _expand(x, e_ref):
    return _dot(_lane_split(x, _EXPAND_TERMS), e_ref[...])


def _tri(upper):
    r = lax.broadcasted_iota(jnp.int32, (BLK, BLK), 0)
    c = lax.broadcasted_iota(jnp.int32, (BLK, BLK), 1)
    return (c >= r) if upper else (c <= r)


def _ones_where(mask):
    return jnp.where(mask, 1.0, 0.0).astype(BF16)


_SSD_TILE_CHUNKS = 8


def _rows(j):
    return slice(j * BLK, (j + 1) * BLK)


def _dt_row_major(dt_raw, dtb, alog):
    L = BLK
    dt = _softplus(dt_raw + dtb)
    sp = _lane_split(dt * (-jnp.exp(alog)))
    lane = lax.broadcasted_iota(jnp.int32, (L, LANES), 1)
    is_fwd = (lane & (DT_REP - 1)) < SSD_HEADS
    part = jnp.where(is_fwd, _dot(_ones_where(_tri(False)), sp), _dot(_ones_where(_tri(True)), sp))
    acum = part
    for r in range(1, _REPLICAS):
        acum = acum + pltpu.roll(part, r * DT_REP, axis=1)
    a_end = jnp.where(is_fwd[0:1], acum[L - 1:L], acum[0:1])
    return dt, acum, a_end


def _dt_head_major(dtT_raw, dtbT, alogT):
    nh = SSD_HEADS
    dtT = _softplus(dtT_raw + dtbT)
    spT = _sublane_split(dtT * (-jnp.exp(alogT)))
    pf = _dot(spT, _ones_where(_tri(True)))
    pb = _dot(spT, _ones_where(_tri(False)))
    acumT_f = pf[0:nh]
    acumT_b = pb[nh:2 * nh]
    for r in range(1, _SPLIT_TERMS):
        acumT_f = acumT_f + pf[r * DT_REP:r * DT_REP + nh]
        acumT_b = acumT_b + pb[r * DT_REP + nh:(r + 1) * DT_REP]
    log2_dtT = jnp.log(dtT) * _LOG2E
    return acumT_f * _LOG2E - log2_dtT[0:nh], acumT_b * _LOG2E - log2_dtT[nh:2 * nh]


def _state_factors(dt, acum, a_end, e_ref):
    ea_exp = _expand(jnp.exp(acum), e_ref)
    w_exp = _expand(dt * jnp.exp(a_end - acum), e_ref)
    cd_exp = _expand(jnp.broadcast_to(jnp.exp(a_end), (8, LANES)), e_ref)[0:1]
    return ea_exp, w_exp, cd_exp


def _state_sweep(st_ref, order, xs, bm, cm_bf, fac):
    gw = SSD_INNER // SSD_GROUPS
    upd = {}
    for j in order:
        wx = (xs[j] * fac[j][1]).astype(BF16)
        upd[j] = [_dot(jnp.transpose(bm[j][g]).astype(BF16), wx[:, g * gw:(g + 1) * gw])
                  for g in range(SSD_GROUPS)]
    sg = [st_ref[g] for g in range(SSD_GROUPS)]
    out = {}
    for j in order:
        y_off = [_dot(cm_bf[j][g], sg[g].astype(BF16)) for g in range(SSD_GROUPS)]
        neg_cd = -fac[j][2]
        sg = [upd[j][g] - sg[g] * neg_cd[:, g * gw:(g + 1) * gw] for g in range(SSD_GROUPS)]
        out[j] = jnp.concatenate(y_off, axis=1) * fac[j][0]
    for g in range(SSD_GROUPS):
        st_ref[g] = sg[g]
    return out


def _split_bc(bc):
    n = SSD_STATE
    return ([bc[:, g * n:(g + 1) * n] for g in range(SSD_GROUPS)],
            [bc[:, (SSD_GROUPS + g) * n:(SSD_GROUPS + g + 1) * n] for g in range(SSD_GROUPS)])


def _ssd_bwd_sweep(xs_ref, bc_ref, dt_ref, dtb_ref, alog_ref, eb_ref, yo_ref, st_ref):
    order = list(reversed(range(_SSD_TILE_CHUNKS)))
    rm = {j: _dt_row_major(dt_ref[_rows(j), :], dtb_ref[...], alog_ref[...]) for j in order}
    fac = {j: _state_factors(*rm[j], eb_ref) for j in order}
    xs = {j: xs_ref[_rows(j), :].astype(F32) for j in order}
    bm = {j: _split_bc(bc_ref[_rows(j), :].astype(F32))[0] for j in order}
    cm_bf = {j: _split_bc(bc_ref[_rows(j), :])[1] for j in order}
    y_off = _state_sweep(st_ref, order, xs, bm, cm_bf, fac)
    for j in order:
        yo_ref[_rows(j), :] = y_off[j].astype(BF16)


def _ssd_body(xsf_ref, bcf_ref, dtf_ref, dtTf_ref, xsb_ref, bcb_ref, dtb_ref, bias_ref, alog_ref, biasT_ref, alogT_ref,
              dskip_ref, ef_ref, eb_ref, y_ref, yob_ref, stf_ref, stb_ref, *, lay):
    ns = _SSD_TILE_CHUNKS
    i = pl.program_id(0)
    n_steps = lay.n_blocks // ns

    @pl.when(_seq_pos(i * ns, lay)[0])
    def _():
        stf_ref[...] = jnp.zeros(stf_ref.shape, F32)

    @pl.when(_seq_pos((n_steps - 1 - i) * ns + ns - 1, lay)[1])
    def _():
        stb_ref[...] = jnp.zeros(stb_ref.shape, F32)

    L, nh = BLK, SSD_HEADS
    hpg = nh // SSD_GROUPS
    order = list(range(ns))
    lane = lax.broadcasted_iota(jnp.int32, (L, LANES), 1)
    xs_bf = {j: xsf_ref[_rows(j), :] for j in order}
    bc_bf = {j: bcf_ref[_rows(j), :] for j in order}
    bm_bf = {j: _split_bc(bc_bf[j])[0] for j in order}
    cm_bf = {j: _split_bc(bc_bf[j])[1] for j in order}
    cb = {j: [_dot_nt(cm_bf[j][g], bm_bf[j][g]) for g in range(SSD_GROUPS)] for j in order}
    rm = {j: _dt_row_major(dtf_ref[_rows(j), :], bias_ref[...], alog_ref[...]) for j in order}
    hm = {j: _dt_head_major(dtTf_ref[:, _rows(j)], biasT_ref[...], alogT_ref[...]) for j in order}

    _ssd_bwd_sweep(xsb_ref, bcb_ref, dtb_ref, bias_ref, alog_ref, eb_ref, yob_ref, stb_ref)

    fac = {j: _state_factors(*rm[j], ef_ref) for j in order}
    xs = {j: xs_bf[j].astype(F32) for j in order}
    bm, coef_exp = {}, {}
    for j in order:
        bm[j], cm = _split_bc(bc_bf[j].astype(F32))
        cbd = jnp.where((lane & (nh - 1)) < hpg, jnp.sum(cm[0] * bm[j][0], axis=1, keepdims=True),
                        jnp.sum(cm[1] * bm[j][1], axis=1, keepdims=True))
        coef_exp[j] = _expand(dskip_ref[...] + cbd * rm[j][0], eb_ref)
    y_off = _state_sweep(stf_ref, order, xs, bm, cm_bf, fac)

    causal = _tri(False)
    left = lane < SSD_HEAD_DIM
    zero_bf = jnp.zeros((L, LANES), BF16)
    acum2 = {j: rm[j][1] * _LOG2E for j in order}
    y_loc = {j: [] for j in order}
    for j in order:
        for k in range(nh // 2):
            g = (2 * k) // hpg
            lhs = []
            for h in (2 * k, 2 * k + 1):
                col_f = jnp.broadcast_to(acum2[j][:, h:h + 1], (L, L))
                col_b = jnp.broadcast_to(acum2[j][:, nh + h:nh + h + 1], (L, L))
                e = jnp.where(causal, col_f - hm[j][0][h:h + 1, :], col_b - hm[j][1][h:h + 1, :])
                lhs.append((cb[j][g] * jnp.exp2(e)).astype(BF16))
            xs_pair = xs_bf[j][:, k * LANES:(k + 1) * LANES]
            rhs = jnp.concatenate([jnp.where(left, xs_pair, zero_bf), jnp.where(left, zero_bf, xs_pair)], axis=0)
            y_loc[j].append(_dot(jnp.concatenate(lhs, axis=1), rhs))
    for j in order:
        y = jnp.concatenate(y_loc[j], axis=1) + coef_exp[j] * xs[j] + y_off[j]
        y_ref[_rows(j), :] = y.astype(BF16)


def _ssd(proj, dt, dtT, p, e_mats, lay):
    t = proj.shape[0]
    ns = _SSD_TILE_CHUNKS
    assert lay.per_seq_first % ns == 0 and lay.per_seq_second % ns == 0 and lay.n_first % ns == 0
    nb = lay.n_blocks // ns
    rows = ns * BLK
    xs_blk, bc_blk = COL_XS // SSD_INNER, COL_BC // SSD_BC_WIDTH
    state = pltpu.VMEM((SSD_GROUPS, SSD_STATE, SSD_INNER // SSD_GROUPS), F32)

    def rev(c):
        return nb - 1 - c

    return pl.pallas_call(
        functools.partial(_ssd_body, lay=lay),
        grid=(nb,),
        in_specs=[
            pl.BlockSpec((rows, SSD_INNER), lambda c: (c, xs_blk)),
            pl.BlockSpec((rows, SSD_BC_WIDTH), lambda c: (c, bc_blk)),
            pl.BlockSpec((rows, LANES), lambda c: (c, 0)),
            pl.BlockSpec((DT_REP, rows), lambda c: (0, c)),
            pl.BlockSpec((rows, SSD_INNER), lambda c: (rev(c), xs_blk)),
            pl.BlockSpec((rows, SSD_BC_WIDTH), lambda c: (rev(c), bc_blk)),
            pl.BlockSpec((rows, LANES), lambda c: (rev(c), 0)),
            _const_spec((1, LANES)), _const_spec((1, LANES)),
            _const_spec((DT_REP, 1)), _const_spec((DT_REP, 1)),
            _const_spec((1, LANES)),
            _const_spec((LANES, SSD_INNER)), _const_spec((LANES, SSD_INNER)),
        ],
        out_specs=[
            pl.BlockSpec((rows, SSD_INNER), lambda c: (c, 0)),
            pl.BlockSpec((rows, SSD_INNER), lambda c: (rev(c), 0)),
        ],
        out_shape=[jax.ShapeDtypeStruct((t, SSD_INNER), BF16), jax.ShapeDtypeStruct((t, SSD_INNER), BF16)],
        scratch_shapes=[state, state],
        compiler_params=_params(("arbitrary",)),
        name="ssd",
    )(proj, proj, dt, dtT, proj, proj, dt, p["dtb"], p["alog"], p["dtbT"], p["alogT"], p["dskip"],
      e_mats[0], e_mats[1])


_ROW_TILE = 512
_MERGE_COLS = 256


def _attn_merge_body(q_ref, kvp_ref, kvm_ref, kvn_ref, bias_ref, sink_ref,
                     xa_ref, xb_ref, gu_ref, gv_ref, yc_ref, yob_ref, z_ref, g0_ref, g1_ref, g2_ref, ng_ref, ws_ref, bs_ref,
                     wa_ref, wb_ref, wc_ref, wo_ref, ha_ref, hb_ref, ya_ref, m_ref, *, lay, n_a, n_tiles):
    i = pl.program_id(0)
    slot = lax.rem(i, 2)

    @pl.when(i == 0)
    def _():
        ya_ref[...] = jnp.zeros(ya_ref.shape, BF16)

    ya = ya_ref[1 - slot]
    gw = GMLP_WIDTH // GMLP_GROUPS
    chunks = []
    for c in range(_ROW_TILE // BLK):
        r = slice(c * BLK, (c + 1) * BLK)
        chunks.append(jnp.concatenate(
            [gu_ref[r, g * gw:(g + 1) * gw].astype(F32)
             * (_dot(ws_ref[g], gv_ref[r, g * gw:(g + 1) * gw]) + bs_ref[:, g * gw:(g + 1) * gw])
             for g in range(GMLP_GROUPS)], axis=1))
    yb = jnp.concatenate(chunks, axis=0).astype(BF16)
    y = (yc_ref[...].astype(F32) + yob_ref[...].astype(F32)) * z_ref[...].astype(F32)
    sw = SSD_INNER // SSD_GROUPS
    yn = jnp.concatenate([_rms(y[:, g * sw:(g + 1) * sw], ng_ref[:, g * sw:(g + 1) * sw])
                          for g in range(SSD_GROUPS)], axis=1).astype(BF16)

    acc = {}

    def piece(c, k):
        cols = slice(c * _MERGE_COLS, (c + 1) * _MERGE_COLS)

        def run():
            if k == 0:
                acc[c] = g0_ref[:, cols].astype(F32) * _dot(ya, wa_ref[:, cols])
            elif k == 1:
                acc[c] = acc[c] + g1_ref[:, cols].astype(F32) * _dot(yb, wb_ref[:, cols])
            else:
                m_ref[:, cols] = (acc[c] + g2_ref[:, cols].astype(F32) * _dot(yn, wc_ref[:, cols])).astype(BF16)
        return run

    pieces = [piece(c, k) for c in range(D_MODEL // _MERGE_COLS) for k in range(3)]
    n_items = _ATTN_TILE_BLOCKS * ATTN_HEADS // 2
    done = [0]

    def after_item(idx):
        while done[0] < (idx + 1) * len(pieces) // n_items:
            pieces[done[0]]()
            done[0] += 1

    def put_block(jb, val):
        ya_ref[slot, jb * BLK:(jb + 1) * BLK, :] = val

    _attention_tile(jnp.minimum(i, n_tiles - 1), q_ref, kvp_ref, kvm_ref, kvn_ref, bias_ref, sink_ref,
                    put_block, after_item, lay)

    m = m_ref[...]
    in_first = jnp.maximum(i - 1, 0) < n_a

    @pl.when(in_first)
    def _():
        ha_ref[...] = xa_ref[...] + _dot(m, wo_ref[...])

    @pl.when(jnp.logical_not(in_first))
    def _():
        hb_ref[...] = xb_ref[...] + _dot(m, wo_ref[...])


def _attn_merge(xa, xb, yc, yob, proj, bias, p, lay):
    tm = _ROW_TILE
    nt = _ATTN_TILE_BLOCKS
    assert tm == nt * BLK
    assert lay.per_seq_first % nt == 0 and lay.per_seq_second % nt == 0 and lay.n_first % nt == 0
    n_a, n_b = xa.shape[0] // tm, xb.shape[0] // tm
    n_tiles = n_a + n_b
    g_blk = COL_GATE // D_MODEL
    kv_w = 2 * ATTN_KV_WIDTH
    kv_blk = COL_KV // kv_w
    last_blk = lay.n_blocks - 1

    def att(i):
        return jnp.minimum(i, n_tiles - 1)

    def mrg(i):
        return jnp.maximum(i - 1, 0)

    def row(width, col_blk=0):
        return pl.BlockSpec((tm, width), lambda i: (mrg(i), col_blk))

    stream = (pl.BlockSpec((tm, D_MODEL), lambda i: (jnp.minimum(mrg(i), n_a - 1), 0)),
              pl.BlockSpec((tm, D_MODEL), lambda i: (jnp.clip(mrg(i) - n_a, 0, n_b - 1), 0)))
    return pl.pallas_call(
        functools.partial(_attn_merge_body, lay=lay, n_a=n_a, n_tiles=n_tiles),
        grid=(n_tiles + 1,),
        in_specs=[
            pl.BlockSpec((tm, ATTN_Q_WIDTH), lambda i: (att(i), COL_Q // ATTN_Q_WIDTH)),
            pl.BlockSpec((BLK, kv_w), lambda i: (jnp.maximum(att(i) * nt - 1, 0), kv_blk)),
            pl.BlockSpec((tm, kv_w), lambda i: (att(i), kv_blk)),
            pl.BlockSpec((BLK, kv_w), lambda i: (jnp.minimum(att(i) * nt + nt, last_blk), kv_blk)),
            _const_spec((ATTN_HEADS // 2, 3 * BLK, 2 * BLK)),
            _const_spec((ATTN_HEADS // 2, 2 * BLK)),
            *stream, row(GMLP_WIDTH, COL_GU // GMLP_WIDTH), row(GMLP_WIDTH, COL_GV // GMLP_WIDTH),
            row(SSD_INNER), row(SSD_INNER), row(SSD_INNER, COL_Z // SSD_INNER),
            row(D_MODEL, g_blk), row(D_MODEL, g_blk + 1), row(D_MODEL, g_blk + 2),
            _const_spec((1, SSD_INNER)),
            _const_spec((GMLP_GROUPS, BLK, BLK)), _const_spec((BLK, GMLP_WIDTH)),
            _const_spec((ATTN_Q_WIDTH, D_MODEL)), _const_spec((GMLP_WIDTH, D_MODEL)),
            _const_spec((SSD_INNER, D_MODEL)), _const_spec((D_MODEL, D_MODEL)),
        ],
        out_specs=list(stream),
        out_shape=[jax.ShapeDtypeStruct(xa.shape, F32), jax.ShapeDtypeStruct(xb.shape, F32)],
        scratch_shapes=[pltpu.VMEM((2, tm, ATTN_Q_WIDTH), BF16), pltpu.VMEM((tm, D_MODEL), BF16)],
        compiler_params=_params(("arbitrary",)),
        name="attn_merge",
    )(proj, proj, proj, proj, bias, p["sink"],
      xa, xb, proj, proj, yc, yob, proj, proj, proj, proj, p["ssd_norm_g"], p["w_s"], p["b_s"],
      p["w_a"], p["w_b"], p["w_c"], p["w_o"])


_FF_CHUNK = 1024


def _ffn_body(ha_ref, hb_ref, g_ref, w1_ref, w2_ref, oa_ref, ob_ref, *, n_a):
    def run(h_ref, o_ref):
        h = h_ref[...]
        hn = _rms(h, g_ref[...]).astype(BF16)
        acc = h
        for c0 in range(0, FF_DIM, _FF_CHUNK):
            a = jnp.maximum(_dot(hn, w1_ref[:, c0:c0 + _FF_CHUNK]), 0.0)
            acc = acc + _dot((a * a).astype(BF16), w2_ref[c0:c0 + _FF_CHUNK, :])
        o_ref[...] = acc

    in_first = pl.program_id(0) < n_a
    pl.when(in_first)(lambda: run(ha_ref, oa_ref))
    pl.when(jnp.logical_not(in_first))(lambda: run(hb_ref, ob_ref))


def _ffn(ha, hb, g, w1, w2):
    tm = _ROW_TILE
    n_a, n_b = ha.shape[0] // tm, hb.shape[0] // tm
    stream = _two_stream_specs(tm, n_a, n_b)
    return pl.pallas_call(
        functools.partial(_ffn_body, n_a=n_a),
        grid=(n_a + n_b,),
        in_specs=[*stream, _const_spec((1, D_MODEL)), _const_spec((D_MODEL, FF_DIM)), _const_spec((FF_DIM, D_MODEL))],
        out_specs=list(stream),
        out_shape=[jax.ShapeDtypeStruct(ha.shape, F32), jax.ShapeDtypeStruct(hb.shape, F32)],
        compiler_params=_params(("arbitrary",)),
        name="ffn",
    )(ha, hb, g, w1, w2)


def _t5_bucket(rel):
    nb = NUM_BUCKETS // 2
    max_exact = nb // 2
    ret = jnp.where(rel > 0, nb, 0)
    n = jnp.abs(rel)
    n_safe = jnp.maximum(n, 1).astype(F32)
    large = max_exact + (jnp.log(n_safe / max_exact) / math.log(MAX_DISTANCE / max_exact)
                         * (nb - max_exact)).astype(jnp.int32)
    large = jnp.minimum(large, nb - 1)
    return ret + jnp.where(n < max_exact, n, large)


def _bias_table(rel_bias):
    qi = jnp.arange(BLK)[:, None]
    ki = jnp.arange(3 * BLK)[None, :]
    rel = ki - BLK - qi
    onehot = (_t5_bucket(rel)[..., None] == jnp.arange(NUM_BUCKETS)).astype(F32)
    bias = jnp.einsum("qkb,bh->hkq", onehot, rel_bias.astype(F32), precision=lax.Precision.HIGHEST) * _LOG2E
    bias = jnp.where((jnp.abs(rel) <= WINDOW).T[None], bias, NEG_INF)
    return jnp.concatenate([bias[0::2], bias[1::2]], axis=-1)


def _expand_matrix(d):
    k = np.arange(LANES)[:, None]
    col = np.arange(SSD_INNER)[None, :]
    hit = ((k % DT_REP) == d * SSD_HEADS + col // SSD_HEAD_DIM)
    return jnp.asarray(hit, dtype=BF16)


def _prep_layer_params(w_in, norm_mix_g, q_norm_g, k_norm_g, attn_sink, gmlp_ln_g, gmlp_ln_b, w_spatial,
                       b_spatial, conv_w, conv_b, dt_bias, a_log, d_skip, ssd_norm_g, w_up_attn, w_up_gmlp,
                       w_up_ssd, w_out, norm_ff_g, w_ff1, w_ff2):
    depth = w_in.shape[0]

    def cols(a, b):
        return w_in[:, :, a:b]

    w_main = jnp.concatenate([
        cols(_R_Z, _R_XS), cols(_R_XS, _R_B), cols(_R_GATE, _R_END), cols(_R_Q, _R_K),
        cols(_R_GU, _R_GV), cols(_R_GV, _R_Z), cols(_R_B, _R_DT), cols(_R_K, _R_GU)], axis=-1).astype(BF16)
    w_dt = cols(_R_DT, _R_GATE)
    rep = LANES // DT_REP
    gw = GMLP_WIDTH // GMLP_GROUPS

    def dt_lanes(v):
        return jnp.tile(v.reshape(depth, 1, DT_REP), (1, 1, rep))

    return dict(
        norm_mix_g=norm_mix_g[:, None, :],
        w_main=w_main,
        w_dt=jnp.tile(w_dt, (1, 1, rep)).astype(BF16),
        w_dtT=jnp.swapaxes(w_dt, 1, 2).astype(BF16),
        kscale=jnp.tile(q_norm_g * k_norm_g * (HEAD_DIM ** -0.5 * _LOG2E), (1, ATTN_KV_HEADS))[:, None, :],
        sink=jnp.repeat(attn_sink * _LOG2E, BLK, axis=-1).reshape(depth, ATTN_HEADS // 2, 2 * BLK),
        ln_g=gmlp_ln_g[:, None, :], ln_b=gmlp_ln_b[:, None, :],
        w_s=w_spatial.astype(BF16),
        b_s=jnp.repeat(jnp.swapaxes(b_spatial, 1, 2), gw, axis=-1),
        conv_w=conv_w, conv_b=conv_b[:, None, :],
        dtb=dt_lanes(dt_bias), alog=dt_lanes(a_log),
        dtbT=dt_bias.reshape(depth, DT_REP, 1), alogT=a_log.reshape(depth, DT_REP, 1),
        dskip=dt_lanes(jnp.concatenate([d_skip, d_skip], axis=-1)),
        ssd_norm_g=ssd_norm_g[:, None, :],
        w_a=w_up_attn.astype(BF16), w_b=w_up_gmlp.astype(BF16), w_c=w_up_ssd.astype(BF16),
        w_o=w_out.astype(BF16),
        norm_ff_g=norm_ff_g[:, None, :],
        w1=w_ff1.astype(BF16), w2=w_ff2.astype(BF16),
    )


def _layer(xa, xb, p, bias, e_mats, lay):
    proj, dt, dtT = _inproj(xa, xb, p, lay)
    yc, yob = _ssd(proj, dt, dtT, p, e_mats, lay)
    ha, hb = _attn_merge(xa, xb, yc, yob, proj, bias, p, lay)
    return _ffn(ha, hb, p["norm_ff_g"], p["w1"], p["w2"])


def _encoder(x_first, x_second, rel_bias, layer_params):
    b1, s1, _ = x_first.shape
    b2, s2, _ = x_second.shape
    t1, t2 = b1 * s1, b2 * s2
    lay = Layout(n_blocks=(t1 + t2) // BLK, n_first=t1 // BLK, per_seq_first=s1 // BLK,
                 per_seq_second=s2 // BLK)
    bias = _bias_table(rel_bias)
    e_mats = (_expand_matrix(0), _expand_matrix(1))
    params = _prep_layer_params(**layer_params)

    xa, xb = x_first.reshape(t1, D_MODEL), x_second.reshape(t2, D_MODEL)
    for layer in range(layer_params["w_in"].shape[0]):
        xa, xb = _layer(xa, xb, {k: v[layer] for k, v in params.items()}, bias, e_mats, lay)
    return xa.reshape(b1, s1, D_MODEL), xb.reshape(b2, s2, D_MODEL)


def kernel(x_prompt, x_sample, rel_bias, norm_mix_g, w_in, q_norm_g, k_norm_g, attn_sink, gmlp_ln_g, gmlp_ln_b,
           w_spatial, b_spatial, conv_w, conv_b, dt_bias, a_log, d_skip, ssd_norm_g, w_up_attn, w_up_gmlp,
           w_up_ssd, w_out, norm_ff_g, w_ff1, w_ff2):
    layer_params = dict(
        w_in=w_in, norm_mix_g=norm_mix_g, q_norm_g=q_norm_g, k_norm_g=k_norm_g, attn_sink=attn_sink,
        gmlp_ln_g=gmlp_ln_g, gmlp_ln_b=gmlp_ln_b, w_spatial=w_spatial, b_spatial=b_spatial, conv_w=conv_w,
        conv_b=conv_b, dt_bias=dt_bias, a_log=a_log, d_skip=d_skip, ssd_norm_g=ssd_norm_g,
        w_up_attn=w_up_attn, w_up_gmlp=w_up_gmlp, w_up_ssd=w_up_ssd, w_out=w_out, norm_ff_g=norm_ff_g,
        w_ff1=w_ff1, w_ff2=w_ff2)
    return _encoder(x_prompt, x_sample, rel_bias, layer_params)
```

```python
import functools
import math
from typing import NamedTuple

import numpy as np
import jax
import jax.numpy as jnp
from jax import lax
from jax.experimental import pallas as pl
from jax.experimental.pallas import tpu as pltpu

F32 = jnp.float32
BF16 = jnp.bfloat16

D_MODEL = 1024
ATTN_HEADS = 8
ATTN_KV_HEADS = 2
HEAD_DIM = 64
ATTN_Q_WIDTH = ATTN_HEADS * HEAD_DIM
ATTN_KV_WIDTH = ATTN_KV_HEADS * HEAD_DIM
WINDOW = 128
NUM_BUCKETS = 32
MAX_DISTANCE = 128
GMLP_WIDTH = 512
GMLP_GROUPS = 4
SSD_INNER = 1024
SSD_HEAD_DIM = 64
SSD_HEADS = SSD_INNER // SSD_HEAD_DIM
SSD_GROUPS = 2
SSD_STATE = 128
N_DIR = 2
FF_DIM = 4 * D_MODEL
EPS = 1e-6
NEG_INF = -1e30

BLK = 128
LANES = 128
BF16_SUBLANES = 16
VMEM_LIMIT = 56 * 1024 * 1024
_LOG2E = math.log2(math.e)

COL_Z = 0
COL_XS = 1024
COL_GATE = 2048
COL_Q = 5120
COL_GU = 5632
COL_GV = 6144
COL_BC = 6656
COL_KV = 7168
P_WIDTH = 7424
DT_REP = 32
SSD_BC_WIDTH = 2 * SSD_GROUPS * SSD_STATE

_R_Q, _R_K, _R_V, _R_GU, _R_GV, _R_Z, _R_XS, _R_B, _R_C, _R_DT, _R_GATE, _R_END = (
    0, 512, 640, 768, 1280, 1792, 2816, 3840, 4096, 4352, 4384, 7456)


class Layout(NamedTuple):
    n_blocks: int
    n_first: int
    per_seq_first: int
    per_seq_second: int


def _seq_pos(blk, lay):
    in_first = blk < lay.n_first
    pos = jnp.where(in_first, lax.rem(blk, lay.per_seq_first),
                    lax.rem(blk - lay.n_first, lay.per_seq_second))
    last = jnp.where(in_first, lay.per_seq_first - 1, lay.per_seq_second - 1)
    return pos == 0, pos == last


def _sigmoid(x):
    return 0.5 * jnp.tanh(0.5 * x) + 0.5


def _silu(x):
    return x * _sigmoid(x)


def _gelu(x):
    return x * (0.5 * (1.0 + jnp.tanh(math.sqrt(2.0 / math.pi) * (x + 0.044715 * (x * x * x)))))


def _softplus(x):
    return jnp.maximum(x, 0.0) + jnp.log1p(jnp.exp(-jnp.abs(x)))


def _rms(x, g):
    return x * lax.rsqrt(jnp.mean(x * x, axis=-1, keepdims=True) + EPS) * g


def _dot(a, b):
    return jnp.dot(a, b, preferred_element_type=F32)


def _dot_nt(a, b):
    return lax.dot_general(a, b, (((1,), (1,)), ((), ())), preferred_element_type=F32)


def _const_spec(shape):
    nd = len(shape)
    return pl.BlockSpec(shape, lambda *_: (0,) * nd)


def _params(semantics):
    return pltpu.CompilerParams(dimension_semantics=semantics, vmem_limit_bytes=VMEM_LIMIT)


_PROJ_TILE = 512
_PROJ_LOOKAHEAD = 1
_X_HALO = 8


def _split2(x):
    hi = x.astype(BF16)
    return hi, (x - hi.astype(F32)).astype(BF16)


def _inproj_body(xa_ref, xb_ref, xa_prev_ref, xb_prev_ref, xa_next_ref, xb_next_ref, g_ref, w_ref, wdt_ref, wdtT_ref,
                 cw_ref, cb_ref, lng_ref, lnb_ref, kscale_ref, qsum_ref, ksum_ref, o_ref, dt_ref, dtT_ref, *, lay):
    tm = _PROJ_TILE
    nblk = tm // BLK
    i = pl.program_id(0)
    keep_prev = jnp.where(_seq_pos(i * nblk, lay)[0], 0.0, 1.0)
    keep_next = jnp.where(_seq_pos(i * nblk + nblk - 1, lay)[1], 0.0, 1.0)
    in_first = i < lay.n_first // nblk

    row_limit = jnp.where(in_first, tm, 0)

    def pick(a_ref, b_ref):
        a = a_ref[...]
        return jnp.where(lax.broadcasted_iota(jnp.int32, a.shape, 0) < row_limit, a, b_ref[...])

    xn = _rms(pick(xa_ref, xb_ref), g_ref[...]).astype(BF16)
    xh = _rms(jnp.concatenate([pick(xa_prev_ref, xb_prev_ref), pick(xa_next_ref, xb_next_ref)], axis=0),
              g_ref[...]).astype(BF16)

    def put(c0, val):
        o_ref[:, c0:c0 + val.shape[1]] = val.astype(BF16)

    def conv_silu(k0):
        def post(c0, y):
            width = y.shape[1]
            yh = _dot(xh, w_ref[:, c0:c0 + width])
            halo_p = yh[_X_HALO - 1:_X_HALO] * keep_prev
            halo_n = yh[_X_HALO:_X_HALO + 1] * keep_next
            up = pltpu.roll(y, 1, axis=0)
            dn = pltpu.roll(y, tm - 1, axis=0)
            r8 = lax.broadcasted_iota(jnp.int32, (8, width), 0)
            y_m1 = jnp.concatenate([jnp.where(r8 == 0, halo_p, up[:8]), up[8:]], axis=0)
            y_p1 = jnp.concatenate([dn[:tm - 8], jnp.where(r8 == 7, halo_n, dn[tm - 8:])], axis=0)
            w = cw_ref[:, k0:k0 + width]
            put(c0, _silu(cb_ref[:, k0:k0 + width] + y_m1 * w[0:1] + y * w[1:2] + y_p1 * w[2:3]))
        return post

    def q_norm(c0, q):
        sq = jnp.concatenate(_split2(q * q), axis=1)

        def tail():
            qss = _dot(sq, qsum_ref[...])
            put(c0, q * lax.rsqrt(qss * (1.0 / HEAD_DIM) + EPS))
        return tail

    def gelu_ln(c0, y):
        vf = _gelu(y)
        vc = vf - jnp.mean(vf, axis=-1, keepdims=True)
        var = jnp.mean(vc * vc, axis=-1, keepdims=True)
        put(c0, vc * lax.rsqrt(var + EPS) * lng_ref[...] + lnb_ref[...])

    def kv_norm(c0, kv):
        kf = kv[:, :ATTN_KV_WIDTH]
        sq = jnp.concatenate(_split2(kf * kf), axis=1)
        put(c0 + ATTN_KV_WIDTH, kv[:, ATTN_KV_WIDTH:])

        def tail():
            kss = _dot(sq, ksum_ref[...])
            put(c0, kf * lax.rsqrt(kss * (1.0 / HEAD_DIM) + EPS) * kscale_ref[...])
        return tail

    def gate(k):
        return (COL_GATE + k * D_MODEL, D_MODEL, lambda c0, y: put(c0, _sigmoid(y)))

    items = [
        (COL_XS, SSD_INNER, conv_silu(0)),
        (COL_Z, SSD_INNER, lambda c0, y: put(c0, _silu(y))),
        (COL_BC, SSD_BC_WIDTH, conv_silu(SSD_INNER)),
        gate(0),
        (COL_GV, GMLP_WIDTH, gelu_ln),
        gate(1),
        (COL_GU, GMLP_WIDTH, lambda c0, y: put(c0, _gelu(y))),
        gate(2),
        (COL_Q, ATTN_Q_WIDTH, q_norm),
        (COL_KV, 2 * ATTN_KV_WIDTH, kv_norm),
    ]

    def proj(k):
        c0, width, _ = items[k]
        return _dot(xn, w_ref[:, c0:c0 + width])

    pending = [proj(k) for k in range(_PROJ_LOOKAHEAD)]
    tail = None
    for k, (c0, _, post) in enumerate(items):
        y = pending.pop(0)
        if k + _PROJ_LOOKAHEAD < len(items):
            pending.append(proj(k + _PROJ_LOOKAHEAD))
        if tail is not None:
            tail()
        tail = post(c0, y)
    dt_ref[...] = _dot(xn, wdt_ref[...])
    dtT_ref[...] = _dot_nt(wdtT_ref[...], xn)
    if tail is not None:
        tail()


def _head_sum_matrix(width):
    k = np.arange(2 * width)[:, None]
    n = np.arange(width)[None, :]
    return jnp.asarray((k % width) // HEAD_DIM == n // HEAD_DIM, dtype=BF16)


def _two_stream_specs(tm, n_a, n_b):
    return (pl.BlockSpec((tm, D_MODEL), lambda i: (jnp.minimum(i, n_a - 1), 0)),
            pl.BlockSpec((tm, D_MODEL), lambda i: (jnp.clip(i - n_a, 0, n_b - 1), 0)))


def _inproj(xa, xb, p, lay):
    tm = _PROJ_TILE
    n_a, n_b = xa.shape[0] // tm, xb.shape[0] // tm
    t = xa.shape[0] + xb.shape[0]
    assert lay.per_seq_first % (tm // BLK) == 0 and lay.per_seq_second % (tm // BLK) == 0
    hal = tm // _X_HALO
    conv_w = SSD_INNER + SSD_BC_WIDTH

    def halo(tile0, n_tiles, shift):
        last = n_tiles * hal - 1
        return pl.BlockSpec((_X_HALO, D_MODEL),
                            lambda i: (jnp.clip((i - tile0 + shift) * hal - 1 + shift, 0, last), 0))

    return pl.pallas_call(
        functools.partial(_inproj_body, lay=lay),
        grid=(n_a + n_b,),
        in_specs=[
            *_two_stream_specs(tm, n_a, n_b),
            halo(0, n_a, 0), halo(n_a, n_b, 0), halo(0, n_a, 1), halo(n_a, n_b, 1),
            _const_spec((1, D_MODEL)),
            _const_spec((D_MODEL, P_WIDTH)),
            _const_spec((D_MODEL, LANES)),
            _const_spec((DT_REP, D_MODEL)),
            _const_spec((3, conv_w)), _const_spec((1, conv_w)),
            _const_spec((1, GMLP_WIDTH)), _const_spec((1, GMLP_WIDTH)),
            _const_spec((1, ATTN_KV_WIDTH)),
            _const_spec((2 * ATTN_Q_WIDTH, ATTN_Q_WIDTH)),
            _const_spec((2 * ATTN_KV_WIDTH, ATTN_KV_WIDTH)),
        ],
        out_specs=[
            pl.BlockSpec((tm, P_WIDTH), lambda i: (i, 0)),
            pl.BlockSpec((tm, LANES), lambda i: (i, 0)),
            pl.BlockSpec((DT_REP, tm), lambda i: (0, i)),
        ],
        out_shape=[
            jax.ShapeDtypeStruct((t, P_WIDTH), BF16),
            jax.ShapeDtypeStruct((t, LANES), F32),
            jax.ShapeDtypeStruct((DT_REP, t), F32),
        ],
        compiler_params=_params(("parallel",)),
        name="inproj",
    )(xa, xb, xa, xb, xa, xb, p["norm_mix_g"], p["w_main"], p["w_dt"], p["w_dtT"], p["conv_w"], p["conv_b"], p["ln_g"], p["ln_b"],
      p["kscale"], _head_sum_matrix(ATTN_Q_WIDTH), _head_sum_matrix(ATTN_KV_WIDTH))


_ATTN_TILE_BLOCKS = 4
_PV_DELAY = 1
_ATTN_LOOKAHEAD = 2


def _attention_tile(tile, q_ref, kvp_ref, kvm_ref, kvn_ref, bias_ref, sink_ref, put_block, after_item, lay):
    nt = _ATTN_TILE_BLOCKS
    grp = ATTN_HEADS // ATTN_KV_HEADS
    kv = jnp.concatenate([kvp_ref[...], kvm_ref[...], kvn_ref[...]], axis=0)
    kn = [kv[:, g * HEAD_DIM:(g + 1) * HEAD_DIM] for g in range(ATTN_KV_HEADS)]
    vT = jnp.transpose(kv[:, ATTN_KV_WIDTH:].astype(F32)).astype(BF16)
    neg_prev = jnp.where(_seq_pos(tile * nt, lay)[0], NEG_INF, 0.0)
    neg_next = jnp.where(_seq_pos(tile * nt + nt - 1, lay)[1], NEG_INF, 0.0)

    qbs = [q_ref[jb * BLK:(jb + 1) * BLK, :] for jb in range(nt)]

    def scores(jb, hp):
        slab = qbs[jb][:, hp * LANES:(hp + 1) * LANES]
        q2 = jnp.concatenate([slab[:, :HEAD_DIM], slab[:, HEAD_DIM:]], axis=0)
        return _dot_nt(kn[2 * hp // grp][jb * BLK:(jb + 3) * BLK], q2) + bias_ref[hp]

    n_pairs = ATTN_HEADS // 2
    items = [(jb, hp) for jb in range(nt) for hp in range(n_pairs)]
    pending = [scores(*it) for it in items[:_ATTN_LOOKAHEAD]]
    outs = []
    waiting = []

    def value_matmul():
        jb, hp, p, inv = waiting.pop(0)
        g = 2 * hp // grp
        o = _dot(vT[g * HEAD_DIM:(g + 1) * HEAD_DIM, jb * BLK:(jb + 3) * BLK], p) * inv
        outs.extend([o[:, :BLK], o[:, BLK:]])
        if hp == n_pairs - 1:
            put_block(jb, jnp.transpose(jnp.concatenate(outs, axis=0)).astype(BF16))
            outs.clear()

    for idx, (jb, hp) in enumerate(items):
        s = pending.pop(0)
        if idx + _ATTN_LOOKAHEAD < len(items):
            pending.append(scores(*items[idx + _ATTN_LOOKAHEAD]))
        if len(waiting) > _PV_DELAY:
            value_matmul()
        if jb == 0:
            s = jnp.concatenate([s[:BLK] + neg_prev, s[BLK:]], axis=0)
        if jb == nt - 1:
            s = jnp.concatenate([s[:2 * BLK], s[2 * BLK:] + neg_next], axis=0)
        snk = sink_ref[hp:hp + 1, :]
        m = jnp.maximum(jnp.max(s, axis=0, keepdims=True), snk)
        p = jnp.exp2(s - m)
        denom = jnp.sum(p, axis=0, keepdims=True) + jnp.exp2(snk - m)
        waiting.append((jb, hp, p.astype(BF16), 1.0 / denom))
        after_item(idx)
    while waiting:
        value_matmul()


_SPLIT_TERMS = 3
_REPLICAS = LANES // DT_REP


def _lane_split(x, terms=_SPLIT_TERMS):
    rep = lax.broadcasted_iota(jnp.int32, x.shape, 1) // DT_REP
    out = jnp.zeros(x.shape, BF16)
    rem = x
    for r in range(terms):
        term = rem.astype(BF16)
        out = jnp.where(rep == r, term, out)
        rem = rem - term.astype(F32)
    return out


def _sublane_split(x):
    terms, rem = [], x
    for _ in range(_SPLIT_TERMS):
        term = rem.astype(BF16)
        terms.append(term)
        rem = rem - term.astype(F32)
    return jnp.concatenate(terms, axis=0)


_EXPAND_TERMS = 2


def _expand(x, e_ref):
    return _dot(_lane_split(x, _EXPAND_TERMS), e_ref[...])


def _tri(upper):
    r = lax.broadcasted_iota(jnp.int32, (BLK, BLK), 0)
    c = lax.broadcasted_iota(jnp.int32, (BLK, BLK), 1)
    return (c >= r) if upper else (c <= r)


def _ones_where(mask):
    return jnp.where(mask, 1.0, 0.0).astype(BF16)


_SSD_TILE_CHUNKS = 8


def _rows(j):
    return slice(j * BLK, (j + 1) * BLK)


def _dt_row_major(dt_raw, dtb, alog):
    L = BLK
    dt = _softplus(dt_raw + dtb)
    sp = _lane_split(dt * (-jnp.exp(alog)))
    lane = lax.broadcasted_iota(jnp.int32, (L, LANES), 1)
    is_fwd = (lane & (DT_REP - 1)) < SSD_HEADS
    part = jnp.where(is_fwd, _dot(_ones_where(_tri(False)), sp), _dot(_ones_where(_tri(True)), sp))
    acum = part
    for r in range(1, _REPLICAS):
        acum = acum + pltpu.roll(part, r * DT_REP, axis=1)
    a_end = jnp.where(is_fwd[0:1], acum[L - 1:L], acum[0:1])
    return dt, acum, a_end


def _dt_head_major(dtT_raw, dtbT, alogT):
    nh = SSD_HEADS
    dtT = _softplus(dtT_raw + dtbT)
    spT = _sublane_split(dtT * (-jnp.exp(alogT)))
    pf = _dot(spT, _ones_where(_tri(True)))
    pb = _dot(spT, _ones_where(_tri(False)))
    acumT_f = pf[0:nh]
    acumT_b = pb[nh:2 * nh]
    for r in range(1, _SPLIT_TERMS):
        acumT_f = acumT_f + pf[r * DT_REP:r * DT_REP + nh]
        acumT_b = acumT_b + pb[r * DT_REP + nh:(r + 1) * DT_REP]
    log2_dtT = jnp.log(dtT) * _LOG2E
    return acumT_f * _LOG2E - log2_dtT[0:nh], acumT_b * _LOG2E - log2_dtT[nh:2 * nh]


def _state_factors(dt, acum, a_end, e_ref):
    ea_exp = _expand(jnp.exp(acum), e_ref)
    w_exp = _expand(dt * jnp.exp(a_end - acum), e_ref)
    cd_exp = _expand(jnp.broadcast_to(jnp.exp(a_end), (8, LANES)), e_ref)[0:1]
    return ea_exp, w_exp, cd_exp


def _state_sweep(st_ref, order, xs, bm, cm_bf, fac):
    gw = SSD_INNER // SSD_GROUPS
    upd = {}
    for j in order:
        wx = (xs[j] * fac[j][1]).astype(BF16)
        upd[j] = [_dot(jnp.transpose(bm[j][g]).astype(BF16), wx[:, g * gw:(g + 1) * gw])
                  for g in range(SSD_GROUPS)]
    sg = [st_ref[g] for g in range(SSD_GROUPS)]
    out = {}
    for j in order:
        y_off = [_dot(cm_bf[j][g], sg[g].astype(BF16)) for g in range(SSD_GROUPS)]
        neg_cd = -fac[j][2]
        sg = [upd[j][g] - sg[g] * neg_cd[:, g * gw:(g + 1) * gw] for g in range(SSD_GROUPS)]
        out[j] = jnp.concatenate(y_off, axis=1) * fac[j][0]
    for g in range(SSD_GROUPS):
        st_ref[g] = sg[g]
    return out


def _split_bc(bc):
    n = SSD_STATE
    return ([bc[:, g * n:(g + 1) * n] for g in range(SSD_GROUPS)],
            [bc[:, (SSD_GROUPS + g) * n:(SSD_GROUPS + g + 1) * n] for g in range(SSD_GROUPS)])


def _ssd_bwd_sweep(xs_ref, bc_ref, dt_ref, dtb_ref, alog_ref, eb_ref, yo_ref, st_ref):
    order = list(reversed(range(_SSD_TILE_CHUNKS)))
    rm = {j: _dt_row_major(dt_ref[_rows(j), :], dtb_ref[...], alog_ref[...]) for j in order}
    fac = {j: _state_factors(*rm[j], eb_ref) for j in order}
    xs = {j: xs_ref[_rows(j), :].astype(F32) for j in order}
    bm = {j: _split_bc(bc_ref[_rows(j), :].astype(F32))[0] for j in order}
    cm_bf = {j: _split_bc(bc_ref[_rows(j), :])[1] for j in order}
    y_off = _state_sweep(st_ref, order, xs, bm, cm_bf, fac)
    for j in order:
        yo_ref[_rows(j), :] = y_off[j].astype(BF16)


def _ssd_body(xsf_ref, bcf_ref, dtf_ref, dtTf_ref, xsb_ref, bcb_ref, dtb_ref, bias_ref, alog_ref, biasT_ref, alogT_ref,
              dskip_ref, ef_ref, eb_ref, y_ref, yob_ref, stf_ref, stb_ref, *, lay):
    ns = _SSD_TILE_CHUNKS
    i = pl.program_id(0)
    n_steps = lay.n_blocks // ns

    @pl.when(_seq_pos(i * ns, lay)[0])
    def _():
        stf_ref[...] = jnp.zeros(stf_ref.shape, F32)

    @pl.when(_seq_pos((n_steps - 1 - i) * ns + ns - 1, lay)[1])
    def _():
        stb_ref[...] = jnp.zeros(stb_ref.shape, F32)

    L, nh = BLK, SSD_HEADS
    hpg = nh // SSD_GROUPS
    order = list(range(ns))
    lane = lax.broadcasted_iota(jnp.int32, (L, LANES), 1)
    xs_bf = {j: xsf_ref[_rows(j), :] for j in order}
    bc_bf = {j: bcf_ref[_rows(j), :] for j in order}
    bm_bf = {j: _split_bc(bc_bf[j])[0] for j in order}
    cm_bf = {j: _split_bc(bc_bf[j])[1] for j in order}
    cb = {j: [_dot_nt(cm_bf[j][g], bm_bf[j][g]) for g in range(SSD_GROUPS)] for j in order}
    rm = {j: _dt_row_major(dtf_ref[_rows(j), :], bias_ref[...], alog_ref[...]) for j in order}
    hm = {j: _dt_head_major(dtTf_ref[:, _rows(j)], biasT_ref[...], alogT_ref[...]) for j in order}

    _ssd_bwd_sweep(xsb_ref, bcb_ref, dtb_ref, bias_ref, alog_ref, eb_ref, yob_ref, stb_ref)

    fac = {j: _state_factors(*rm[j], ef_ref) for j in order}
    xs = {j: xs_bf[j].astype(F32) for j in order}
    bm, coef_exp = {}, {}
    for j in order:
        bm[j], cm = _split_bc(bc_bf[j].astype(F32))
        cbd = jnp.where((lane & (nh - 1)) < hpg, jnp.sum(cm[0] * bm[j][0], axis=1, keepdims=True),
                        jnp.sum(cm[1] * bm[j][1], axis=1, keepdims=True))
        coef_exp[j] = _expand(dskip_ref[...] + cbd * rm[j][0], eb_ref)
    y_off = _state_sweep(stf_ref, order, xs, bm, cm_bf, fac)

    causal = _tri(False)
    left = lane < SSD_HEAD_DIM
    zero_bf = jnp.zeros((L, LANES), BF16)
    acum2 = {j: rm[j][1] * _LOG2E for j in order}
    y_loc = {j: [] for j in order}
    for j in order:
        for k in range(nh // 2):
            g = (2 * k) // hpg
            lhs = []
            for h in (2 * k, 2 * k + 1):
                col_f = jnp.broadcast_to(acum2[j][:, h:h + 1], (L, L))
                col_b = jnp.broadcast_to(acum2[j][:, nh + h:nh + h + 1], (L, L))
                e = jnp.where(causal, col_f - hm[j][0][h:h + 1, :], col_b - hm[j][1][h:h + 1, :])
                lhs.append((cb[j][g] * jnp.exp2(e)).astype(BF16))
            xs_pair = xs_bf[j][:, k * LANES:(k + 1) * LANES]
            rhs = jnp.concatenate([jnp.where(left, xs_pair, zero_bf), jnp.where(left, zero_bf, xs_pair)], axis=0)
            y_loc[j].append(_dot(jnp.concatenate(lhs, axis=1), rhs))
    for j in order:
        y = jnp.concatenate(y_loc[j], axis=1) + coef_exp[j] * xs[j] + y_off[j]
        y_ref[_rows(j), :] = y.astype(BF16)


def _ssd(proj, dt, dtT, p, e_mats, lay):
    t = proj.shape[0]
    ns = _SSD_TILE_CHUNKS
    assert lay.per_seq_first % ns == 0 and lay.per_seq_second % ns == 0 and lay.n_first % ns == 0
    nb = lay.n_blocks // ns
    rows = ns * BLK
    xs_blk, bc_blk = COL_XS // SSD_INNER, COL_BC // SSD_BC_WIDTH
    state = pltpu.VMEM((SSD_GROUPS, SSD_STATE, SSD_INNER // SSD_GROUPS), F32)

    def rev(c):
        return nb - 1 - c

    return pl.pallas_call(
        functools.partial(_ssd_body, lay=lay),
        grid=(nb,),
        in_specs=[
            pl.BlockSpec((rows, SSD_INNER), lambda c: (c, xs_blk)),
            pl.BlockSpec((rows, SSD_BC_WIDTH), lambda c: (c, bc_blk)),
            pl.BlockSpec((rows, LANES), lambda c: (c, 0)),
            pl.BlockSpec((DT_REP, rows), lambda c: (0, c)),
            pl.BlockSpec((rows, SSD_INNER), lambda c: (rev(c), xs_blk)),
            pl.BlockSpec((rows, SSD_BC_WIDTH), lambda c: (rev(c), bc_blk)),
            pl.BlockSpec((rows, LANES), lambda c: (rev(c), 0)),
            _const_spec((1, LANES)), _const_spec((1, LANES)),
            _const_spec((DT_REP, 1)), _const_spec((DT_REP, 1)),
            _const_spec((1, LANES)),
            _const_spec((LANES, SSD_INNER)), _const_spec((LANES, SSD_INNER)),
        ],
        out_specs=[
            pl.BlockSpec((rows, SSD_INNER), lambda c: (c, 0)),
            pl.BlockSpec((rows, SSD_INNER), lambda c: (rev(c), 0)),
        ],
        out_shape=[jax.ShapeDtypeStruct((t, SSD_INNER), BF16), jax.ShapeDtypeStruct((t, SSD_INNER), BF16)],
        scratch_shapes=[state, state],
        compiler_params=_params(("arbitrary",)),
        name="ssd",
    )(proj, proj, dt, dtT, proj, proj, dt, p["dtb"], p["alog"], p["dtbT"], p["alogT"], p["dskip"],
      e_mats[0], e_mats[1])


_ROW_TILE = 512
_MERGE_COLS = 256


def _attn_merge_body(q_ref, kvp_ref, kvm_ref, kvn_ref, bias_ref, sink_ref,
                     xa_ref, xb_ref, gu_ref, gv_ref, yc_ref, yob_ref, z_ref, g0_ref, g1_ref, g2_ref, ng_ref, ws_ref, bs_ref,
                     wa_ref, wb_ref, wc_ref, wo_ref, ha_ref, hb_ref, ya_ref, m_ref, *, lay, n_a, n_tiles):
    i = pl.program_id(0)
    slot = lax.rem(i, 2)

    @pl.when(i == 0)
    def _():
        ya_ref[...] = jnp.zeros(ya_ref.shape, BF16)

    ya = ya_ref[1 - slot]
    gw = GMLP_WIDTH // GMLP_GROUPS
    chunks = []
    for c in range(_ROW_TILE // BLK):
        r = slice(c * BLK, (c + 1) * BLK)
        chunks.append(jnp.concatenate(
            [gu_ref[r, g * gw:(g + 1) * gw].astype(F32)
             * (_dot(ws_ref[g], gv_ref[r, g * gw:(g + 1) * gw]) + bs_ref[:, g * gw:(g + 1) * gw])
             for g in range(GMLP_GROUPS)], axis=1))
    yb = jnp.concatenate(chunks, axis=0).astype(BF16)
    y = (yc_ref[...].astype(F32) + yob_ref[...].astype(F32)) * z_ref[...].astype(F32)
    sw = SSD_INNER // SSD_GROUPS
    yn = jnp.concatenate([_rms(y[:, g * sw:(g + 1) * sw], ng_ref[:, g * sw:(g + 1) * sw])
                          for g in range(SSD_GROUPS)], axis=1).astype(BF16)

    acc = {}

    def piece(c, k):
        cols = slice(c * _MERGE_COLS, (c + 1) * _MERGE_COLS)

        def run():
            if k == 0:
                acc[c] = g0_ref[:, cols].astype(F32) * _dot(ya, wa_ref[:, cols])
            elif k == 1:
                acc[c] = acc[c] + g1_ref[:, cols].astype(F32) * _dot(yb, wb_ref[:, cols])
            else:
                m_ref[:, cols] = (acc[c] + g2_ref[:, cols].astype(F32) * _dot(yn, wc_ref[:, cols])).astype(BF16)
        return run

    pieces = [piece(c, k) for c in range(D_MODEL // _MERGE_COLS) for k in range(3)]
    n_items = _ATTN_TILE_BLOCKS * ATTN_HEADS // 2
    done = [0]

    def after_item(idx):
        while done[0] < (idx + 1) * len(pieces) // n_items:
            pieces[done[0]]()
            done[0] += 1

    def put_block(jb, val):
        ya_ref[slot, jb * BLK:(jb + 1) * BLK, :] = val

    _attention_tile(jnp.minimum(i, n_tiles - 1), q_ref, kvp_ref, kvm_ref, kvn_ref, bias_ref, sink_ref,
                    put_block, after_item, lay)

    m = m_ref[...]
    in_first = jnp.maximum(i - 1, 0) < n_a

    @pl.when(in_first)
    def _():
        ha_ref[...] = xa_ref[...] + _dot(m, wo_ref[...])

    @pl.when(jnp.logical_not(in_first))
    def _():
        hb_ref[...] = xb_ref[...] + _dot(m, wo_ref[...])


def _attn_merge(xa, xb, yc, yob, proj, bias, p, lay):
    tm = _ROW_TILE
    nt = _ATTN_TILE_BLOCKS
    assert tm == nt * BLK
    assert lay.per_seq_first % nt == 0 and lay.per_seq_second % nt == 0 and lay.n_first % nt == 0
    n_a, n_b = xa.shape[0] // tm, xb.shape[0] // tm
    n_tiles = n_a + n_b
    g_blk = COL_GATE // D_MODEL
    kv_w = 2 * ATTN_KV_WIDTH
    kv_blk = COL_KV // kv_w
    last_blk = lay.n_blocks - 1

    def att(i):
        return jnp.minimum(i, n_tiles - 1)

    def mrg(i):
        return jnp.maximum(i - 1, 0)

    def row(width, col_blk=0):
        return pl.BlockSpec((tm, width), lambda i: (mrg(i), col_blk))

    stream = (pl.BlockSpec((tm, D_MODEL), lambda i: (jnp.minimum(mrg(i), n_a - 1), 0)),
              pl.BlockSpec((tm, D_MODEL), lambda i: (jnp.clip(mrg(i) - n_a, 0, n_b - 1), 0)))
    return pl.pallas_call(
        functools.partial(_attn_merge_body, lay=lay, n_a=n_a, n_tiles=n_tiles),
        grid=(n_tiles + 1,),
        in_specs=[
            pl.BlockSpec((tm, ATTN_Q_WIDTH), lambda i: (att(i), COL_Q // ATTN_Q_WIDTH)),
            pl.BlockSpec((BLK, kv_w), lambda i: (jnp.maximum(att(i) * nt - 1, 0), kv_blk)),
            pl.BlockSpec((tm, kv_w), lambda i: (att(i), kv_blk)),
            pl.BlockSpec((BLK, kv_w), lambda i: (jnp.minimum(att(i) * nt + nt, last_blk), kv_blk)),
            _const_spec((ATTN_HEADS // 2, 3 * BLK, 2 * BLK)),
            _const_spec((ATTN_HEADS // 2, 2 * BLK)),
            *stream, row(GMLP_WIDTH, COL_GU // GMLP_WIDTH), row(GMLP_WIDTH, COL_GV // GMLP_WIDTH),
            row(SSD_INNER), row(SSD_INNER), row(SSD_INNER, COL_Z // SSD_INNER),
            row(D_MODEL, g_blk), row(D_MODEL, g_blk + 1), row(D_MODEL, g_blk + 2),
            _const_spec((1, SSD_INNER)),
            _const_spec((GMLP_GROUPS, BLK, BLK)), _const_spec((BLK, GMLP_WIDTH)),
            _const_spec((ATTN_Q_WIDTH, D_MODEL)), _const_spec((GMLP_WIDTH, D_MODEL)),
            _const_spec((SSD_INNER, D_MODEL)), _const_spec((D_MODEL, D_MODEL)),
        ],
        out_specs=list(stream),
        out_shape=[jax.ShapeDtypeStruct(xa.shape, F32), jax.ShapeDtypeStruct(xb.shape, F32)],
        scratch_shapes=[pltpu.VMEM((2, tm, ATTN_Q_WIDTH), BF16), pltpu.VMEM((tm, D_MODEL), BF16)],
        compiler_params=_params(("arbitrary",)),
        name="attn_merge",
    )(proj, proj, proj, proj, bias, p["sink"],
      xa, xb, proj, proj, yc, yob, proj, proj, proj, proj, p["ssd_norm_g"], p["w_s"], p["b_s"],
      p["w_a"], p["w_b"], p["w_c"], p["w_o"])


_FF_CHUNK = 1024


def _ffn_body(ha_ref, hb_ref, g_ref, w1_ref, w2_ref, oa_ref, ob_ref, *, n_a):
    def run(h_ref, o_ref):
        h = h_ref[...]
        hn = _rms(h, g_ref[...]).astype(BF16)
        acc = h
        for c0 in range(0, FF_DIM, _FF_CHUNK):
            a = jnp.maximum(_dot(hn, w1_ref[:, c0:c0 + _FF_CHUNK]), 0.0)
            acc = acc + _dot((a * a).astype(BF16), w2_ref[c0:c0 + _FF_CHUNK, :])
        o_ref[...] = acc

    in_first = pl.program_id(0) < n_a
    pl.when(in_first)(lambda: run(ha_ref, oa_ref))
    pl.when(jnp.logical_not(in_first))(lambda: run(hb_ref, ob_ref))


def _ffn(ha, hb, g, w1, w2):
    tm = _ROW_TILE
    n_a, n_b = ha.shape[0] // tm, hb.shape[0] // tm
    stream = _two_stream_specs(tm, n_a, n_b)
    return pl.pallas_call(
        functools.partial(_ffn_body, n_a=n_a),
        grid=(n_a + n_b,),
        in_specs=[*stream, _const_spec((1, D_MODEL)), _const_spec((D_MODEL, FF_DIM)), _const_spec((FF_DIM, D_MODEL))],
        out_specs=list(stream),
        out_shape=[jax.ShapeDtypeStruct(ha.shape, F32), jax.ShapeDtypeStruct(hb.shape, F32)],
        compiler_params=_params(("arbitrary",)),
        name="ffn",
    )(ha, hb, g, w1, w2)


def _t5_bucket(rel):
    nb = NUM_BUCKETS // 2
    max_exact = nb // 2
    ret = jnp.where(rel > 0, nb, 0)
    n = jnp.abs(rel)
    n_safe = jnp.maximum(n, 1).astype(F32)
    large = max_exact + (jnp.log(n_safe / max_exact) / math.log(MAX_DISTANCE / max_exact)
                         * (nb - max_exact)).astype(jnp.int32)
    large = jnp.minimum(large, nb - 1)
    return ret + jnp.where(n < max_exact, n, large)


def _bias_table(rel_bias):
    qi = jnp.arange(BLK)[:, None]
    ki = jnp.arange(3 * BLK)[None, :]
    rel = ki - BLK - qi
    onehot = (_t5_bucket(rel)[..., None] == jnp.arange(NUM_BUCKETS)).astype(F32)
    bias = jnp.einsum("qkb,bh->hkq", onehot, rel_bias.astype(F32), precision=lax.Precision.HIGHEST) * _LOG2E
    bias = jnp.where((jnp.abs(rel) <= WINDOW).T[None], bias, NEG_INF)
    return jnp.concatenate([bias[0::2], bias[1::2]], axis=-1)


def _expand_matrix(d):
    k = np.arange(LANES)[:, None]
    col = np.arange(SSD_INNER)[None, :]
    hit = ((k % DT_REP) == d * SSD_HEADS + col // SSD_HEAD_DIM)
    return jnp.asarray(hit, dtype=BF16)


def _prep_layer_params(w_in, norm_mix_g, q_norm_g, k_norm_g, attn_sink, gmlp_ln_g, gmlp_ln_b, w_spatial,
                       b_spatial, conv_w, conv_b, dt_bias, a_log, d_skip, ssd_norm_g, w_up_attn, w_up_gmlp,
                       w_up_ssd, w_out, norm_ff_g, w_ff1, w_ff2):
    depth = w_in.shape[0]

    def cols(a, b):
        return w_in[:, :, a:b]

    w_main = jnp.concatenate([
        cols(_R_Z, _R_XS), cols(_R_XS, _R_B), cols(_R_GATE, _R_END), cols(_R_Q, _R_K),
        cols(_R_GU, _R_GV), cols(_R_GV, _R_Z), cols(_R_B, _R_DT), cols(_R_K, _R_GU)], axis=-1).astype(BF16)
    w_dt = cols(_R_DT, _R_GATE)
    rep = LANES // DT_REP
    gw = GMLP_WIDTH // GMLP_GROUPS

    def dt_lanes(v):
        return jnp.tile(v.reshape(depth, 1, DT_REP), (1, 1, rep))

    return dict(
        norm_mix_g=norm_mix_g[:, None, :],
        w_main=w_main,
        w_dt=jnp.tile(w_dt, (1, 1, rep)).astype(BF16),
        w_dtT=jnp.swapaxes(w_dt, 1, 2).astype(BF16),
        kscale=jnp.tile(q_norm_g * k_norm_g * (HEAD_DIM ** -0.5 * _LOG2E), (1, ATTN_KV_HEADS))[:, None, :],
        sink=jnp.repeat(attn_sink * _LOG2E, BLK, axis=-1).reshape(depth, ATTN_HEADS // 2, 2 * BLK),
        ln_g=gmlp_ln_g[:, None, :], ln_b=gmlp_ln_b[:, None, :],
        w_s=w_spatial.astype(BF16),
        b_s=jnp.repeat(jnp.swapaxes(b_spatial, 1, 2), gw, axis=-1),
        conv_w=conv_w, conv_b=conv_b[:, None, :],
        dtb=dt_lanes(dt_bias), alog=dt_lanes(a_log),
        dtbT=dt_bias.reshape(depth, DT_REP, 1), alogT=a_log.reshape(depth, DT_REP, 1),
        dskip=dt_lanes(jnp.concatenate([d_skip, d_skip], axis=-1)),
        ssd_norm_g=ssd_norm_g[:, None, :],
        w_a=w_up_attn.astype(BF16), w_b=w_up_gmlp.astype(BF16), w_c=w_up_ssd.astype(BF16),
        w_o=w_out.astype(BF16),
        norm_ff_g=norm_ff_g[:, None, :],
        w1=w_ff1.astype(BF16), w2=w_ff2.astype(BF16),
    )


def _layer(xa, xb, p, bias, e_mats, lay):
    proj, dt, dtT = _inproj(xa, xb, p, lay)
    yc, yob = _ssd(proj, dt, dtT, p, e_mats, lay)
    ha, hb = _attn_merge(xa, xb, yc, yob, proj, bias, p, lay)
    return _ffn(ha, hb, p["norm_ff_g"], p["w1"], p["w2"])


def _encoder(x_first, x_second, rel_bias, layer_params):
    b1, s1, _ = x_first.shape
    b2, s2, _ = x_second.shape
    t1, t2 = b1 * s1, b2 * s2
    lay = Layout(n_blocks=(t1 + t2) // BLK, n_first=t1 // BLK, per_seq_first=s1 // BLK,
                 per_seq_second=s2 // BLK)
    bias = _bias_table(rel_bias)
    e_mats = (_expand_matrix(0), _expand_matrix(1))
    params = _prep_layer_params(**layer_params)

    xa, xb = x_first.reshape(t1, D_MODEL), x_second.reshape(t2, D_MODEL)
    for layer in range(layer_params["w_in"].shape[0]):
        xa, xb = _layer(xa, xb, {k: v[layer] for k, v in params.items()}, bias, e_mats, lay)
    return xa.reshape(b1, s1, D_MODEL), xb.reshape(b2, s2, D_MODEL)


def kernel(x_prompt, x_sample, rel_bias, norm_mix_g, w_in, q_norm_g, k_norm_g, attn_sink, gmlp_ln_g, gmlp_ln_b,
           w_spatial, b_spatial, conv_w, conv_b, dt_bias, a_log, d_skip, ssd_norm_g, w_up_attn, w_up_gmlp,
           w_up_ssd, w_out, norm_ff_g, w_ff1, w_ff2):
    layer_params = dict(
        w_in=w_in, norm_mix_g=norm_mix_g, q_norm_g=q_norm_g, k_norm_g=k_norm_g, attn_sink=attn_sink,
        gmlp_ln_g=gmlp_ln_g, gmlp_ln_b=gmlp_ln_b, w_spatial=w_spatial, b_spatial=b_spatial, conv_w=conv_w,
        conv_b=conv_b, dt_bias=dt_bias, a_log=a_log, d_skip=d_skip, ssd_norm_g=ssd_norm_g,
        w_up_attn=w_up_attn, w_up_gmlp=w_up_gmlp, w_up_ssd=w_up_ssd, w_out=w_out, norm_ff_g=norm_ff_g,
        w_ff1=w_ff1, w_ff2=w_ff2)
    return _encoder(x_prompt, x_sample, rel_bias, layer_params)
```

```python
import functools
import math
from typing import NamedTuple

import numpy as np
import jax
import jax.numpy as jnp
from jax import lax
from jax.experimental import pallas as pl
from jax.experimental.pallas import tpu as pltpu

F32 = jnp.float32
BF16 = jnp.bfloat16

D_MODEL = 1024
ATTN_HEADS = 8
ATTN_KV_HEADS = 2
HEAD_DIM = 64
ATTN_Q_WIDTH = ATTN_HEADS * HEAD_DIM
ATTN_KV_WIDTH = ATTN_KV_HEADS * HEAD_DIM
WINDOW = 128
NUM_BUCKETS = 32
MAX_DISTANCE = 128
GMLP_WIDTH = 512
GMLP_GROUPS = 4
SSD_INNER = 1024
SSD_HEAD_DIM = 64
SSD_HEADS = SSD_INNER // SSD_HEAD_DIM
SSD_GROUPS = 2
SSD_STATE = 128
N_DIR = 2
FF_DIM = 4 * D_MODEL
EPS = 1e-6
NEG_INF = -1e30

BLK = 128
LANES = 128
BF16_SUBLANES = 16
VMEM_LIMIT = 56 * 1024 * 1024
_LOG2E = math.log2(math.e)

COL_Z = 0
COL_XS = 1024
COL_GATE = 2048
COL_Q = 5120
COL_GU = 5632
COL_GV = 6144
COL_BC = 6656
COL_KV = 7168
P_WIDTH = 7424
DT_REP = 32
SSD_BC_WIDTH = 2 * SSD_GROUPS * SSD_STATE

_R_Q, _R_K, _R_V, _R_GU, _R_GV, _R_Z, _R_XS, _R_B, _R_C, _R_DT, _R_GATE, _R_END = (
    0, 512, 640, 768, 1280, 1792, 2816, 3840, 4096, 4352, 4384, 7456)


class Layout(NamedTuple):
    n_blocks: int
    n_first: int
    per_seq_first: int
    per_seq_second: int


def _seq_pos(blk, lay):
    in_first = blk < lay.n_first
    pos = jnp.where(in_first, lax.rem(blk, lay.per_seq_first),
                    lax.rem(blk - lay.n_first, lay.per_seq_second))
    last = jnp.where(in_first, lay.per_seq_first - 1, lay.per_seq_second - 1)
    return pos == 0, pos == last


def _sigmoid(x):
    return 0.5 * jnp.tanh(0.5 * x) + 0.5


def _silu(x):
    return x * _sigmoid(x)


def _gelu(x):
    return x * (0.5 * (1.0 + jnp.tanh(math.sqrt(2.0 / math.pi) * (x + 0.044715 * (x * x * x)))))


def _softplus(x):
    return jnp.maximum(x, 0.0) + jnp.log1p(jnp.exp(-jnp.abs(x)))


def _rms(x, g):
    return x * lax.rsqrt(jnp.mean(x * x, axis=-1, keepdims=True) + EPS) * g


def _dot(a, b):
    return jnp.dot(a, b, preferred_element_type=F32)


def _dot_nt(a, b):
    return lax.dot_general(a, b, (((1,), (1,)), ((), ())), preferred_element_type=F32)


def _const_spec(shape):
    nd = len(shape)
    return pl.BlockSpec(shape, lambda *_: (0,) * nd)


def _params(semantics):
    return pltpu.CompilerParams(dimension_semantics=semantics, vmem_limit_bytes=VMEM_LIMIT)


_PROJ_TILE = 512
_PROJ_LOOKAHEAD = 1
_X_HALO = 8


def _split2(x):
    hi = x.astype(BF16)
    return hi, (x - hi.astype(F32)).astype(BF16)


def _inproj_body(xa_ref, xb_ref, xa_prev_ref, xb_prev_ref, xa_next_ref, xb_next_ref, g_ref, w_ref, wdt_ref, wdtT_ref,
                 cw_ref, cb_ref, lng_ref, lnb_ref, kscale_ref, qsum_ref, ksum_ref, o_ref, dt_ref, dtT_ref, *, lay):
    tm = _PROJ_TILE
    nblk = tm // BLK
    i = pl.program_id(0)
    keep_prev = jnp.where(_seq_pos(i * nblk, lay)[0], 0.0, 1.0)
    keep_next = jnp.where(_seq_pos(i * nblk + nblk - 1, lay)[1], 0.0, 1.0)
    in_first = i < lay.n_first // nblk

    row_limit = jnp.where(in_first, tm, 0)

    def pick(a_ref, b_ref):
        a = a_ref[...]
        return jnp.where(lax.broadcasted_iota(jnp.int32, a.shape, 0) < row_limit, a, b_ref[...])

    xn = _rms(pick(xa_ref, xb_ref), g_ref[...]).astype(BF16)
    xh = _rms(jnp.concatenate([pick(xa_prev_ref, xb_prev_ref), pick(xa_next_ref, xb_next_ref)], axis=0),
              g_ref[...]).astype(BF16)

    def put(c0, val):
        o_ref[:, c0:c0 + val.shape[1]] = val.astype(BF16)

    def conv_silu(k0):
        def post(c0, y):
            width = y.shape[1]
            yh = _dot(xh, w_ref[:, c0:c0 + width])
            halo_p = yh[_X_HALO - 1:_X_HALO] * keep_prev
            halo_n = yh[_X_HALO:_X_HALO + 1] * keep_next
            up = pltpu.roll(y, 1, axis=0)
            dn = pltpu.roll(y, tm - 1, axis=0)
            r8 = lax.broadcasted_iota(jnp.int32, (8, width), 0)
            y_m1 = jnp.concatenate([jnp.where(r8 == 0, halo_p, up[:8]), up[8:]], axis=0)
            y_p1 = jnp.concatenate([dn[:tm - 8], jnp.where(r8 == 7, halo_n, dn[tm - 8:])], axis=0)
            w = cw_ref[:, k0:k0 + width]
            put(c0, _silu(cb_ref[:, k0:k0 + width] + y_m1 * w[0:1] + y * w[1:2] + y_p1 * w[2:3]))
        return post

    def q_norm(c0, q):
        sq = jnp.concatenate(_split2(q * q), axis=1)

        def tail():
            qss = _dot(sq, qsum_ref[...])
            put(c0, q * lax.rsqrt(qss * (1.0 / HEAD_DIM) + EPS))
        return tail

    def gelu_ln(c0, y):
        vf = _gelu(y)
        vc = vf - jnp.mean(vf, axis=-1, keepdims=True)
        var = jnp.mean(vc * vc, axis=-1, keepdims=True)
        put(c0, vc * lax.rsqrt(var + EPS) * lng_ref[...] + lnb_ref[...])

    def kv_norm(c0, kv):
        kf = kv[:, :ATTN_KV_WIDTH]
        sq = jnp.concatenate(_split2(kf * kf), axis=1)
        put(c0 + ATTN_KV_WIDTH, kv[:, ATTN_KV_WIDTH:])

        def tail():
            kss = _dot(sq, ksum_ref[...])
            put(c0, kf * lax.rsqrt(kss * (1.0 / HEAD_DIM) + EPS) * kscale_ref[...])
        return tail

    def gate(k):
        return (COL_GATE + k * D_MODEL, D_MODEL, lambda c0, y: put(c0, _sigmoid(y)))

    items = [
        (COL_XS, SSD_INNER, conv_silu(0)),
        (COL_Z, SSD_INNER, lambda c0, y: put(c0, _silu(y))),
        (COL_BC, SSD_BC_WIDTH, conv_silu(SSD_INNER)),
        gate(0),
        (COL_GV, GMLP_WIDTH, gelu_ln),
        gate(1),
        (COL_GU, GMLP_WIDTH, lambda c0, y: put(c0, _gelu(y))),
        gate(2),
        (COL_Q, ATTN_Q_WIDTH, q_norm),
        (COL_KV, 2 * ATTN_KV_WIDTH, kv_norm),
    ]

    def proj(k):
        c0, width, _ = items[k]
        return _dot(xn, w_ref[:, c0:c0 + width])

    pending = [proj(k) for k in range(_PROJ_LOOKAHEAD)]
    tail = None
    for k, (c0, _, post) in enumerate(items):
        y = pending.pop(0)
        if k + _PROJ_LOOKAHEAD < len(items):
            pending.append(proj(k + _PROJ_LOOKAHEAD))
        if tail is not None:
            tail()
        tail = post(c0, y)
    dt_ref[...] = _dot(xn, wdt_ref[...])
    dtT_ref[...] = _dot_nt(wdtT_ref[...], xn)
    if tail is not None:
        tail()


def _head_sum_matrix(width):
    k = np.arange(2 * width)[:, None]
    n = np.arange(width)[None, :]
    return jnp.asarray((k % width) // HEAD_DIM == n // HEAD_DIM, dtype=BF16)


def _two_stream_specs(tm, n_a, n_b):
    return (pl.BlockSpec((tm, D_MODEL), lambda i: (jnp.minimum(i, n_a - 1), 0)),
            pl.BlockSpec((tm, D_MODEL), lambda i: (jnp.clip(i - n_a, 0, n_b - 1), 0)))


def _inproj(xa, xb, p, lay):
    tm = _PROJ_TILE
    n_a, n_b = xa.shape[0] // tm, xb.shape[0] // tm
    t = xa.shape[0] + xb.shape[0]
    assert lay.per_seq_first % (tm // BLK) == 0 and lay.per_seq_second % (tm // BLK) == 0
    hal = tm // _X_HALO
    conv_w = SSD_INNER + SSD_BC_WIDTH

    def halo(tile0, n_tiles, shift):
        last = n_tiles * hal - 1
        return pl.BlockSpec((_X_HALO, D_MODEL),
                            lambda i: (jnp.clip((i - tile0 + shift) * hal - 1 + shift, 0, last), 0))

    return pl.pallas_call(
        functools.partial(_inproj_body, lay=lay),
        grid=(n_a + n_b,),
        in_specs=[
            *_two_stream_specs(tm, n_a, n_b),
            halo(0, n_a, 0), halo(n_a, n_b, 0), halo(0, n_a, 1), halo(n_a, n_b, 1),
            _const_spec((1, D_MODEL)),
            _const_spec((D_MODEL, P_WIDTH)),
            _const_spec((D_MODEL, LANES)),
            _const_spec((DT_REP, D_MODEL)),
            _const_spec((3, conv_w)), _const_spec((1, conv_w)),
            _const_spec((1, GMLP_WIDTH)), _const_spec((1, GMLP_WIDTH)),
            _const_spec((1, ATTN_KV_WIDTH)),
            _const_spec((2 * ATTN_Q_WIDTH, ATTN_Q_WIDTH)),
            _const_spec((2 * ATTN_KV_WIDTH, ATTN_KV_WIDTH)),
        ],
        out_specs=[
            pl.BlockSpec((tm, P_WIDTH), lambda i: (i, 0)),
            pl.BlockSpec((tm, LANES), lambda i: (i, 0)),
            pl.BlockSpec((DT_REP, tm), lambda i: (0, i)),
        ],
        out_shape=[
            jax.ShapeDtypeStruct((t, P_WIDTH), BF16),
            jax.ShapeDtypeStruct((t, LANES), F32),
            jax.ShapeDtypeStruct((DT_REP, t), F32),
        ],
        compiler_params=_params(("parallel",)),
        name="inproj",
    )(xa, xb, xa, xb, xa, xb, p["norm_mix_g"], p["w_main"], p["w_dt"], p["w_dtT"], p["conv_w"], p["conv_b"], p["ln_g"], p["ln_b"],
      p["kscale"], _head_sum_matrix(ATTN_Q_WIDTH), _head_sum_matrix(ATTN_KV_WIDTH))


_ATTN_TILE_BLOCKS = 4
_PV_DELAY = 1
_ATTN_LOOKAHEAD = 1


def _attention_tile(tile, q_ref, kvp_ref, kvm_ref, kvn_ref, bias_ref, sink_ref, put_block, after_item, lay):
    nt = _ATTN_TILE_BLOCKS
    grp = ATTN_HEADS // ATTN_KV_HEADS
    kv = jnp.concatenate([kvp_ref[...], kvm_ref[...], kvn_ref[...]], axis=0)
    kn = [kv[:, g * HEAD_DIM:(g + 1) * HEAD_DIM] for g in range(ATTN_KV_HEADS)]
    vT = jnp.transpose(kv[:, ATTN_KV_WIDTH:].astype(F32)).astype(BF16)
    neg_prev = jnp.where(_seq_pos(tile * nt, lay)[0], NEG_INF, 0.0)
    neg_next = jnp.where(_seq_pos(tile * nt + nt - 1, lay)[1], NEG_INF, 0.0)

    qbs = [q_ref[jb * BLK:(jb + 1) * BLK, :] for jb in range(nt)]

    def scores(jb, hp):
        slab = qbs[jb][:, hp * LANES:(hp + 1) * LANES]
        q2 = jnp.concatenate([slab[:, :HEAD_DIM], slab[:, HEAD_DIM:]], axis=0)
        return _dot_nt(kn[2 * hp // grp][jb * BLK:(jb + 3) * BLK], q2) + bias_ref[hp]

    n_pairs = ATTN_HEADS // 2
    items = [(jb, hp) for jb in range(nt) for hp in range(n_pairs)]
    pending = [scores(*it) for it in items[:_ATTN_LOOKAHEAD]]
    outs = []
    waiting = []

    def value_matmul():
        jb, hp, p, inv = waiting.pop(0)
        g = 2 * hp // grp
        o = _dot(vT[g * HEAD_DIM:(g + 1) * HEAD_DIM, jb * BLK:(jb + 3) * BLK], p) * inv
        outs.extend([o[:, :BLK], o[:, BLK:]])
        if hp == n_pairs - 1:
            put_block(jb, jnp.transpose(jnp.concatenate(outs, axis=0)).astype(BF16))
            outs.clear()

    for idx, (jb, hp) in enumerate(items):
        s = pending.pop(0)
        if idx + _ATTN_LOOKAHEAD < len(items):
            pending.append(scores(*items[idx + _ATTN_LOOKAHEAD]))
        if len(waiting) > _PV_DELAY:
            value_matmul()
        if jb == 0:
            s = jnp.concatenate([s[:BLK] + neg_prev, s[BLK:]], axis=0)
        if jb == nt - 1:
            s = jnp.concatenate([s[:2 * BLK], s[2 * BLK:] + neg_next], axis=0)
        snk = sink_ref[hp:hp + 1, :]
        m = jnp.maximum(jnp.max(s, axis=0, keepdims=True), snk)
        p = jnp.exp2(s - m)
        denom = jnp.sum(p, axis=0, keepdims=True) + jnp.exp2(snk - m)
        waiting.append((jb, hp, p.astype(BF16), 1.0 / denom))
        after_item(idx)
    while waiting:
        value_matmul()


_SPLIT_TERMS = 3
_REPLICAS = LANES // DT_REP


def _lane_split(x, terms=_SPLIT_TERMS):
    rep = lax.broadcasted_iota(jnp.int32, x.shape, 1) // DT_REP
    out = jnp.zeros(x.shape, BF16)
    rem = x
    for r in range(terms):
        term = rem.astype(BF16)
        out = jnp.where(rep == r, term, out)
        rem = rem - term.astype(F32)
    return out


def _sublane_split(x):
    terms, rem = [], x
    for _ in range(_SPLIT_TERMS):
        term = rem.astype(BF16)
        terms.append(term)
        rem = rem - term.astype(F32)
    return jnp.concatenate(terms, axis=0)


_EXPAND_TERMS = 2


def _expand(x, e_ref):
    return _dot(_lane_split(x, _EXPAND_TERMS), e_ref[...])


def _tri(upper):
    r = lax.broadcasted_iota(jnp.int32, (BLK, BLK), 0)
    c = lax.broadcasted_iota(jnp.int32, (BLK, BLK), 1)
    return (c >= r) if upper else (c <= r)


def _ones_where(mask):
    return jnp.where(mask, 1.0, 0.0).astype(BF16)


_SSD_TILE_CHUNKS = 8


def _rows(j):
    return slice(j * BLK, (j + 1) * BLK)


def _dt_row_major(dt_raw, dtb, alog):
    L = BLK
    dt = _softplus(dt_raw + dtb)
    sp = _lane_split(dt * (-jnp.exp(alog)))
    lane = lax.broadcasted_iota(jnp.int32, (L, LANES), 1)
    is_fwd = (lane & (DT_REP - 1)) < SSD_HEADS
    part = jnp.where(is_fwd, _dot(_ones_where(_tri(False)), sp), _dot(_ones_where(_tri(True)), sp))
    acum = part
    for r in range(1, _REPLICAS):
        acum = acum + pltpu.roll(part, r * DT_REP, axis=1)
    a_end = jnp.where(is_fwd[0:1], acum[L - 1:L], acum[0:1])
    return dt, acum, a_end


def _dt_head_major(dtT_raw, dtbT, alogT):
    nh = SSD_HEADS
    dtT = _softplus(dtT_raw + dtbT)
    spT = _sublane_split(dtT * (-jnp.exp(alogT)))
    pf = _dot(spT, _ones_where(_tri(True)))
    pb = _dot(spT, _ones_where(_tri(False)))
    acumT_f = pf[0:nh]
    acumT_b = pb[nh:2 * nh]
    for r in range(1, _SPLIT_TERMS):
        acumT_f = acumT_f + pf[r * DT_REP:r * DT_REP + nh]
        acumT_b = acumT_b + pb[r * DT_REP + nh:(r + 1) * DT_REP]
    log2_dtT = jnp.log(dtT) * _LOG2E
    return acumT_f * _LOG2E - log2_dtT[0:nh], acumT_b * _LOG2E - log2_dtT[nh:2 * nh]


def _state_factors(dt, acum, a_end, e_ref):
    ea_exp = _expand(jnp.exp(acum), e_ref)
    w_exp = _expand(dt * jnp.exp(a_end - acum), e_ref)
    cd_exp = _expand(jnp.broadcast_to(jnp.exp(a_end), (8, LANES)), e_ref)[0:1]
    return ea_exp, w_exp, cd_exp


def _state_sweep(st_ref, order, xs, bm, cm_bf, fac):
    gw = SSD_INNER // SSD_GROUPS
    upd = {}
    for j in order:
        wx = (xs[j] * fac[j][1]).astype(BF16)
        upd[j] = [_dot(jnp.transpose(bm[j][g]).astype(BF16), wx[:, g * gw:(g + 1) * gw])
                  for g in range(SSD_GROUPS)]
    sg = [st_ref[g] for g in range(SSD_GROUPS)]
    out = {}
    for j in order:
        y_off = [_dot(cm_bf[j][g], sg[g].astype(BF16)) for g in range(SSD_GROUPS)]
        neg_cd = -fac[j][2]
        sg = [upd[j][g] - sg[g] * neg_cd[:, g * gw:(g + 1) * gw] for g in range(SSD_GROUPS)]
        out[j] = jnp.concatenate(y_off, axis=1) * fac[j][0]
    for g in range(SSD_GROUPS):
        st_ref[g] = sg[g]
    return out


def _split_bc(bc):
    n = SSD_STATE
    return ([bc[:, g * n:(g + 1) * n] for g in range(SSD_GROUPS)],
            [bc[:, (SSD_GROUPS + g) * n:(SSD_GROUPS + g + 1) * n] for g in range(SSD_GROUPS)])


def _ssd_bwd_sweep(xs_ref, bc_ref, dt_ref, dtb_ref, alog_ref, eb_ref, yo_ref, st_ref):
    order = list(reversed(range(_SSD_TILE_CHUNKS)))
    rm = {j: _dt_row_major(dt_ref[_rows(j), :], dtb_ref[...], alog_ref[...]) for j in order}
    fac = {j: _state_factors(*rm[j], eb_ref) for j in order}
    xs = {j: xs_ref[_rows(j), :].astype(F32) for j in order}
    bm = {j: _split_bc(bc_ref[_rows(j), :].astype(F32))[0] for j in order}
    cm_bf = {j: _split_bc(bc_ref[_rows(j), :])[1] for j in order}
    y_off = _state_sweep(st_ref, order, xs, bm, cm_bf, fac)
    for j in order:
        yo_ref[_rows(j), :] = y_off[j].astype(BF16)


def _ssd_body(xsf_ref, bcf_ref, dtf_ref, dtTf_ref, xsb_ref, bcb_ref, dtb_ref, bias_ref, alog_ref, biasT_ref, alogT_ref,
              dskip_ref, ef_ref, eb_ref, y_ref, yob_ref, stf_ref, stb_ref, *, lay):
    ns = _SSD_TILE_CHUNKS
    i = pl.program_id(0)
    n_steps = lay.n_blocks // ns

    @pl.when(_seq_pos(i * ns, lay)[0])
    def _():
        stf_ref[...] = jnp.zeros(stf_ref.shape, F32)

    @pl.when(_seq_pos((n_steps - 1 - i) * ns + ns - 1, lay)[1])
    def _():
        stb_ref[...] = jnp.zeros(stb_ref.shape, F32)

    L, nh = BLK, SSD_HEADS
    hpg = nh // SSD_GROUPS
    order = list(range(ns))
    lane = lax.broadcasted_iota(jnp.int32, (L, LANES), 1)
    xs_bf = {j: xsf_ref[_rows(j), :] for j in order}
    bc_bf = {j: bcf_ref[_rows(j), :] for j in order}
    bm_bf = {j: _split_bc(bc_bf[j])[0] for j in order}
    cm_bf = {j: _split_bc(bc_bf[j])[1] for j in order}
    cb = {j: [_dot_nt(cm_bf[j][g], bm_bf[j][g]) for g in range(SSD_GROUPS)] for j in order}
    rm = {j: _dt_row_major(dtf_ref[_rows(j), :], bias_ref[...], alog_ref[...]) for j in order}
    hm = {j: _dt_head_major(dtTf_ref[:, _rows(j)], biasT_ref[...], alogT_ref[...]) for j in order}

    _ssd_bwd_sweep(xsb_ref, bcb_ref, dtb_ref, bias_ref, alog_ref, eb_ref, yob_ref, stb_ref)

    fac = {j: _state_factors(*rm[j], ef_ref) for j in order}
    xs = {j: xs_bf[j].astype(F32) for j in order}
    bm, coef_exp = {}, {}
    for j in order:
        bm[j], cm = _split_bc(bc_bf[j].astype(F32))
        cbd = jnp.where((lane & (nh - 1)) < hpg, jnp.sum(cm[0] * bm[j][0], axis=1, keepdims=True),
                        jnp.sum(cm[1] * bm[j][1], axis=1, keepdims=True))
        coef_exp[j] = _expand(dskip_ref[...] + cbd * rm[j][0], eb_ref)
    y_off = _state_sweep(stf_ref, order, xs, bm, cm_bf, fac)

    causal = _tri(False)
    left = lane < SSD_HEAD_DIM
    zero_bf = jnp.zeros((L, LANES), BF16)
    acum2 = {j: rm[j][1] * _LOG2E for j in order}
    y_loc = {j: [] for j in order}
    for j in order:
        for k in range(nh // 2):
            g = (2 * k) // hpg
            lhs = []
            for h in (2 * k, 2 * k + 1):
                col_f = jnp.broadcast_to(acum2[j][:, h:h + 1], (L, L))
                col_b = jnp.broadcast_to(acum2[j][:, nh + h:nh + h + 1], (L, L))
                e = jnp.where(causal, col_f - hm[j][0][h:h + 1, :], col_b - hm[j][1][h:h + 1, :])
                lhs.append((cb[j][g] * jnp.exp2(e)).astype(BF16))
            xs_pair = xs_bf[j][:, k * LANES:(k + 1) * LANES]
            rhs = jnp.concatenate([jnp.where(left, xs_pair, zero_bf), jnp.where(left, zero_bf, xs_pair)], axis=0)
            y_loc[j].append(_dot(jnp.concatenate(lhs, axis=1), rhs))
    for j in order:
        y = jnp.concatenate(y_loc[j], axis=1) + coef_exp[j] * xs[j] + y_off[j]
        y_ref[_rows(j), :] = y.astype(BF16)


def _ssd(proj, dt, dtT, p, e_mats, lay):
    t = proj.shape[0]
    ns = _SSD_TILE_CHUNKS
    assert lay.per_seq_first % ns == 0 and lay.per_seq_second % ns == 0 and lay.n_first % ns == 0
    nb = lay.n_blocks // ns
    rows = ns * BLK
    xs_blk, bc_blk = COL_XS // SSD_INNER, COL_BC // SSD_BC_WIDTH
    state = pltpu.VMEM((SSD_GROUPS, SSD_STATE, SSD_INNER // SSD_GROUPS), F32)

    def rev(c):
        return nb - 1 - c

    return pl.pallas_call(
        functools.partial(_ssd_body, lay=lay),
        grid=(nb,),
        in_specs=[
            pl.BlockSpec((rows, SSD_INNER), lambda c: (c, xs_blk)),
            pl.BlockSpec((rows, SSD_BC_WIDTH), lambda c: (c, bc_blk)),
            pl.BlockSpec((rows, LANES), lambda c: (c, 0)),
            pl.BlockSpec((DT_REP, rows), lambda c: (0, c)),
            pl.BlockSpec((rows, SSD_INNER), lambda c: (rev(c), xs_blk)),
            pl.BlockSpec((rows, SSD_BC_WIDTH), lambda c: (rev(c), bc_blk)),
            pl.BlockSpec((rows, LANES), lambda c: (rev(c), 0)),
            _const_spec((1, LANES)), _const_spec((1, LANES)),
            _const_spec((DT_REP, 1)), _const_spec((DT_REP, 1)),
            _const_spec((1, LANES)),
            _const_spec((LANES, SSD_INNER)), _const_spec((LANES, SSD_INNER)),
        ],
        out_specs=[
            pl.BlockSpec((rows, SSD_INNER), lambda c: (c, 0)),
            pl.BlockSpec((rows, SSD_INNER), lambda c: (rev(c), 0)),
        ],
        out_shape=[jax.ShapeDtypeStruct((t, SSD_INNER), BF16), jax.ShapeDtypeStruct((t, SSD_INNER), BF16)],
        scratch_shapes=[state, state],
        compiler_params=_params(("arbitrary",)),
        name="ssd",
    )(proj, proj, dt, dtT, proj, proj, dt, p["dtb"], p["alog"], p["dtbT"], p["alogT"], p["dskip"],
      e_mats[0], e_mats[1])


_ROW_TILE = 512
_MERGE_COLS = 256


def _attn_merge_body(q_ref, kvp_ref, kvm_ref, kvn_ref, bias_ref, sink_ref,
                     xa_ref, xb_ref, gu_ref, gv_ref, yc_ref, yob_ref, z_ref, g0_ref, g1_ref, g2_ref, ng_ref, ws_ref, bs_ref,
                     wa_ref, wb_ref, wc_ref, wo_ref, ha_ref, hb_ref, ya_ref, m_ref, *, lay, n_a, n_tiles):
    i = pl.program_id(0)
    slot = lax.rem(i, 2)

    @pl.when(i == 0)
    def _():
        ya_ref[...] = jnp.zeros(ya_ref.shape, BF16)

    ya = ya_ref[1 - slot]
    gw = GMLP_WIDTH // GMLP_GROUPS
    chunks = []
    for c in range(_ROW_TILE // BLK):
        r = slice(c * BLK, (c + 1) * BLK)
        chunks.append(jnp.concatenate(
            [gu_ref[r, g * gw:(g + 1) * gw].astype(F32)
             * (_dot(ws_ref[g], gv_ref[r, g * gw:(g + 1) * gw]) + bs_ref[:, g * gw:(g + 1) * gw])
             for g in range(GMLP_GROUPS)], axis=1))
    yb = jnp.concatenate(chunks, axis=0).astype(BF16)
    y = (yc_ref[...].astype(F32) + yob_ref[...].astype(F32)) * z_ref[...].astype(F32)
    sw = SSD_INNER // SSD_GROUPS
    yn = jnp.concatenate([_rms(y[:, g * sw:(g + 1) * sw], ng_ref[:, g * sw:(g + 1) * sw])
                          for g in range(SSD_GROUPS)], axis=1).astype(BF16)

    acc = {}

    def piece(c, k):
        cols = slice(c * _MERGE_COLS, (c + 1) * _MERGE_COLS)

        def run():
            if k == 0:
                acc[c] = g0_ref[:, cols].astype(F32) * _dot(ya, wa_ref[:, cols])
            elif k == 1:
                acc[c] = acc[c] + g1_ref[:, cols].astype(F32) * _dot(yb, wb_ref[:, cols])
            else:
                m_ref[:, cols] = (acc[c] + g2_ref[:, cols].astype(F32) * _dot(yn, wc_ref[:, cols])).astype(BF16)
        return run

    pieces = [piece(c, k) for c in range(D_MODEL // _MERGE_COLS) for k in range(3)]
    n_items = _ATTN_TILE_BLOCKS * ATTN_HEADS // 2
    done = [0]

    def after_item(idx):
        while done[0] < (idx + 1) * len(pieces) // n_items:
            pieces[done[0]]()
            done[0] += 1

    def put_block(jb, val):
        ya_ref[slot, jb * BLK:(jb + 1) * BLK, :] = val

    _attention_tile(jnp.minimum(i, n_tiles - 1), q_ref, kvp_ref, kvm_ref, kvn_ref, bias_ref, sink_ref,
                    put_block, after_item, lay)

    m = m_ref[...]
    in_first = jnp.maximum(i - 1, 0) < n_a

    @pl.when(in_first)
    def _():
        ha_ref[...] = xa_ref[...] + _dot(m, wo_ref[...])

    @pl.when(jnp.logical_not(in_first))
    def _():
        hb_ref[...] = xb_ref[...] + _dot(m, wo_ref[...])


def _attn_merge(xa, xb, yc, yob, proj, bias, p, lay):
    tm = _ROW_TILE
    nt = _ATTN_TILE_BLOCKS
    assert tm == nt * BLK
    assert lay.per_seq_first % nt == 0 and lay.per_seq_second % nt == 0 and lay.n_first % nt == 0
    n_a, n_b = xa.shape[0] // tm, xb.shape[0] // tm
    n_tiles = n_a + n_b
    g_blk = COL_GATE // D_MODEL
    kv_w = 2 * ATTN_KV_WIDTH
    kv_blk = COL_KV // kv_w
    last_blk = lay.n_blocks - 1

    def att(i):
        return jnp.minimum(i, n_tiles - 1)

    def mrg(i):
        return jnp.maximum(i - 1, 0)

    def row(width, col_blk=0):
        return pl.BlockSpec((tm, width), lambda i: (mrg(i), col_blk))

    stream = (pl.BlockSpec((tm, D_MODEL), lambda i: (jnp.minimum(mrg(i), n_a - 1), 0)),
              pl.BlockSpec((tm, D_MODEL), lambda i: (jnp.clip(mrg(i) - n_a, 0, n_b - 1), 0)))
    return pl.pallas_call(
        functools.partial(_attn_merge_body, lay=lay, n_a=n_a, n_tiles=n_tiles),
        grid=(n_tiles + 1,),
        in_specs=[
            pl.BlockSpec((tm, ATTN_Q_WIDTH), lambda i: (att(i), COL_Q // ATTN_Q_WIDTH)),
            pl.BlockSpec((BLK, kv_w), lambda i: (jnp.maximum(att(i) * nt - 1, 0), kv_blk)),
            pl.BlockSpec((tm, kv_w), lambda i: (att(i), kv_blk)),
            pl.BlockSpec((BLK, kv_w), lambda i: (jnp.minimum(att(i) * nt + nt, last_blk), kv_blk)),
            _const_spec((ATTN_HEADS // 2, 3 * BLK, 2 * BLK)),
            _const_spec((ATTN_HEADS // 2, 2 * BLK)),
            *stream, row(GMLP_WIDTH, COL_GU // GMLP_WIDTH), row(GMLP_WIDTH, COL_GV // GMLP_WIDTH),
            row(SSD_INNER), row(SSD_INNER), row(SSD_INNER, COL_Z // SSD_INNER),
            row(D_MODEL, g_blk), row(D_MODEL, g_blk + 1), row(D_MODEL, g_blk + 2),
            _const_spec((1, SSD_INNER)),
            _const_spec((GMLP_GROUPS, BLK, BLK)), _const_spec((BLK, GMLP_WIDTH)),
            _const_spec((ATTN_Q_WIDTH, D_MODEL)), _const_spec((GMLP_WIDTH, D_MODEL)),
            _const_spec((SSD_INNER, D_MODEL)), _const_spec((D_MODEL, D_MODEL)),
        ],
        out_specs=list(stream),
        out_shape=[jax.ShapeDtypeStruct(xa.shape, F32), jax.ShapeDtypeStruct(xb.shape, F32)],
        scratch_shapes=[pltpu.VMEM((2, tm, ATTN_Q_WIDTH), BF16), pltpu.VMEM((tm, D_MODEL), BF16)],
        compiler_params=_params(("arbitrary",)),
        name="attn_merge",
    )(proj, proj, proj, proj, bias, p["sink"],
      xa, xb, proj, proj, yc, yob, proj, proj, proj, proj, p["ssd_norm_g"], p["w_s"], p["b_s"],
      p["w_a"], p["w_b"], p["w_c"], p["w_o"])


_FF_CHUNK = 1024


def _ffn_body(ha_ref, hb_ref, g_ref, w1_ref, w2_ref, oa_ref, ob_ref, *, n_a):
    def run(h_ref, o_ref):
        h = h_ref[...]
        hn = _rms(h, g_ref[...]).astype(BF16)
        acc = h
        for c0 in range(0, FF_DIM, _FF_CHUNK):
            a = jnp.maximum(_dot(hn, w1_ref[:, c0:c0 + _FF_CHUNK]), 0.0)
            acc = acc + _dot((a * a).astype(BF16), w2_ref[c0:c0 + _FF_CHUNK, :])
        o_ref[...] = acc

    in_first = pl.program_id(0) < n_a
    pl.when(in_first)(lambda: run(ha_ref, oa_ref))
    pl.when(jnp.logical_not(in_first))(lambda: run(hb_ref, ob_ref))


def _ffn(ha, hb, g, w1, w2):
    tm = _ROW_TILE
    n_a, n_b = ha.shape[0] // tm, hb.shape[0] // tm
    stream = _two_stream_specs(tm, n_a, n_b)
    return pl.pallas_call(
        functools.partial(_ffn_body, n_a=n_a),
        grid=(n_a + n_b,),
        in_specs=[*stream, _const_spec((1, D_MODEL)), _const_spec((D_MODEL, FF_DIM)), _const_spec((FF_DIM, D_MODEL))],
        out_specs=list(stream),
        out_shape=[jax.ShapeDtypeStruct(ha.shape, F32), jax.ShapeDtypeStruct(hb.shape, F32)],
        compiler_params=_params(("arbitrary",)),
        name="ffn",
    )(ha, hb, g, w1, w2)


def _t5_bucket(rel):
    nb = NUM_BUCKETS // 2
    max_exact = nb // 2
    ret = jnp.where(rel > 0, nb, 0)
    n = jnp.abs(rel)
    n_safe = jnp.maximum(n, 1).astype(F32)
    large = max_exact + (jnp.log(n_safe / max_exact) / math.log(MAX_DISTANCE / max_exact)
                         * (nb - max_exact)).astype(jnp.int32)
    large = jnp.minimum(large, nb - 1)
    return ret + jnp.where(n < max_exact, n, large)


def _bias_table(rel_bias):
    qi = jnp.arange(BLK)[:, None]
    ki = jnp.arange(3 * BLK)[None, :]
    rel = ki - BLK - qi
    onehot = (_t5_bucket(rel)[..., None] == jnp.arange(NUM_BUCKETS)).astype(F32)
    bias = jnp.einsum("qkb,bh->hkq", onehot, rel_bias.astype(F32), precision=lax.Precision.HIGHEST) * _LOG2E
    bias = jnp.where((jnp.abs(rel) <= WINDOW).T[None], bias, NEG_INF)
    return jnp.concatenate([bias[0::2], bias[1::2]], axis=-1)


def _expand_matrix(d):
    k = np.arange(LANES)[:, None]
    col = np.arange(SSD_INNER)[None, :]
    hit = ((k % DT_REP) == d * SSD_HEADS + col // SSD_HEAD_DIM)
    return jnp.asarray(hit, dtype=BF16)


def _prep_layer_params(w_in, norm_mix_g, q_norm_g, k_norm_g, attn_sink, gmlp_ln_g, gmlp_ln_b, w_spatial,
                       b_spatial, conv_w, conv_b, dt_bias, a_log, d_skip, ssd_norm_g, w_up_attn, w_up_gmlp,
                       w_up_ssd, w_out, norm_ff_g, w_ff1, w_ff2):
    depth = w_in.shape[0]

    def cols(a, b):
        return w_in[:, :, a:b]

    w_main = jnp.concatenate([
        cols(_R_Z, _R_XS), cols(_R_XS, _R_B), cols(_R_GATE, _R_END), cols(_R_Q, _R_K),
        cols(_R_GU, _R_GV), cols(_R_GV, _R_Z), cols(_R_B, _R_DT), cols(_R_K, _R_GU)], axis=-1).astype(BF16)
    w_dt = cols(_R_DT, _R_GATE)
    rep = LANES // DT_REP
    gw = GMLP_WIDTH // GMLP_GROUPS

    def dt_lanes(v):
        return jnp.tile(v.reshape(depth, 1, DT_REP), (1, 1, rep))

    return dict(
        norm_mix_g=norm_mix_g[:, None, :],
        w_main=w_main,
        w_dt=jnp.tile(w_dt, (1, 1, rep)).astype(BF16),
        w_dtT=jnp.swapaxes(w_dt, 1, 2).astype(BF16),
        kscale=jnp.tile(q_norm_g * k_norm_g * (HEAD_DIM ** -0.5 * _LOG2E), (1, ATTN_KV_HEADS))[:, None, :],
        sink=jnp.repeat(attn_sink * _LOG2E, BLK, axis=-1).reshape(depth, ATTN_HEADS // 2, 2 * BLK),
        ln_g=gmlp_ln_g[:, None, :], ln_b=gmlp_ln_b[:, None, :],
        w_s=w_spatial.astype(BF16),
        b_s=jnp.repeat(jnp.swapaxes(b_spatial, 1, 2), gw, axis=-1),
        conv_w=conv_w, conv_b=conv_b[:, None, :],
        dtb=dt_lanes(dt_bias), alog=dt_lanes(a_log),
        dtbT=dt_bias.reshape(depth, DT_REP, 1), alogT=a_log.reshape(depth, DT_REP, 1),
        dskip=dt_lanes(jnp.concatenate([d_skip, d_skip], axis=-1)),
        ssd_norm_g=ssd_norm_g[:, None, :],
        w_a=w_up_attn.astype(BF16), w_b=w_up_gmlp.astype(BF16), w_c=w_up_ssd.astype(BF16),
        w_o=w_out.astype(BF16),
        norm_ff_g=norm_ff_g[:, None, :],
        w1=w_ff1.astype(BF16), w2=w_ff2.astype(BF16),
    )


def _layer(xa, xb, p, bias, e_mats, lay):
    proj, dt, dtT = _inproj(xa, xb, p, lay)
    yc, yob = _ssd(proj, dt, dtT, p, e_mats, lay)
    ha, hb = _attn_merge(xa, xb, yc, yob, proj, bias, p, lay)
    return _ffn(ha, hb, p["norm_ff_g"], p["w1"], p["w2"])


def _encoder(x_first, x_second, rel_bias, layer_params):
    b1, s1, _ = x_first.shape
    b2, s2, _ = x_second.shape
    t1, t2 = b1 * s1, b2 * s2
    lay = Layout(n_blocks=(t1 + t2) // BLK, n_first=t1 // BLK, per_seq_first=s1 // BLK,
                 per_seq_second=s2 // BLK)
    bias = _bias_table(rel_bias)
    e_mats = (_expand_matrix(0), _expand_matrix(1))
    params = _prep_layer_params(**layer_params)

    xa, xb = x_first.reshape(t1, D_MODEL), x_second.reshape(t2, D_MODEL)
    for layer in range(layer_params["w_in"].shape[0]):
        xa, xb = _layer(xa, xb, {k: v[layer] for k, v in params.items()}, bias, e_mats, lay)
    return xa.reshape(b1, s1, D_MODEL), xb.reshape(b2, s2, D_MODEL)


def kernel(x_prompt, x_sample, rel_bias, norm_mix_g, w_in, q_norm_g, k_norm_g, attn_sink, gmlp_ln_g, gmlp_ln_b,
           w_spatial, b_spatial, conv_w, conv_b, dt_bias, a_log, d_skip, ssd_norm_g, w_up_attn, w_up_gmlp,
           w_up_ssd, w_out, norm_ff_g, w_ff1, w_ff2):
    layer_params = dict(
        w_in=w_in, norm_mix_g=norm_mix_g, q_norm_g=q_norm_g, k_norm_g=k_norm_g, attn_sink=attn_sink,
        gmlp_ln_g=gmlp_ln_g, gmlp_ln_b=gmlp_ln_b, w_spatial=w_spatial, b_spatial=b_spatial, conv_w=conv_w,
        conv_b=conv_b, dt_bias=dt_bias, a_log=a_log, d_skip=d_skip, ssd_norm_g=ssd_norm_g,
        w_up_attn=w_up_attn, w_up_gmlp=w_up_gmlp, w_up_ssd=w_up_ssd, w_out=w_out, norm_ff_g=norm_ff_g,
        w_ff1=w_ff1, w_ff2=w_ff2)
    return _encoder(x_prompt, x_sample, rel_bias, layer_params)
```
